```python
import jax, jax.numpy as jnp
from jax import lax
import numpy as np

D_MODEL = 2048
BATCH = 2
SEQ = 4096
DEPTH = 4

HEAD_DIM = 128
N_HEADS = D_MODEL // HEAD_DIM
DIL_CONFIGS = ((128, 1), (512, 4), (2048, 16))
N_DIL = len(DIL_CONFIGS)
GM_CHUNK = 128
GM_WIDTH = D_MODEL
GM_GROUP_DIM = 128
GM_GROUPS = GM_WIDTH // GM_GROUP_DIM
D_FF = 7 * D_MODEL // 2
N_EXPERTS = 8
TOP_K = 2
N_ATTN_LAYERS = (DEPTH + 1) // 2
N_GMLP_LAYERS = DEPTH // 2
EPS = 1e-6
NEG_INF = -1e30

kernel_name = "hybrid_dilated_attn_gmlp_moe"


def rmsnorm(x, g):
    xf = x.astype(jnp.float32)
    y = xf * lax.rsqrt(jnp.mean(xf * xf, axis=-1, keepdims=True) + EPS)
    return (y * g.astype(jnp.float32)).astype(x.dtype)


def alibi_slopes():
    n = N_DIL * N_HEADS
    s = jnp.exp2(-8.0 * jnp.arange(1, n + 1, dtype=jnp.float32) / n)
    return s.reshape(N_HEADS, N_DIL).T


def dilated_branch(q, k, v, dil, blk, slopes):
    B, S, H, Dh = q.shape
    span = dil * blk
    L = -(-S // span) * span
    n = L // dil
    nb = n // blk

    def to_blocks(a):
        a = jnp.pad(a, ((0, 0), (0, L - S), (0, 0), (0, 0)))
        a = a.reshape(B, n, dil, H, Dh).transpose(0, 2, 1, 3, 4)
        return a.reshape(B, dil, nb, blk, H, Dh)

    def with_prev(a):
        prev = jnp.pad(a, ((0, 0), (0, 0), (1, 0), (0, 0), (0, 0), (0, 0)))[:, :, :-1]
        return jnp.concatenate([prev, a], axis=3)

    qb = to_blocks(q)
    kk = with_prev(to_blocks(k))
    vv = with_prev(to_blocks(v))

    s = jnp.einsum('brcihd,brcjhd->brchij', qb, kk,
                   preferred_element_type=jnp.float32) * (Dh ** -0.5)
    i = jnp.arange(blk)[:, None]
    j = jnp.arange(2 * blk)[None, :]
    delta = i + blk - j
    in_band = (delta >= 0) & (delta <= blk)
    has_prev = (jnp.arange(nb) > 0)[:, None, None] | (j >= blk)[None]
    mask = in_band[None] & has_prev
    bias = -slopes[:, None, None] * (dil * delta).astype(jnp.float32)[None]
    s = jnp.where(mask[:, None], s + bias, NEG_INF)

    m = jnp.max(s, axis=-1, keepdims=True)
    p = jnp.exp(s - m)
    den = jnp.sum(p, axis=-1, keepdims=True)
    o = jnp.einsum('brchij,brcjhd->brchid', p, vv.astype(jnp.float32)) / den
    lse = (m + jnp.log(den))[..., 0]

    def from_blocks(a):
        a = jnp.moveaxis(a, 3, 4)
        a = a.reshape(B, dil, n, *a.shape[4:])
        a = jnp.moveaxis(a, 1, 2)
        return a.reshape(B, L, *a.shape[3:])[:, :S]

    return from_blocks(o), from_blocks(lse)


def dilated_attention(h, w_in, w_out):
    B, S, D = h.shape
    qkv = (h @ w_in).reshape(B, S, N_DIL, 3, N_HEADS, HEAD_DIM)
    slopes = alibi_slopes()
    outs, lses = [], []
    for g, (win, dil) in enumerate(DIL_CONFIGS):
        o, lse = dilated_branch(qkv[:, :, g, 0], qkv[:, :, g, 1], qkv[:, :, g, 2],
                                dil, win // dil, slopes[g])
        outs.append(o)
        lses.append(lse)
    w = jax.nn.softmax(jnp.stack(lses), axis=0)
    o = jnp.sum(w[..., None] * jnp.stack(outs), axis=0)
    return o.reshape(B, S, D).astype(h.dtype) @ w_out


def chunked_gmlp(h, w_in, v_norm_g, w_s, b_s, w_out):
    B, S, _ = h.shape
    z = jax.nn.gelu(h @ w_in)
    u, v = jnp.split(z, 2, axis=-1)
    v = rmsnorm(v, v_norm_g)
    vc = v.reshape(B, S // GM_CHUNK, GM_CHUNK, GM_GROUPS, GM_GROUP_DIM)
    causal = jnp.tril(jnp.ones((GM_CHUNK, GM_CHUNK), dtype=bool))
    ws = jnp.where(causal[None], w_s, jnp.zeros_like(w_s))
    s = jnp.einsum('gij,bcjgd->bcigd', ws, vc) + b_s.T[None, None, :, :, None]
    return (u * s.reshape(B, S, GM_WIDTH)) @ w_out


def swiglu(t, w_gate, w_up, w_down):
    return (jax.nn.silu(t @ w_gate) * (t @ w_up)) @ w_down


def moe_swiglu(h, router_w, w_gate, w_up, w_down):
    B, S, D = h.shape
    t = h.reshape(B * S, D)
    logits = jnp.matmul(t, router_w, preferred_element_type=jnp.float32)
    top_v, top_i = lax.top_k(logits, TOP_K)
    gates = jax.nn.softmax(top_v, axis=-1)
    combine = jnp.sum(jax.nn.one_hot(top_i, N_EXPERTS, dtype=jnp.float32) * gates[..., None], axis=1)
    out = jnp.zeros((B * S, D), jnp.float32)
    for e in range(N_EXPERTS):
        out = out + combine[:, e:e + 1] * swiglu(t, w_gate[e], w_up[e], w_down[e])
    return out.astype(h.dtype).reshape(B, S, D)


def setup_inputs(seed: int = 0) -> dict:
    key = jax.random.key(seed)
    ks = jax.random.split(key, 20)
    f32 = jnp.float32

    def nrm(k, shape, scale):
        return jax.random.normal(k, shape, f32) * scale

    D = D_MODEL
    return {
        "x": nrm(ks[0], (BATCH, SEQ, D), 1.0),
        "mix_norm_g": 1.0 + nrm(ks[1], (DEPTH, D), 0.02),
        "ffn_norm_g": 1.0 + nrm(ks[2], (DEPTH, D), 0.02),
        "attn_w_in": nrm(ks[3], (N_ATTN_LAYERS, D, N_DIL * 3 * N_HEADS * HEAD_DIM), D ** -0.5),
        "attn_w_out": nrm(ks[4], (N_ATTN_LAYERS, N_HEADS * HEAD_DIM, D), (N_HEADS * HEAD_DIM) ** -0.5),
        "gm_w_in": nrm(ks[5], (N_GMLP_LAYERS, D, 2 * GM_WIDTH), D ** -0.5),
        "gm_v_norm_g": 1.0 + nrm(ks[6], (N_GMLP_LAYERS, GM_WIDTH), 0.02),
        "gm_w_s": nrm(ks[7], (N_GMLP_LAYERS, GM_GROUPS, GM_CHUNK, GM_CHUNK), GM_CHUNK ** -0.5),
        "gm_b_s": 1.0 + nrm(ks[8], (N_GMLP_LAYERS, GM_GROUPS, GM_CHUNK), 0.02),
        "gm_w_out": nrm(ks[9], (N_GMLP_LAYERS, GM_WIDTH, D), GM_WIDTH ** -0.5),
        "dense_w_gate": nrm(ks[10], (N_ATTN_LAYERS, D, D_FF), D ** -0.5),
        "dense_w_up": nrm(ks[11], (N_ATTN_LAYERS, D, D_FF), D ** -0.5),
        "dense_w_down": nrm(ks[12], (N_ATTN_LAYERS, D_FF, D), D_FF ** -0.5),
        "router_w": nrm(ks[13], (N_GMLP_LAYERS, D, N_EXPERTS), D ** -0.5),
        "moe_w_gate": nrm(ks[14], (N_GMLP_LAYERS, N_EXPERTS, D, D_FF), D ** -0.5),
        "moe_w_up": nrm(ks[15], (N_GMLP_LAYERS, N_EXPERTS, D, D_FF), D ** -0.5),
        "moe_w_down": nrm(ks[16], (N_GMLP_LAYERS, N_EXPERTS, D_FF, D), D_FF ** -0.5),
        "final_norm_g": 1.0 + nrm(ks[17], (D,), 0.02),
    }


def reference(x, mix_norm_g, ffn_norm_g, attn_w_in, attn_w_out, gm_w_in, gm_v_norm_g,
              gm_w_s, gm_b_s, gm_w_out, dense_w_gate, dense_w_up, dense_w_down,
              router_w, moe_w_gate, moe_w_up, moe_w_down, final_norm_g):
    h = x
    for i in range(DEPTH):
        j = i // 2
        a = rmsnorm(h, mix_norm_g[i])
        if i % 2 == 0:
            h = h + dilated_attention(a, attn_w_in[j], attn_w_out[j])
        else:
            h = h + chunked_gmlp(a, gm_w_in[j], gm_v_norm_g[j], gm_w_s[j], gm_b_s[j], gm_w_out[j])
        f = rmsnorm(h, ffn_norm_g[i])
        if i % 2 == 0:
            h = h + swiglu(f, dense_w_gate[j], dense_w_up[j], dense_w_down[j])
        else:
            h = h + moe_swiglu(f, router_w[j], moe_w_gate[j], moe_w_up[j], moe_w_down[j])
    return rmsnorm(h, final_norm_g)
```

```python
import functools

import jax
import jax.numpy as jnp
from jax import lax
from jax.experimental import pallas as pl
from jax.experimental.pallas import tpu as pltpu

D_MODEL = 2048
BATCH = 2
SEQ = 4096
DEPTH = 4
N_TOK = BATCH * SEQ
HEAD_DIM = 128
N_HEADS = D_MODEL // HEAD_DIM
DIL_CONFIGS = ((128, 1), (512, 4), (2048, 16))
N_DIL = len(DIL_CONFIGS)
ATT_BLK = 128
QKV_GROUP_COLS = 3 * N_HEADS * HEAD_DIM
GM_CHUNK = 128
GM_WIDTH = D_MODEL
GM_GROUP_DIM = 128
GM_GROUPS = GM_WIDTH // GM_GROUP_DIM
D_FF = 7 * D_MODEL // 2
N_EXPERTS = 8
EPS = 1e-6
NEG_INF = -1e30

LANES = 128
VMEM_LIMIT = 56 * 1024 * 1024
CAST_ROWS = 256

F32 = jnp.float32
BF16 = jnp.bfloat16


def _params(n_axes):
    return pltpu.CompilerParams(
        dimension_semantics=("arbitrary",) * n_axes, vmem_limit_bytes=VMEM_LIMIT)


def _cast_weight(w_ref, wb_ref):
    def body(c, carry):
        r = pl.multiple_of(c * CAST_ROWS, CAST_ROWS)
        wb_ref[pl.ds(r, CAST_ROWS), :] = w_ref[pl.ds(r, CAST_ROWS), :].astype(BF16)
        return carry
    lax.fori_loop(0, w_ref.shape[0] // CAST_ROWS, body, 0)


def _gelu_tanh(x):
    return 0.5 * x * (1.0 + jnp.tanh(0.7978845608028654 * (x + 0.044715 * (x * x * x))))


def _rms_kernel(x_ref, g_ref, o_ref):
    x = x_ref[...]
    ms = jnp.mean(x * x, axis=-1, keepdims=True)
    o_ref[...] = ((x * lax.rsqrt(ms + EPS)) * g_ref[...]).astype(o_ref.dtype)


def _rmsnorm(h, g3, layer, out_dtype, tm=512):
    return pl.pallas_call(
        _rms_kernel,
        grid=(N_TOK // tm,),
        in_specs=[pl.BlockSpec((tm, D_MODEL), lambda i: (i, 0)),
                  pl.BlockSpec((None, 1, D_MODEL), lambda i: (layer, 0, 0))],
        out_specs=pl.BlockSpec((tm, D_MODEL), lambda i: (i, 0)),
        out_shape=jax.ShapeDtypeStruct((N_TOK, D_MODEL), out_dtype),
        compiler_params=_params(1),
        name="rmsnorm",
    )(h, g3)


def _w_spec(prefix, k, tn, col_off):
    shape = (None,) * len(prefix) + (k, tn)
    return pl.BlockSpec(shape, lambda j, i: (*prefix, 0, col_off + j))


def _mm_act_kernel(a_ref, w_ref, o_ref, wb_ref, *, act):
    @pl.when(pl.program_id(1) == 0)
    def _():
        _cast_weight(w_ref, wb_ref)
    acc = jnp.dot(a_ref[...], wb_ref[...], preferred_element_type=F32)
    if act == "gelu":
        acc = _gelu_tanh(acc)
    o_ref[...] = acc.astype(o_ref.dtype)


def _mm_act(a, w, prefix, n_cols, act, tm=1024, tn=1024, col_off=0):
    k = a.shape[1]
    return pl.pallas_call(
        functools.partial(_mm_act_kernel, act=act),
        grid=(n_cols // tn, N_TOK // tm),
        in_specs=[pl.BlockSpec((tm, k), lambda j, i: (i, 0)),
                  _w_spec(prefix, k, tn, col_off)],
        out_specs=pl.BlockSpec((tm, tn), lambda j, i: (i, j)),
        out_shape=jax.ShapeDtypeStruct((N_TOK, n_cols), BF16),
        scratch_shapes=[pltpu.VMEM((k, tn), BF16)],
        compiler_params=_params(2),
        name="mm_act",
    )(a, w)


def _mm_perm_kernel(a_ref, w_ref, o_ref, wb_ref, acc_ref, *, dil):
    @pl.when(pl.program_id(1) == 0)
    def _():
        _cast_weight(w_ref, wb_ref)
    acc = jnp.dot(a_ref[...], wb_ref[...], preferred_element_type=F32)
    rows = acc_ref.shape[1] // dil
    for cb in range(acc_ref.shape[0]):
        cols = slice(cb * LANES, (cb + 1) * LANES)
        acc_ref[cb] = acc[:, cols]
        for r in range(dil):
            o_ref[r, :, cols] = acc_ref[cb, pl.ds(r, rows, stride=dil), :].astype(o_ref.dtype)


def _mm_perm(a, w, prefix, n_cols, dil, col_off, tm=1024, tn=1024):
    k = a.shape[1]
    tiles_per_b = SEQ // tm
    out = pl.pallas_call(
        functools.partial(_mm_perm_kernel, dil=dil),
        grid=(n_cols // tn, N_TOK // tm),
        in_specs=[pl.BlockSpec((tm, k), lambda j, i: (i, 0)),
                  _w_spec(prefix, k, tn, col_off)],
        out_specs=pl.BlockSpec((None, dil, tm // dil, tn),
                               lambda j, i: (i // tiles_per_b, 0, i % tiles_per_b, j)),
        out_shape=jax.ShapeDtypeStruct((BATCH, dil, SEQ // dil, n_cols), BF16),
        scratch_shapes=[pltpu.VMEM((k, tn), BF16), pltpu.VMEM((tn // LANES, tm, LANES), F32)],
        compiler_params=_params(2),
        name="mm_perm",
    )(a, w)
    return out.reshape(BATCH, SEQ, n_cols)


def _mm_res_kernel(a_ref, w_ref, res_ref, *rest, scale_col):
    if scale_col is None:
        o_ref, wb_ref = rest
    else:
        sc_ref, o_ref, wb_ref = rest

    @pl.when(pl.program_id(1) == 0)
    def _():
        _cast_weight(w_ref, wb_ref)
    acc = jnp.dot(a_ref[...], wb_ref[...], preferred_element_type=F32)
    if scale_col is not None:
        acc = acc * sc_ref[:, scale_col:scale_col + 1]
    o_ref[...] = res_ref[...] + acc


def _mm_res(a, w, prefix, res, tm, tn, scale=None, scale_col=None):
    k = a.shape[1]
    in_specs = [pl.BlockSpec((tm, k), lambda j, i: (i, 0)),
                _w_spec(prefix, k, tn, 0),
                pl.BlockSpec((tm, tn), lambda j, i: (i, j))]
    args = [a, w, res]
    if scale is not None:
        in_specs.append(pl.BlockSpec((tm, scale.shape[1]), lambda j, i: (i, 0)))
        args.append(scale)
    return pl.pallas_call(
        functools.partial(_mm_res_kernel, scale_col=scale_col),
        grid=(D_MODEL // tn, N_TOK // tm),
        in_specs=in_specs,
        out_specs=pl.BlockSpec((tm, tn), lambda j, i: (i, j)),
        out_shape=jax.ShapeDtypeStruct((N_TOK, D_MODEL), F32),
        scratch_shapes=[pltpu.VMEM((k, tn), BF16)],
        compiler_params=_params(2),
        name="mm_res",
    )(*args)


def _mm_swiglu_kernel(a_ref, wg_ref, wu_ref, o_ref, wgb_ref, wub_ref):
    @pl.when(pl.program_id(1) == 0)
    def _():
        _cast_weight(wg_ref, wgb_ref)
        _cast_weight(wu_ref, wub_ref)
    a = a_ref[...]
    g = jnp.dot(a, wgb_ref[...], preferred_element_type=F32)
    u = jnp.dot(a, wub_ref[...], preferred_element_type=F32)
    o_ref[...] = ((g * jax.nn.sigmoid(g)) * u).astype(o_ref.dtype)


def _mm_swiglu(a, wg, wu, prefix, tm=1024, tn=512):
    k = a.shape[1]
    return pl.pallas_call(
        _mm_swiglu_kernel,
        grid=(D_FF // tn, N_TOK // tm),
        in_specs=[pl.BlockSpec((tm, k), lambda j, i: (i, 0)),
                  _w_spec(prefix, k, tn, 0), _w_spec(prefix, k, tn, 0)],
        out_specs=pl.BlockSpec((tm, tn), lambda j, i: (i, j)),
        out_shape=jax.ShapeDtypeStruct((N_TOK, D_FF), BF16),
        scratch_shapes=[pltpu.VMEM((k, tn), BF16), pltpu.VMEM((k, tn), BF16)],
        compiler_params=_params(2),
        name="mm_swiglu",
    )(a, wg, wu)


def _attn_kernel(slopes_ref, *refs):
    qkv_refs = refs[:9]
    o_ref, tabp_ref, tabc_ref, oscr_ref, lscr_ref = refs[9:]
    h = pl.program_id(1)
    ii = lax.broadcasted_iota(jnp.int32, (ATT_BLK, ATT_BLK), 0)
    jj = lax.broadcasted_iota(jnp.int32, (ATT_BLK, ATT_BLK), 1)
    scale = HEAD_DIM ** -0.5
    nt = (((1,), (1,)), ((), ()))
    for g, (win, dil) in enumerate(DIL_CONFIGS):
        q_ref, k_ref, v_ref = qkv_refs[3 * g:3 * g + 3]
        nb = SEQ // dil // ATT_BLK
        slope = slopes_ref[g, h]
        tabp_ref[...] = jnp.where(
            jj >= ii, -slope * (dil * (ii + ATT_BLK - jj)).astype(F32), NEG_INF)
        tabc_ref[...] = jnp.where(
            jj <= ii, -slope * (dil * (ii - jj)).astype(F32), NEG_INF)

        def body(c, carry, q_ref=q_ref, k_ref=k_ref, v_ref=v_ref, nb=nb, dil=dil, g=g):
            r0 = pl.multiple_of(c * ATT_BLK, ATT_BLK)
            rp = pl.multiple_of(jnp.maximum(c - 1, 0) * ATT_BLK, ATT_BLK)
            q = q_ref[pl.ds(r0, ATT_BLK), :]
            cc = c % nb
            res = c // nb
            sc = lax.dot_general(q, k_ref[pl.ds(r0, ATT_BLK), :], nt,
                                 preferred_element_type=F32) * scale + tabc_ref[...]
            sp = lax.dot_general(q, k_ref[pl.ds(rp, ATT_BLK), :], nt,
                                 preferred_element_type=F32) * scale + tabp_ref[...]
            sp = jnp.where(cc > 0, sp, NEG_INF)
            m = jnp.maximum(jnp.max(sc, axis=-1, keepdims=True),
                            jnp.max(sp, axis=-1, keepdims=True))
            pc = jnp.exp(sc - m)
            pp = jnp.exp(sp - m)
            den = jnp.sum(pc, axis=-1, keepdims=True) + jnp.sum(pp, axis=-1, keepdims=True)
            o = (jnp.dot(pc.astype(BF16), v_ref[pl.ds(r0, ATT_BLK), :],
                         preferred_element_type=F32)
                 + jnp.dot(pp.astype(BF16), v_ref[pl.ds(rp, ATT_BLK), :],
                           preferred_element_type=F32)) / den
            lse = jnp.broadcast_to(m + jnp.log(den), (ATT_BLK, HEAD_DIM))
            start = cc * (ATT_BLK * dil) + res
            if dil == 1:
                dst = pl.ds(pl.multiple_of(start, ATT_BLK), ATT_BLK)
            else:
                dst = pl.ds(start, ATT_BLK, stride=dil)
            oscr_ref[g, dst, :] = o
            lscr_ref[g, dst, :] = lse
            return carry

        lax.fori_loop(0, SEQ // ATT_BLK, body, 0)

    rows = 256

    def merge(t, carry):
        sl = pl.ds(pl.multiple_of(t * rows, rows), rows)
        l0 = lscr_ref[0, sl, :]
        l1 = lscr_ref[1, sl, :]
        l2 = lscr_ref[2, sl, :]
        m = jnp.maximum(jnp.maximum(l0, l1), l2)
        e0 = jnp.exp(l0 - m)
        e1 = jnp.exp(l1 - m)
        e2 = jnp.exp(l2 - m)
        den = e0 + e1 + e2
        o = (e0 * oscr_ref[0, sl, :] + e1 * oscr_ref[1, sl, :] + e2 * oscr_ref[2, sl, :]) / den
        o_ref[sl, :] = o.astype(o_ref.dtype)
        return carry

    lax.fori_loop(0, SEQ // rows, merge, 0)


def _attention(qkv_groups, slopes):
    in_specs = [pl.BlockSpec(memory_space=pltpu.SMEM)]
    args = [slopes]
    for qkv in qkv_groups:
        for part in range(3):
            in_specs.append(pl.BlockSpec(
                (None, SEQ, HEAD_DIM), lambda b, h, part=part: (b, 0, part * N_HEADS + h)))
            args.append(qkv)
    return pl.pallas_call(
        _attn_kernel,
        grid=(BATCH, N_HEADS),
        in_specs=in_specs,
        out_specs=pl.BlockSpec((None, SEQ, HEAD_DIM), lambda b, h: (b, 0, h)),
        out_shape=jax.ShapeDtypeStruct((BATCH, SEQ, D_MODEL), BF16),
        scratch_shapes=[pltpu.VMEM((ATT_BLK, ATT_BLK), F32),
                        pltpu.VMEM((ATT_BLK, ATT_BLK), F32),
                        pltpu.VMEM((N_DIL, SEQ, HEAD_DIM), F32),
                        pltpu.VMEM((N_DIL, SEQ, HEAD_DIM), F32)],
        compiler_params=_params(2),
        name="dilated_attn",
    )(*args)


def _gm_spatial_kernel(z_ref, vg_ref, ws_ref, bt_ref, y_ref, wsb_ref):
    @pl.when(pl.program_id(0) == 0)
    def _():
        ii = lax.broadcasted_iota(jnp.int32, (GM_CHUNK, GM_CHUNK), 0)
        jj = lax.broadcasted_iota(jnp.int32, (GM_CHUNK, GM_CHUNK), 1)
        for g in range(GM_GROUPS):
            wsb_ref[g] = jnp.where(ii >= jj, ws_ref[g], 0.0).astype(BF16)

    for c in range(z_ref.shape[0] // GM_CHUNK):
        rows = slice(c * GM_CHUNK, (c + 1) * GM_CHUNK)
        v = z_ref[rows, GM_WIDTH:].astype(F32)
        ms = jnp.mean(v * v, axis=-1, keepdims=True)
        vn = ((v * lax.rsqrt(ms + EPS)) * vg_ref[...]).astype(BF16)
        for g in range(GM_GROUPS):
            cols = slice(g * GM_GROUP_DIM, (g + 1) * GM_GROUP_DIM)
            s = jnp.dot(wsb_ref[g], vn[:, cols], preferred_element_type=F32) + bt_ref[:, g:g + 1]
            y_ref[rows, cols] = (z_ref[rows, cols].astype(F32) * s).astype(y_ref.dtype)


def _gm_spatial(z, vg3, ws, bt, layer, tm=256):
    return pl.pallas_call(
        _gm_spatial_kernel,
        grid=(N_TOK // tm,),
        in_specs=[pl.BlockSpec((tm, 2 * GM_WIDTH), lambda i: (i, 0)),
                  pl.BlockSpec((None, 1, GM_WIDTH), lambda i: (layer, 0, 0)),
                  pl.BlockSpec((None, GM_GROUPS, GM_CHUNK, GM_CHUNK), lambda i: (layer, 0, 0, 0)),
                  pl.BlockSpec((None, GM_CHUNK, GM_GROUPS), lambda i: (layer, 0, 0))],
        out_specs=pl.BlockSpec((tm, GM_WIDTH), lambda i: (i, 0)),
        out_shape=jax.ShapeDtypeStruct((N_TOK, GM_WIDTH), BF16),
        scratch_shapes=[pltpu.VMEM((GM_GROUPS, GM_CHUNK, GM_CHUNK), BF16)],
        compiler_params=_params(1),
        name="gm_spatial",
    )(z, vg3, ws, bt)


def _router_kernel(h_ref, g_ref, rwt_ref, f_ref, comb_ref):
    x = h_ref[...]
    ms = jnp.mean(x * x, axis=-1, keepdims=True)
    f = (x * lax.rsqrt(ms + EPS)) * g_ref[...]
    f_ref[...] = f.astype(f_ref.dtype)
    logits = lax.dot_general(rwt_ref[...], f, (((1,), (1,)), ((), ())),
                             precision=lax.Precision.HIGHEST, preferred_element_type=F32)
    eid = lax.broadcasted_iota(jnp.int32, logits.shape, 0)
    m1 = jnp.max(logits, axis=0, keepdims=True)
    i1 = jnp.min(jnp.where(logits == m1, eid, N_EXPERTS), axis=0, keepdims=True)
    rest = jnp.where(eid == i1, -jnp.inf, logits)
    m2 = jnp.max(rest, axis=0, keepdims=True)
    i2 = jnp.min(jnp.where(rest == m2, eid, N_EXPERTS), axis=0, keepdims=True)
    e2 = jnp.exp(m2 - m1)
    den = 1.0 + e2
    comb_ref[...] = jnp.where(eid == i1, 1.0 / den, 0.0) + jnp.where(eid == i2, e2 / den, 0.0)


def _router(h, g3, layer, rwt, mlayer, tm=512):
    return pl.pallas_call(
        _router_kernel,
        grid=(N_TOK // tm,),
        in_specs=[pl.BlockSpec((tm, D_MODEL), lambda i: (i, 0)),
                  pl.BlockSpec((None, 1, D_MODEL), lambda i: (layer, 0, 0)),
                  pl.BlockSpec((None, N_EXPERTS, D_MODEL), lambda i: (mlayer, 0, 0))],
        out_specs=[pl.BlockSpec((tm, D_MODEL), lambda i: (i, 0)),
                   pl.BlockSpec((N_EXPERTS, tm), lambda i: (0, i))],
        out_shape=[jax.ShapeDtypeStruct((N_TOK, D_MODEL), BF16),
                   jax.ShapeDtypeStruct((N_EXPERTS, N_TOK), F32)],
        compiler_params=_params(1),
        name="router",
    )(h, g3, rwt)


def _alibi_slopes():
    n = N_DIL * N_HEADS
    s = jnp.exp2(-8.0 * jnp.arange(1, n + 1, dtype=F32) / n)
    return s.reshape(N_HEADS, N_DIL).T


def kernel(x, mix_norm_g, ffn_norm_g, attn_w_in, attn_w_out, gm_w_in, gm_v_norm_g, gm_w_s,
           gm_b_s, gm_w_out, dense_w_gate, dense_w_up, dense_w_down, router_w, moe_w_gate,
           moe_w_up, moe_w_down, final_norm_g):
    h = x.reshape(N_TOK, D_MODEL)
    mix_g = mix_norm_g.reshape(DEPTH, 1, D_MODEL)
    ffn_g = ffn_norm_g.reshape(DEPTH, 1, D_MODEL)
    gm_vg = gm_v_norm_g.reshape(-1, 1, GM_WIDTH)
    gm_bt = jnp.swapaxes(gm_b_s, 1, 2)
    router_wt = jnp.swapaxes(router_w, 1, 2)
    slopes = _alibi_slopes()

    for i in range(DEPTH):
        j = i // 2
        a = _rmsnorm(h, mix_g, i, BF16)
        if i % 2 == 0:
            groups = []
            for g, (win, dil) in enumerate(DIL_CONFIGS):
                off = g * QKV_GROUP_COLS // 1024
                if dil == 1:
                    qkv = _mm_act(a, attn_w_in, (j,), QKV_GROUP_COLS, None, col_off=off)
                    qkv = qkv.reshape(BATCH, SEQ, QKV_GROUP_COLS)
                else:
                    qkv = _mm_perm(a, attn_w_in, (j,), QKV_GROUP_COLS, dil, off)
                groups.append(qkv)
            o = _attention(groups, slopes).reshape(N_TOK, D_MODEL)
            h = _mm_res(o, attn_w_out, (j,), h, tm=512, tn=1024)
            f = _rmsnorm(h, ffn_g, i, BF16)
            act = _mm_swiglu(f, dense_w_gate, dense_w_up, (j,))
            h = _mm_res(act, dense_w_down, (j,), h, tm=256, tn=512)
        else:
            z = _mm_act(a, gm_w_in, (j,), 2 * GM_WIDTH, "gelu")
            y = _gm_spatial(z, gm_vg, gm_w_s, gm_bt, j)
            h = _mm_res(y, gm_w_out, (j,), h, tm=512, tn=1024)
            f, comb_t = _router(h, ffn_g, i, router_wt, j)
            comb = comb_t.T
            for e in range(N_EXPERTS):
                act = _mm_swiglu(f, moe_w_gate, moe_w_up, (j, e))
                h = _mm_res(act, moe_w_down, (j, e), h, tm=256, tn=512, scale=comb, scale_col=e)

    out = pl.pallas_call(
        _rms_kernel,
        grid=(N_TOK // 512,),
        in_specs=[pl.BlockSpec((512, D_MODEL), lambda i: (i, 0)),
                  pl.BlockSpec((1, D_MODEL), lambda i: (0, 0))],
        out_specs=pl.BlockSpec((512, D_MODEL), lambda i: (i, 0)),
        out_shape=jax.ShapeDtypeStruct((N_TOK, D_MODEL), F32),
        compiler_params=_params(1),
        name="final_norm",
    )(h, final_norm_g.reshape(1, D_MODEL))
    return out.reshape(BATCH, SEQ, D_MODEL)
```

```python
import functools

import jax
import jax.numpy as jnp
from jax import lax
from jax.experimental import pallas as pl
from jax.experimental.pallas import tpu as pltpu

D_MODEL = 2048
BATCH = 2
SEQ = 4096
DEPTH = 4
N_TOK = BATCH * SEQ
HEAD_DIM = 128
N_HEADS = D_MODEL // HEAD_DIM
DIL_CONFIGS = ((128, 1), (512, 4), (2048, 16))
N_DIL = len(DIL_CONFIGS)
ATT_BLK = 128
ATT_UNROLL = 8
QKV_GROUP_COLS = 3 * N_HEADS * HEAD_DIM
GM_CHUNK = 128
GM_WIDTH = D_MODEL
GM_GROUP_DIM = 128
GM_GROUPS = GM_WIDTH // GM_GROUP_DIM
D_FF = 7 * D_MODEL // 2
N_EXPERTS = 8
TOP_K = 2
EPS = 1e-6
NEG_INF = -1e30

LANES = 128
VMEM_LIMIT = 56 * 1024 * 1024
CAST_ROWS = 256

FFN_TILE = 256
FFN_CHUNK = 2304
FFN_TILES_PER_CHUNK = FFN_CHUNK // FFN_TILE
DENSE_CHUNK = 2048
FFN_TF = 256
FFN_ZERO_ROWS = 64
MOE_MAX_ITEMS = -(-TOP_K * N_TOK // FFN_CHUNK) + N_EXPERTS
MOE_ROWS = MOE_MAX_ITEMS * FFN_CHUNK

F32 = jnp.float32
BF16 = jnp.bfloat16
I32 = jnp.int32


def _params(n_axes):
    return pltpu.CompilerParams(
        dimension_semantics=("arbitrary",) * n_axes, vmem_limit_bytes=VMEM_LIMIT)


def _cast_weight(w_ref, wb_ref):
    def body(c, carry):
        r = pl.multiple_of(c * CAST_ROWS, CAST_ROWS)
        wb_ref[pl.ds(r, CAST_ROWS), :] = w_ref[pl.ds(r, CAST_ROWS), :].astype(BF16)
        return carry
    lax.fori_loop(0, w_ref.shape[0] // CAST_ROWS, body, 0)


def _gelu_tanh(x):
    return 0.5 * x * (1.0 + jnp.tanh(0.7978845608028654 * (x + 0.044715 * (x * x * x))))


def _rms(x, g):
    ms = jnp.mean(x * x, axis=-1, keepdims=True)
    return (x * lax.rsqrt(ms + EPS)) * g


def _rms_kernel(x_ref, g_ref, o_ref):
    o_ref[...] = _rms(x_ref[...], g_ref[...]).astype(o_ref.dtype)


def _rmsnorm(h, g3, layer, out_dtype, tm=512):
    return pl.pallas_call(
        _rms_kernel,
        grid=(N_TOK // tm,),
        in_specs=[pl.BlockSpec((tm, D_MODEL), lambda i: (i, 0)),
                  pl.BlockSpec((None, 1, D_MODEL), lambda i: (layer, 0, 0))],
        out_specs=pl.BlockSpec((tm, D_MODEL), lambda i: (i, 0)),
        out_shape=jax.ShapeDtypeStruct((N_TOK, D_MODEL), out_dtype),
        compiler_params=_params(1),
        name="rmsnorm",
    )(h, g3)


def _add_rms_kernel(h_ref, y_ref, g_ref, hn_ref, a_ref):
    hn = h_ref[...] + y_ref[...]
    hn_ref[...] = hn
    a_ref[...] = _rms(hn, g_ref[...]).astype(a_ref.dtype)


def _add_rms(h, y, g3, layer, tm=512):
    row = pl.BlockSpec((tm, D_MODEL), lambda i: (i, 0))
    return pl.pallas_call(
        _add_rms_kernel,
        grid=(N_TOK // tm,),
        in_specs=[row, row, pl.BlockSpec((None, 1, D_MODEL), lambda i: (layer, 0, 0))],
        out_specs=[row, row],
        out_shape=[jax.ShapeDtypeStruct((N_TOK, D_MODEL), F32),
                   jax.ShapeDtypeStruct((N_TOK, D_MODEL), BF16)],
        compiler_params=_params(1),
        name="add_rms",
    )(h, y, g3)


def _w_spec(layer, k, tn, col_off):
    return pl.BlockSpec((None, k, tn), lambda j, i: (layer, 0, col_off + j))


def _mm_act_kernel(a_ref, w_ref, o_ref, wb_ref, *, act):
    @pl.when(pl.program_id(1) == 0)
    def _():
        _cast_weight(w_ref, wb_ref)
    acc = jnp.dot(a_ref[...], wb_ref[...], preferred_element_type=F32)
    if act == "gelu":
        acc = _gelu_tanh(acc)
    o_ref[...] = acc.astype(o_ref.dtype)


def _mm_act(a, w, layer, n_cols, act, tm=1024, tn=1024, col_off=0):
    k = a.shape[1]
    return pl.pallas_call(
        functools.partial(_mm_act_kernel, act=act),
        grid=(n_cols // tn, N_TOK // tm),
        in_specs=[pl.BlockSpec((tm, k), lambda j, i: (i, 0)),
                  _w_spec(layer, k, tn, col_off)],
        out_specs=pl.BlockSpec((tm, tn), lambda j, i: (i, j)),
        out_shape=jax.ShapeDtypeStruct((N_TOK, n_cols), BF16),
        scratch_shapes=[pltpu.VMEM((k, tn), BF16)],
        compiler_params=_params(2),
        name="mm_act",
    )(a, w)


def _mm_perm_kernel(a_ref, w_ref, o_ref, wb_ref, acc_ref, *, dil):
    @pl.when(pl.program_id(1) == 0)
    def _():
        _cast_weight(w_ref, wb_ref)
    acc = jnp.dot(a_ref[...], wb_ref[...], preferred_element_type=F32)
    rows = acc_ref.shape[1] // dil
    for cb in range(acc_ref.shape[0]):
        cols = slice(cb * LANES, (cb + 1) * LANES)
        acc_ref[cb] = acc[:, cols]
        for r in range(dil):
            o_ref[r, :, cols] = acc_ref[cb, pl.ds(r, rows, stride=dil), :].astype(o_ref.dtype)


def _mm_perm(a, w, layer, n_cols, dil, col_off, tm=1024, tn=1024):
    k = a.shape[1]
    tiles_per_b = SEQ // tm
    out = pl.pallas_call(
        functools.partial(_mm_perm_kernel, dil=dil),
        grid=(n_cols // tn, N_TOK // tm),
        in_specs=[pl.BlockSpec((tm, k), lambda j, i: (i, 0)),
                  _w_spec(layer, k, tn, col_off)],
        out_specs=pl.BlockSpec((None, dil, tm // dil, tn),
                               lambda j, i: (i // tiles_per_b, 0, i % tiles_per_b, j)),
        out_shape=jax.ShapeDtypeStruct((BATCH, dil, SEQ // dil, n_cols), BF16),
        scratch_shapes=[pltpu.VMEM((k, tn), BF16), pltpu.VMEM((tn // LANES, tm, LANES), F32)],
        compiler_params=_params(2),
        name="mm_perm",
    )(a, w)
    return out.reshape(BATCH, SEQ, n_cols)


def _mm_res_kernel(a_ref, w_ref, res_ref, o_ref, wb_ref):
    @pl.when(pl.program_id(1) == 0)
    def _():
        _cast_weight(w_ref, wb_ref)
    o_ref[...] = res_ref[...] + jnp.dot(a_ref[...], wb_ref[...], preferred_element_type=F32)


def _mm_res(a, w, layer, res, tm=512, tn=1024):
    k = a.shape[1]
    return pl.pallas_call(
        _mm_res_kernel,
        grid=(D_MODEL // tn, N_TOK // tm),
        in_specs=[pl.BlockSpec((tm, k), lambda j, i: (i, 0)),
                  _w_spec(layer, k, tn, 0),
                  pl.BlockSpec((tm, tn), lambda j, i: (i, j))],
        out_specs=pl.BlockSpec((tm, tn), lambda j, i: (i, j)),
        out_shape=jax.ShapeDtypeStruct((N_TOK, D_MODEL), F32),
        scratch_shapes=[pltpu.VMEM((k, tn), BF16)],
        compiler_params=_params(2),
        name="mm_res",
    )(a, w, res)


def _attn_kernel(slopes_ref, *refs):
    qkv_refs = refs[:9]
    o_ref, tabp_ref, tabc_ref, oscr_ref, lscr_ref = refs[9:]
    h = pl.program_id(1)
    ii = lax.broadcasted_iota(I32, (ATT_BLK, ATT_BLK), 0)
    jj = lax.broadcasted_iota(I32, (ATT_BLK, ATT_BLK), 1)
    scale = HEAD_DIM ** -0.5
    nt = (((1,), (1,)), ((), ()))
    for g, (win, dil) in enumerate(DIL_CONFIGS):
        q_ref, k_ref, v_ref = qkv_refs[3 * g:3 * g + 3]
        nb = SEQ // dil // ATT_BLK
        slope = slopes_ref[g, h]
        tabp_ref[...] = jnp.where(
            jj >= ii, -slope * (dil * (ii + ATT_BLK - jj)).astype(F32), NEG_INF)
        tabc_ref[...] = jnp.where(
            jj <= ii, -slope * (dil * (ii - jj)).astype(F32), NEG_INF)

        def body(t, carry, q_ref=q_ref, k_ref=k_ref, v_ref=v_ref, nb=nb, dil=dil, g=g):
            blocks = [t * ATT_UNROLL + u for u in range(ATT_UNROLL)]
            cur = [pl.ds(pl.multiple_of(c * ATT_BLK, ATT_BLK), ATT_BLK) for c in blocks]
            prev = [pl.ds(pl.multiple_of(jnp.maximum(c - 1, 0) * ATT_BLK, ATT_BLK), ATT_BLK)
                    for c in blocks]
            q = [q_ref[r, :] for r in cur]
            sc = [lax.dot_general(q[u], k_ref[cur[u], :], nt, preferred_element_type=F32)
                  * scale + tabc_ref[...] for u in range(ATT_UNROLL)]
            sp = [jnp.where(blocks[u] % nb > 0,
                            lax.dot_general(q[u], k_ref[prev[u], :], nt,
                                            preferred_element_type=F32)
                            * scale + tabp_ref[...], NEG_INF) for u in range(ATT_UNROLL)]
            m = [jnp.max(jnp.maximum(sc[u], sp[u]), axis=-1, keepdims=True)
                 for u in range(ATT_UNROLL)]
            pc = [jnp.exp(sc[u] - m[u]).astype(BF16) for u in range(ATT_UNROLL)]
            pp = [jnp.exp(sp[u] - m[u]).astype(BF16) for u in range(ATT_UNROLL)]
            ones = jnp.ones((ATT_BLK, HEAD_DIM), BF16)
            od = [jnp.dot(pc[u], jnp.concatenate([v_ref[cur[u], :], ones], axis=1),
                          preferred_element_type=F32)
                  + jnp.dot(pp[u], jnp.concatenate([v_ref[prev[u], :], ones], axis=1),
                            preferred_element_type=F32) for u in range(ATT_UNROLL)]
            den = [x[:, HEAD_DIM:] for x in od]
            o = [od[u][:, :HEAD_DIM] / den[u] for u in range(ATT_UNROLL)]
            for u, c in enumerate(blocks):
                start = (c % nb) * (ATT_BLK * dil) + c // nb
                if dil == 1:
                    dst = pl.ds(pl.multiple_of(start, ATT_BLK), ATT_BLK)
                else:
                    dst = pl.ds(start, ATT_BLK, stride=dil)
                oscr_ref[g, dst, :] = o[u]
                lscr_ref[g, dst, :] = jnp.broadcast_to(m[u] + jnp.log(den[u]),
                                                        (ATT_BLK, HEAD_DIM))
            return carry

        lax.fori_loop(0, SEQ // ATT_BLK // ATT_UNROLL, body, 0)

    rows = 256

    def merge(t, carry):
        sl = pl.ds(pl.multiple_of(t * rows, rows), rows)
        l0 = lscr_ref[0, sl, :]
        l1 = lscr_ref[1, sl, :]
        l2 = lscr_ref[2, sl, :]
        m = jnp.maximum(jnp.maximum(l0, l1), l2)
        e0 = jnp.exp(l0 - m)
        e1 = jnp.exp(l1 - m)
        e2 = jnp.exp(l2 - m)
        den = e0 + e1 + e2
        o = (e0 * oscr_ref[0, sl, :] + e1 * oscr_ref[1, sl, :] + e2 * oscr_ref[2, sl, :]) / den
        o_ref[sl, :] = o.astype(o_ref.dtype)
        return carry

    lax.fori_loop(0, SEQ // rows, merge, 0)


def _attention(qkv_groups, slopes):
    in_specs = [pl.BlockSpec(memory_space=pltpu.SMEM)]
    args = [slopes]
    for qkv in qkv_groups:
        for part in range(3):
            in_specs.append(pl.BlockSpec(
                (None, SEQ, HEAD_DIM), lambda b, h, part=part: (b, 0, part * N_HEADS + h)))
            args.append(qkv)
    return pl.pallas_call(
        _attn_kernel,
        grid=(BATCH, N_HEADS),
        in_specs=in_specs,
        out_specs=pl.BlockSpec((None, SEQ, HEAD_DIM), lambda b, h: (b, 0, h)),
        out_shape=jax.ShapeDtypeStruct((BATCH, SEQ, D_MODEL), BF16),
        scratch_shapes=[pltpu.VMEM((ATT_BLK, ATT_BLK), F32),
                        pltpu.VMEM((ATT_BLK, ATT_BLK), F32),
                        pltpu.VMEM((N_DIL, SEQ, HEAD_DIM), F32),
                        pltpu.VMEM((N_DIL, SEQ, HEAD_DIM), F32)],
        compiler_params=_params(2),
        name="dilated_attn",
    )(*args)


def _gm_spatial_kernel(z_ref, vg_ref, ws_ref, bt_ref, y_ref, wsb_ref):
    @pl.when(pl.program_id(0) == 0)
    def _():
        ii = lax.broadcasted_iota(I32, (GM_CHUNK, GM_CHUNK), 0)
        jj = lax.broadcasted_iota(I32, (GM_CHUNK, GM_CHUNK), 1)
        for g in range(GM_GROUPS):
            wsb_ref[g] = jnp.where(ii >= jj, ws_ref[g], 0.0).astype(BF16)

    for c in range(z_ref.shape[0] // GM_CHUNK):
        rows = slice(c * GM_CHUNK, (c + 1) * GM_CHUNK)
        vn = _rms(z_ref[rows, GM_WIDTH:].astype(F32), vg_ref[...]).astype(BF16)
        for g in range(GM_GROUPS):
            cols = slice(g * GM_GROUP_DIM, (g + 1) * GM_GROUP_DIM)
            s = jnp.dot(wsb_ref[g], vn[:, cols], preferred_element_type=F32) + bt_ref[:, g:g + 1]
            y_ref[rows, cols] = (z_ref[rows, cols].astype(F32) * s).astype(y_ref.dtype)


def _gm_spatial(z, vg3, ws, bt, layer, tm=256):
    return pl.pallas_call(
        _gm_spatial_kernel,
        grid=(N_TOK // tm,),
        in_specs=[pl.BlockSpec((tm, 2 * GM_WIDTH), lambda i: (i, 0)),
                  pl.BlockSpec((None, 1, GM_WIDTH), lambda i: (layer, 0, 0)),
                  pl.BlockSpec((None, GM_GROUPS, GM_CHUNK, GM_CHUNK), lambda i: (layer, 0, 0, 0)),
                  pl.BlockSpec((None, GM_CHUNK, GM_GROUPS), lambda i: (layer, 0, 0))],
        out_specs=pl.BlockSpec((tm, GM_WIDTH), lambda i: (i, 0)),
        out_shape=jax.ShapeDtypeStruct((N_TOK, GM_WIDTH), BF16),
        scratch_shapes=[pltpu.VMEM((GM_GROUPS, GM_CHUNK, GM_CHUNK), BF16)],
        compiler_params=_params(1),
        name="gm_spatial",
    )(z, vg3, ws, bt)


def _router_kernel(h_ref, g_ref, rwt_ref, idx_ref, gate_ref, rank_ref, cnt_ref, run_ref):
    @pl.when(pl.program_id(0) == 0)
    def _():
        run_ref[...] = jnp.zeros_like(run_ref)

    f = _rms(h_ref[...], g_ref[...])
    logits = lax.dot_general(rwt_ref[...], f, (((1,), (1,)), ((), ())),
                             precision=lax.Precision.HIGHEST, preferred_element_type=F32)
    tm = logits.shape[1]
    eid = lax.broadcasted_iota(I32, logits.shape, 0)
    m1 = jnp.max(logits, axis=0, keepdims=True)
    i1 = jnp.min(jnp.where(logits == m1, eid, N_EXPERTS), axis=0, keepdims=True)
    rest = jnp.where(eid == i1, -jnp.inf, logits)
    m2 = jnp.max(rest, axis=0, keepdims=True)
    i2 = jnp.min(jnp.where(rest == m2, eid, N_EXPERTS), axis=0, keepdims=True)
    e2 = jnp.exp(m2 - m1)
    den = 1.0 + e2
    idx_ref[0:1, :] = i1
    idx_ref[1:2, :] = i2
    gate_ref[0:1, :] = 1.0 / den
    gate_ref[1:2, :] = e2 / den

    sel1 = eid == i1
    sel2 = eid == i2
    onehot = jnp.where(sel1, 1.0, jnp.where(sel2, 1.0, 0.0))
    earlier = (lax.broadcasted_iota(I32, (tm, tm), 0)
               < lax.broadcasted_iota(I32, (tm, tm), 1))
    before = jnp.dot(onehot.astype(BF16), jnp.where(earlier, 1.0, 0.0).astype(BF16),
                     preferred_element_type=F32) + run_ref[:, 0:1]
    rank_ref[0:1, :] = jnp.sum(jnp.where(sel1, before, 0.0), axis=0, keepdims=True).astype(I32)
    rank_ref[1:2, :] = jnp.sum(jnp.where(sel2, before, 0.0), axis=0, keepdims=True).astype(I32)
    run_ref[...] = run_ref[...] + jnp.sum(onehot, axis=1, keepdims=True)
    cnt_ref[...] = run_ref[...]


def _router(h, g3, layer, rwt, mlayer, tm=512):
    pair = pl.BlockSpec((TOP_K, tm), lambda i: (0, i))
    return pl.pallas_call(
        _router_kernel,
        grid=(N_TOK // tm,),
        in_specs=[pl.BlockSpec((tm, D_MODEL), lambda i: (i, 0)),
                  pl.BlockSpec((None, 1, D_MODEL), lambda i: (layer, 0, 0)),
                  pl.BlockSpec((None, N_EXPERTS, D_MODEL), lambda i: (mlayer, 0, 0))],
        out_specs=[pair, pair, pair, pl.BlockSpec((N_EXPERTS, LANES), lambda i: (0, 0))],
        out_shape=[jax.ShapeDtypeStruct((TOP_K, N_TOK), I32),
                   jax.ShapeDtypeStruct((TOP_K, N_TOK), F32),
                   jax.ShapeDtypeStruct((TOP_K, N_TOK), I32),
                   jax.ShapeDtypeStruct((N_EXPERTS, LANES), F32)],
        scratch_shapes=[pltpu.VMEM((N_EXPERTS, LANES), F32)],
        compiler_params=_params(1),
        name="router",
    )(h, g3, rwt)


def _dispatch_plan(idx, rank, cnt):
    counts = cnt[:, 0].astype(I32)
    tiles_e = (counts + FFN_TILE - 1) // FFN_TILE
    items_e = (tiles_e + FFN_TILES_PER_CHUNK - 1) // FFN_TILES_PER_CHUNK
    items_end = jnp.cumsum(items_e)
    items_start = items_end - items_e
    n_items = items_end[-1]
    pos = (items_start * FFN_CHUNK)[idx] + rank
    it = jnp.arange(MOE_MAX_ITEMS, dtype=I32)
    it_c = jnp.minimum(it, n_items - 1)
    item_expert = jnp.searchsorted(items_end, it_c, side="right").astype(I32)
    local = it_c - items_start[item_expert]
    item_tiles = jnp.clip(tiles_e[item_expert] - local * FFN_TILES_PER_CHUNK,
                          0, FFN_TILES_PER_CHUNK)
    item_tiles = jnp.where(it < n_items, item_tiles, 0).astype(I32)
    tok = jnp.broadcast_to(jnp.arange(N_TOK, dtype=I32), (TOP_K, N_TOK))
    src = jnp.zeros((MOE_ROWS,), I32).at[pos.reshape(-1)].set(tok.reshape(-1))
    tile_id = jnp.arange(MOE_ROWS // FFN_TILE, dtype=I32)
    tile_valid = (tile_id % FFN_TILES_PER_CHUNK
                  < item_tiles[tile_id // FFN_TILES_PER_CHUNK]).astype(I32)
    return pos, src, tile_valid, item_expert, item_tiles, it_c


def _row_copy(src_hbm, row, dst_ref, slot, r, sem_ref):
    return pltpu.make_async_copy(src_hbm.at[pl.ds(row, 1), :],
                                 dst_ref.at[slot, pl.ds(r, 1), :], sem_ref.at[slot])


def _dispatch_kernel(src_ref, valid_ref, h_hbm, g_ref, o_ref, buf_ref, sem_ref):
    p = pl.program_id(0)
    last = pl.num_programs(0) - 1

    def issue(tile, slot):
        def body(r, carry):
            _row_copy(h_hbm, src_ref[tile * FFN_TILE + r], buf_ref, slot, r, sem_ref).start()
            return carry
        lax.fori_loop(0, FFN_TILE, body, 0)

    @pl.when((p == 0) & (valid_ref[0] > 0))
    def _():
        issue(0, 0)

    nxt = jnp.minimum(p + 1, last)

    @pl.when((p < last) & (valid_ref[nxt] > 0))
    def _():
        issue(nxt, nxt % 2)

    slot = p % 2

    @pl.when(valid_ref[p] > 0)
    def _():
        for r in range(FFN_TILE):
            _row_copy(h_hbm, 0, buf_ref, slot, r, sem_ref).wait()
        o_ref[...] = _rms(buf_ref[slot], g_ref[...]).astype(o_ref.dtype)

    @pl.when(valid_ref[p] == 0)
    def _():
        o_ref[...] = jnp.zeros_like(o_ref)


def _dispatch(h, src, tile_valid, g3, layer):
    return pl.pallas_call(
        _dispatch_kernel,
        grid_spec=pltpu.PrefetchScalarGridSpec(
            num_scalar_prefetch=2,
            grid=(MOE_ROWS // FFN_TILE,),
            in_specs=[pl.BlockSpec(memory_space=pl.ANY),
                      pl.BlockSpec((None, 1, D_MODEL), lambda p, s, v: (layer, 0, 0))],
            out_specs=pl.BlockSpec((FFN_TILE, D_MODEL), lambda p, s, v: (p, 0)),
            scratch_shapes=[pltpu.VMEM((2, FFN_TILE, D_MODEL), F32),
                            pltpu.SemaphoreType.DMA((2,))]),
        out_shape=jax.ShapeDtypeStruct((MOE_ROWS, D_MODEL), BF16),
        compiler_params=_params(1),
        name="moe_dispatch",
    )(src, tile_valid, h, g3)


def _ffn_kernel(exp_ref, tiles_ref, blk_ref, x_ref, wg_ref, wu_ref, wd_ref, y_hbm,
                wgu_ref, wdb_ref, acc_ref, zero_ref, sem_ref, zsem_ref):
    it = pl.program_id(0)
    j = pl.program_id(1)
    last_j = pl.num_programs(1) - 1
    n_tiles = tiles_ref[it]
    tf = wg_ref.shape[1]
    chunk = x_ref.shape[0]

    def out_copy(i):
        r = pl.multiple_of(i * FFN_TILE, FFN_TILE)
        row = pl.multiple_of(blk_ref[it] * chunk + r, FFN_TILE)
        return pltpu.make_async_copy(acc_ref.at[pl.ds(r, FFN_TILE), :],
                                     y_hbm.at[pl.ds(row, FFN_TILE), :], sem_ref.at[i])

    def ffn_rows(start, n_rows, first):
        rows = pl.ds(pl.multiple_of(start, n_rows), n_rows)
        gu = jnp.dot(x_ref[rows, :], wgu_ref[...], preferred_element_type=F32)
        gate = gu[:, :tf]
        act = ((gate * jax.nn.sigmoid(gate)) * gu[:, tf:]).astype(BF16)
        d = jnp.dot(act, wdb_ref[...], preferred_element_type=F32)
        if first:
            acc_ref[rows, :] = d
        else:
            acc_ref[rows, :] += d

    def run(first, last):
        def pair(p, carry):
            ffn_rows(p * (2 * FFN_TILE), 2 * FFN_TILE, first)
            if last:
                out_copy(2 * p).start()
                out_copy(2 * p + 1).start()
            return carry
        lax.fori_loop(0, n_tiles // 2, pair, 0)

        @pl.when(n_tiles % 2 == 1)
        def _():
            ffn_rows((n_tiles - 1) * FFN_TILE, FFN_TILE, first)
            if last:
                out_copy(n_tiles - 1).start()

        if last:
            def drain(i, carry):
                out_copy(i).wait()
                return carry
            lax.fori_loop(0, n_tiles, drain, 0)

    zero_rows = zero_ref.shape[0]

    def zero_copy(i, q):
        row = pl.multiple_of(it * chunk + i * FFN_TILE + q * zero_rows, zero_rows)
        return pltpu.make_async_copy(zero_ref, y_hbm.at[pl.ds(row, zero_rows), :], zsem_ref.at[0])

    def zero_fill():
        def start(i, carry):
            for q in range(FFN_TILE // zero_rows):
                zero_copy(i, q).start()
            return carry

        def wait(i, carry):
            for q in range(FFN_TILE // zero_rows):
                zero_copy(i, q).wait()
            return carry
        lax.fori_loop(n_tiles, chunk // FFN_TILE, start, 0)
        lax.fori_loop(n_tiles, chunk // FFN_TILE, wait, 0)

    @pl.when((it == 0) & (j == 0))
    def _():
        zero_ref[...] = jnp.zeros_like(zero_ref)

    @pl.when(j == jnp.where(n_tiles > 0, last_j, 0))
    def _():
        zero_fill()

    @pl.when(n_tiles > 0)
    def _():
        wgu_ref[:, :tf] = wg_ref[...].astype(BF16)
        wgu_ref[:, tf:] = wu_ref[...].astype(BF16)
        wdb_ref[...] = wd_ref[...].astype(BF16)

        @pl.when(j == 0)
        def _():
            run(True, False)

        @pl.when((j > 0) & (j < last_j))
        def _():
            run(False, False)

        @pl.when(j == last_j)
        def _():
            run(False, True)


def _grouped_ffn(xs, wg, wu, wd, layer, item_expert, item_tiles, item_blk, out_rows, chunk):
    n_items = item_expert.shape[0]
    nj = D_FF // FFN_TF

    def col(it, j, e, t, b):
        return jnp.where(t[it] > 0, j, nj - 1)

    return pl.pallas_call(
        _ffn_kernel,
        grid_spec=pltpu.PrefetchScalarGridSpec(
            num_scalar_prefetch=3,
            grid=(n_items, nj),
            in_specs=[
                pl.BlockSpec((chunk, D_MODEL), lambda it, j, e, t, b: (b[it], 0)),
                pl.BlockSpec((None, None, D_MODEL, FFN_TF),
                             lambda it, j, e, t, b: (layer, e[it], 0, col(it, j, e, t, b))),
                pl.BlockSpec((None, None, D_MODEL, FFN_TF),
                             lambda it, j, e, t, b: (layer, e[it], 0, col(it, j, e, t, b))),
                pl.BlockSpec((None, None, FFN_TF, D_MODEL),
                             lambda it, j, e, t, b: (layer, e[it], col(it, j, e, t, b), 0)),
            ],
            out_specs=pl.BlockSpec(memory_space=pl.ANY),
            scratch_shapes=[pltpu.VMEM((D_MODEL, 2 * FFN_TF), BF16),
                            pltpu.VMEM((FFN_TF, D_MODEL), BF16),
                            pltpu.VMEM((chunk, D_MODEL), F32),
                            pltpu.VMEM((FFN_ZERO_ROWS, D_MODEL), F32),
                            pltpu.SemaphoreType.DMA((chunk // FFN_TILE,)),
                            pltpu.SemaphoreType.DMA((1,))]),
        out_shape=jax.ShapeDtypeStruct((out_rows, D_MODEL), F32),
        compiler_params=_params(2),
        name="grouped_ffn",
    )(item_expert, item_tiles, item_blk, xs, wg, wu, wd)


def _dense_ffn(f, wg, wu, wd, layer):
    n_items = N_TOK // DENSE_CHUNK
    return _grouped_ffn(
        f, wg[:, None], wu[:, None], wd[:, None], layer,
        jnp.zeros((n_items,), I32), jnp.full((n_items,), DENSE_CHUNK // FFN_TILE, I32),
        jnp.arange(n_items, dtype=I32), N_TOK, DENSE_CHUNK)


def _combine_kernel(pos_ref, h_ref, gate_ref, g_ref, y_hbm, *rest, final):
    if final:
        o_ref, buf_ref, sem_ref = rest
    else:
        hn_ref, a_ref, buf_ref, sem_ref = rest
    t = pl.program_id(0)
    last = pl.num_programs(0) - 1
    tm = h_ref.shape[0]

    def issue(tile, slot):
        def body(r, carry):
            for k in range(TOP_K):
                _row_copy(y_hbm, pos_ref[k * N_TOK + tile * tm + r],
                          buf_ref, slot, k * tm + r, sem_ref).start()
            return carry
        lax.fori_loop(0, tm, body, 0)

    @pl.when(t == 0)
    def _():
        issue(0, 0)

    @pl.when(t < last)
    def _():
        issue(t + 1, (t + 1) % 2)

    slot = t % 2
    for r in range(TOP_K * tm):
        _row_copy(y_hbm, 0, buf_ref, slot, r, sem_ref).wait()
    hn = (h_ref[...] + gate_ref[:, 0:1] * buf_ref[slot, pl.ds(0, tm), :]
          + gate_ref[:, 1:2] * buf_ref[slot, pl.ds(tm, tm), :])
    if final:
        o_ref[...] = _rms(hn, g_ref[...])
    else:
        hn_ref[...] = hn
        a_ref[...] = _rms(hn, g_ref[...]).astype(a_ref.dtype)


def _combine(h, y, pos, gates_t, g3, layer, final, tm=256):
    row = pl.BlockSpec((tm, D_MODEL), lambda t, p: (t, 0))
    if final:
        out_specs = row
        out_shape = jax.ShapeDtypeStruct((N_TOK, D_MODEL), F32)
    else:
        out_specs = [row, row]
        out_shape = [jax.ShapeDtypeStruct((N_TOK, D_MODEL), F32),
                     jax.ShapeDtypeStruct((N_TOK, D_MODEL), BF16)]
    return pl.pallas_call(
        functools.partial(_combine_kernel, final=final),
        grid_spec=pltpu.PrefetchScalarGridSpec(
            num_scalar_prefetch=1,
            grid=(N_TOK // tm,),
            in_specs=[row,
                      pl.BlockSpec((tm, TOP_K), lambda t, p: (t, 0)),
                      pl.BlockSpec((None, 1, D_MODEL), lambda t, p: (layer, 0, 0)),
                      pl.BlockSpec(memory_space=pl.ANY)],
            out_specs=out_specs,
            scratch_shapes=[pltpu.VMEM((2, TOP_K * tm, D_MODEL), F32),
                            pltpu.SemaphoreType.DMA((2,))]),
        out_shape=out_shape,
        compiler_params=_params(1),
        name="moe_combine",
    )(pos.reshape(-1), h, gates_t, g3, y)


def _alibi_slopes():
    n = N_DIL * N_HEADS
    s = jnp.exp2(-8.0 * jnp.arange(1, n + 1, dtype=F32) / n)
    return s.reshape(N_HEADS, N_DIL).T


def kernel(x, mix_norm_g, ffn_norm_g, attn_w_in, attn_w_out, gm_w_in, gm_v_norm_g, gm_w_s,
           gm_b_s, gm_w_out, dense_w_gate, dense_w_up, dense_w_down, router_w, moe_w_gate,
           moe_w_up, moe_w_down, final_norm_g):
    h = x.reshape(N_TOK, D_MODEL)
    mix_g = mix_norm_g.reshape(DEPTH, 1, D_MODEL)
    ffn_g = ffn_norm_g.reshape(DEPTH, 1, D_MODEL)
    final_g = final_norm_g.reshape(1, 1, D_MODEL)
    gm_vg = gm_v_norm_g.reshape(-1, 1, GM_WIDTH)
    gm_bt = jnp.swapaxes(gm_b_s, 1, 2)
    router_wt = jnp.swapaxes(router_w, 1, 2)
    slopes = _alibi_slopes()

    a = _rmsnorm(h, mix_g, 0, BF16)
    for i in range(DEPTH):
        j = i // 2
        if i % 2 == 0:
            groups = []
            for g, (win, dil) in enumerate(DIL_CONFIGS):
                off = g * QKV_GROUP_COLS // 1024
                if dil == 1:
                    qkv = _mm_act(a, attn_w_in, j, QKV_GROUP_COLS, None, col_off=off)
                    qkv = qkv.reshape(BATCH, SEQ, QKV_GROUP_COLS)
                else:
                    qkv = _mm_perm(a, attn_w_in, j, QKV_GROUP_COLS, dil, off)
                groups.append(qkv)
            o = _attention(groups, slopes).reshape(N_TOK, D_MODEL)
            h = _mm_res(o, attn_w_out, j, h)
            f = _rmsnorm(h, ffn_g, i, BF16)
            y = _dense_ffn(f, dense_w_gate, dense_w_up, dense_w_down, j)
            h, a = _add_rms(h, y, mix_g, i + 1)
        else:
            z = _mm_act(a, gm_w_in, j, 2 * GM_WIDTH, "gelu")
            y = _gm_spatial(z, gm_vg, gm_w_s, gm_bt, j)
            h = _mm_res(y, gm_w_out, j, h)
            idx, gates, rank, cnt = _router(h, ffn_g, i, router_wt, j)
            pos, src, tile_valid, item_expert, item_tiles, item_blk = _dispatch_plan(idx, rank, cnt)
            xs = _dispatch(h, src, tile_valid, ffn_g, i)
            y = _grouped_ffn(xs, moe_w_gate, moe_w_up, moe_w_down, j,
                             item_expert, item_tiles, item_blk, MOE_ROWS, FFN_CHUNK)
            if i == DEPTH - 1:
                return _combine(h, y, pos, gates.T, final_g, 0, True).reshape(BATCH, SEQ, D_MODEL)
            h, a = _combine(h, y, pos, gates.T, mix_g, i + 1, False)
```

```python
import functools

import jax
import jax.numpy as jnp
from jax import lax
from jax.experimental import pallas as pl
from jax.experimental.pallas import tpu as pltpu

D_MODEL = 2048
BATCH = 2
SEQ = 4096
DEPTH = 4
N_TOK = BATCH * SEQ
HEAD_DIM = 128
N_HEADS = D_MODEL // HEAD_DIM
DIL_CONFIGS = ((128, 1), (512, 4), (2048, 16))
N_DIL = len(DIL_CONFIGS)
ATT_BLK = 128
ATT_UNROLL = 8
QKV_GROUP_COLS = 3 * N_HEADS * HEAD_DIM
GM_CHUNK = 128
GM_WIDTH = D_MODEL
GM_GROUP_DIM = 128
GM_GROUPS = GM_WIDTH // GM_GROUP_DIM
D_FF = 7 * D_MODEL // 2
N_EXPERTS = 8
TOP_K = 2
EPS = 1e-6
NEG_INF = -1e30

LANES = 128
VMEM_LIMIT = 56 * 1024 * 1024
CAST_ROWS = 256

FFN_TILE = 256
FFN_CHUNK = 2304
FFN_TILES_PER_CHUNK = FFN_CHUNK // FFN_TILE
DENSE_CHUNK = 2048
FFN_TF = 256
DMA_ISSUE_UNROLL = 8
FFN_ZERO_ROWS = 64
MOE_MAX_ITEMS = -(-TOP_K * N_TOK // FFN_CHUNK) + N_EXPERTS
MOE_ROWS = MOE_MAX_ITEMS * FFN_CHUNK

F32 = jnp.float32
BF16 = jnp.bfloat16
I32 = jnp.int32


def _params(n_axes):
    return pltpu.CompilerParams(
        dimension_semantics=("arbitrary",) * n_axes, vmem_limit_bytes=VMEM_LIMIT)


def _cast_weight(w_ref, wb_ref):
    def body(c, carry):
        r = pl.multiple_of(c * CAST_ROWS, CAST_ROWS)
        wb_ref[pl.ds(r, CAST_ROWS), :] = w_ref[pl.ds(r, CAST_ROWS), :].astype(BF16)
        return carry
    lax.fori_loop(0, w_ref.shape[0] // CAST_ROWS, body, 0)


def _gelu_tanh(x):
    return 0.5 * x * (1.0 + jnp.tanh(0.7978845608028654 * (x + 0.044715 * (x * x * x))))


def _rms(x, g):
    ms = jnp.mean(x * x, axis=-1, keepdims=True)
    return (x * lax.rsqrt(ms + EPS)) * g


def _rms_kernel(x_ref, g_ref, o_ref):
    o_ref[...] = _rms(x_ref[...], g_ref[...]).astype(o_ref.dtype)


def _rmsnorm(h, g3, layer, out_dtype, tm=512):
    return pl.pallas_call(
        _rms_kernel,
        grid=(N_TOK // tm,),
        in_specs=[pl.BlockSpec((tm, D_MODEL), lambda i: (i, 0)),
                  pl.BlockSpec((None, 1, D_MODEL), lambda i: (layer, 0, 0))],
        out_specs=pl.BlockSpec((tm, D_MODEL), lambda i: (i, 0)),
        out_shape=jax.ShapeDtypeStruct((N_TOK, D_MODEL), out_dtype),
        compiler_params=_params(1),
        name="rmsnorm",
    )(h, g3)


def _add_rms_kernel(h_ref, y_ref, g_ref, hn_ref, a_ref):
    hn = h_ref[...] + y_ref[...]
    hn_ref[...] = hn
    a_ref[...] = _rms(hn, g_ref[...]).astype(a_ref.dtype)


def _add_rms(h, y, g3, layer, tm=512):
    row = pl.BlockSpec((tm, D_MODEL), lambda i: (i, 0))
    return pl.pallas_call(
        _add_rms_kernel,
        grid=(N_TOK // tm,),
        in_specs=[row, row, pl.BlockSpec((None, 1, D_MODEL), lambda i: (layer, 0, 0))],
        out_specs=[row, row],
        out_shape=[jax.ShapeDtypeStruct((N_TOK, D_MODEL), F32),
                   jax.ShapeDtypeStruct((N_TOK, D_MODEL), BF16)],
        compiler_params=_params(1),
        name="add_rms",
    )(h, y, g3)


def _w_spec(layer, k, tn, col_off):
    return pl.BlockSpec((None, k, tn), lambda j, i: (layer, 0, col_off + j))


def _mm_act_kernel(a_ref, w_ref, o_ref, wb_ref, *, act):
    @pl.when(pl.program_id(1) == 0)
    def _():
        _cast_weight(w_ref, wb_ref)
    acc = jnp.dot(a_ref[...], wb_ref[...], preferred_element_type=F32)
    if act == "gelu":
        acc = _gelu_tanh(acc)
    o_ref[...] = acc.astype(o_ref.dtype)


def _mm_act(a, w, layer, n_cols, act, tm=1024, tn=1024, col_off=0):
    k = a.shape[1]
    return pl.pallas_call(
        functools.partial(_mm_act_kernel, act=act),
        grid=(n_cols // tn, N_TOK // tm),
        in_specs=[pl.BlockSpec((tm, k), lambda j, i: (i, 0)),
                  _w_spec(layer, k, tn, col_off)],
        out_specs=pl.BlockSpec((tm, tn), lambda j, i: (i, j)),
        out_shape=jax.ShapeDtypeStruct((N_TOK, n_cols), BF16),
        scratch_shapes=[pltpu.VMEM((k, tn), BF16)],
        compiler_params=_params(2),
        name="mm_act",
    )(a, w)


def _mm_qkv_kernel(a_ref, w_ref, o_ref, wb_ref, *rest, dil):
    @pl.when(pl.program_id(1) == 0)
    def _():
        _cast_weight(w_ref, wb_ref)
    acc = jnp.dot(a_ref[...], wb_ref[...], preferred_element_type=F32)
    heads = o_ref.shape[0]
    if dil == 1:
        for hh in range(heads):
            o_ref[hh, 0] = acc[:, hh * HEAD_DIM:(hh + 1) * HEAD_DIM].astype(o_ref.dtype)
    else:
        acc_ref, = rest
        rows = acc_ref.shape[1] // dil
        for hh in range(heads):
            acc_ref[hh] = acc[:, hh * HEAD_DIM:(hh + 1) * HEAD_DIM]

        def regroup(hh, carry):
            for r in range(dil):
                o_ref[hh, r] = acc_ref[hh, pl.ds(r, rows, stride=dil), :].astype(o_ref.dtype)
            return carry
        lax.fori_loop(0, heads, regroup, 0, unroll=dil <= 4)


def _mm_qkv(a, w, layer, dil, col_off, tm=1024, tn=1024):
    k = a.shape[1]
    tiles_per_b = SEQ // tm
    heads = tn // HEAD_DIM
    scratch = [pltpu.VMEM((k, tn), BF16)]
    if dil > 1:
        scratch.append(pltpu.VMEM((heads, tm, HEAD_DIM), F32))
    out = pl.pallas_call(
        functools.partial(_mm_qkv_kernel, dil=dil),
        grid=(QKV_GROUP_COLS // tn, N_TOK // tm),
        in_specs=[pl.BlockSpec((tm, k), lambda j, i: (i, 0)),
                  _w_spec(layer, k, tn, col_off)],
        out_specs=pl.BlockSpec((None, heads, dil, tm // dil, HEAD_DIM),
                               lambda j, i: (i // tiles_per_b, j, 0, i % tiles_per_b, 0)),
        out_shape=jax.ShapeDtypeStruct(
            (BATCH, 3 * N_HEADS, dil, SEQ // dil, HEAD_DIM), BF16),
        scratch_shapes=scratch,
        compiler_params=_params(2),
        name="mm_qkv",
    )(a, w)
    return out.reshape(BATCH, 3 * N_HEADS, SEQ, HEAD_DIM)


def _mm_res_kernel(a_ref, w_ref, res_ref, o_ref, wb_ref):
    @pl.when(pl.program_id(1) == 0)
    def _():
        _cast_weight(w_ref, wb_ref)
    if len(a_ref.shape) == 3:
        a = jnp.concatenate([a_ref[hh] for hh in range(a_ref.shape[0])], axis=1)
    else:
        a = a_ref[...]
    o_ref[...] = res_ref[...] + jnp.dot(a, wb_ref[...], preferred_element_type=F32)


def _mm_res(a, w, layer, res, tm=512, tn=1024):
    if a.ndim == 4:
        tiles_per_b = SEQ // tm
        k = a.shape[1] * a.shape[3]
        a_spec = pl.BlockSpec((None, a.shape[1], tm, a.shape[3]),
                              lambda j, i: (i // tiles_per_b, 0, i % tiles_per_b, 0))
    else:
        k = a.shape[1]
        a_spec = pl.BlockSpec((tm, k), lambda j, i: (i, 0))
    return pl.pallas_call(
        _mm_res_kernel,
        grid=(D_MODEL // tn, N_TOK // tm),
        in_specs=[a_spec,
                  _w_spec(layer, k, tn, 0),
                  pl.BlockSpec((tm, tn), lambda j, i: (i, j))],
        out_specs=pl.BlockSpec((tm, tn), lambda j, i: (i, j)),
        out_shape=jax.ShapeDtypeStruct((N_TOK, D_MODEL), F32),
        scratch_shapes=[pltpu.VMEM((k, tn), BF16)],
        compiler_params=_params(2),
        name="mm_res",
    )(a, w, res)


def _attn_kernel(slopes_ref, *refs):
    qkv_refs = refs[:9]
    o_ref, tabp_ref, tabc_ref, oscr_ref, lscr_ref = refs[9:]
    h = pl.program_id(1)
    ii = lax.broadcasted_iota(I32, (ATT_BLK, ATT_BLK), 0)
    jj = lax.broadcasted_iota(I32, (ATT_BLK, ATT_BLK), 1)
    scale = HEAD_DIM ** -0.5
    nt = (((1,), (1,)), ((), ()))
    for g, (win, dil) in enumerate(DIL_CONFIGS):
        q_ref, k_ref, v_ref = qkv_refs[3 * g:3 * g + 3]
        nb = SEQ // dil // ATT_BLK
        slope = slopes_ref[g, h]
        tabp_ref[...] = jnp.where(
            jj >= ii, -slope * (dil * (ii + ATT_BLK - jj)).astype(F32), NEG_INF)
        tabc_ref[...] = jnp.where(
            jj <= ii, -slope * (dil * (ii - jj)).astype(F32), NEG_INF)

        def body(t, carry, q_ref=q_ref, k_ref=k_ref, v_ref=v_ref, nb=nb, dil=dil, g=g):
            blocks = [t * ATT_UNROLL + u for u in range(ATT_UNROLL)]
            cur = [pl.ds(pl.multiple_of(c * ATT_BLK, ATT_BLK), ATT_BLK) for c in blocks]
            prev = [pl.ds(pl.multiple_of(jnp.maximum(c - 1, 0) * ATT_BLK, ATT_BLK), ATT_BLK)
                    for c in blocks]
            if ATT_UNROLL % nb == 0:
                has_prev = [u % nb > 0 for u in range(ATT_UNROLL)]
            else:
                assert nb % ATT_UNROLL == 0
                has_prev = [None] + [True] * (ATT_UNROLL - 1)
            with_prev = [u for u in range(ATT_UNROLL) if has_prev[u] is not False]
            q = [q_ref[r, :] for r in cur]
            sc = [lax.dot_general(q[u], k_ref[cur[u], :], nt, preferred_element_type=F32)
                  * scale + tabc_ref[...] for u in range(ATT_UNROLL)]
            sp = {u: lax.dot_general(q[u], k_ref[prev[u], :], nt, preferred_element_type=F32)
                  * scale + tabp_ref[...] for u in with_prev}
            for u in with_prev:
                if has_prev[u] is None:
                    sp[u] = jnp.where(blocks[u] % nb > 0, sp[u], NEG_INF)
            m = [jnp.max(jnp.maximum(sc[u], sp[u]) if u in sp else sc[u], axis=-1, keepdims=True)
                 for u in range(ATT_UNROLL)]
            pc = [jnp.exp(sc[u] - m[u]).astype(BF16) for u in range(ATT_UNROLL)]
            pp = {u: jnp.exp(sp[u] - m[u]).astype(BF16) for u in with_prev}
            ones = jnp.ones((ATT_BLK, HEAD_DIM), BF16)
            od = [jnp.dot(pc[u], jnp.concatenate([v_ref[cur[u], :], ones], axis=1),
                          preferred_element_type=F32) for u in range(ATT_UNROLL)]
            for u in with_prev:
                od[u] = od[u] + jnp.dot(pp[u], jnp.concatenate([v_ref[prev[u], :], ones], axis=1),
                                        preferred_element_type=F32)
            den = [x[:, HEAD_DIM:] for x in od]
            o = [od[u][:, :HEAD_DIM] / den[u] for u in range(ATT_UNROLL)]
            for u, c in enumerate(blocks):
                start = (c % nb) * (ATT_BLK * dil) + c // nb
                if dil == 1:
                    dst = pl.ds(pl.multiple_of(start, ATT_BLK), ATT_BLK)
                else:
                    dst = pl.ds(start, ATT_BLK, stride=dil)
                oscr_ref[g, dst, :] = o[u]
                lscr_ref[g, dst, :] = jnp.broadcast_to(m[u] + jnp.log(den[u]),
                                                        (ATT_BLK, HEAD_DIM))
            return carry

        lax.fori_loop(0, SEQ // ATT_BLK // ATT_UNROLL, body, 0)

    rows = 256

    def merge(t, carry):
        sl = pl.ds(pl.multiple_of(t * rows, rows), rows)
        l0 = lscr_ref[0, sl, :]
        l1 = lscr_ref[1, sl, :]
        l2 = lscr_ref[2, sl, :]
        m = jnp.maximum(jnp.maximum(l0, l1), l2)
        e0 = jnp.exp(l0 - m)
        e1 = jnp.exp(l1 - m)
        e2 = jnp.exp(l2 - m)
        den = e0 + e1 + e2
        o = (e0 * oscr_ref[0, sl, :] + e1 * oscr_ref[1, sl, :] + e2 * oscr_ref[2, sl, :]) / den
        o_ref[sl, :] = o.astype(o_ref.dtype)
        return carry

    lax.fori_loop(0, SEQ // rows, merge, 0)


def _attention(qkv_groups, slopes):
    in_specs = [pl.BlockSpec(memory_space=pltpu.SMEM)]
    args = [slopes]
    for qkv in qkv_groups:
        for part in range(3):
            in_specs.append(pl.BlockSpec(
                (None, None, SEQ, HEAD_DIM),
                lambda b, h, part=part: (b, part * N_HEADS + h, 0, 0)))
            args.append(qkv)
    return pl.pallas_call(
        _attn_kernel,
        grid=(BATCH, N_HEADS),
        in_specs=in_specs,
        out_specs=pl.BlockSpec((None, None, SEQ, HEAD_DIM), lambda b, h: (b, h, 0, 0)),
        out_shape=jax.ShapeDtypeStruct((BATCH, N_HEADS, SEQ, HEAD_DIM), BF16),
        scratch_shapes=[pltpu.VMEM((ATT_BLK, ATT_BLK), F32),
                        pltpu.VMEM((ATT_BLK, ATT_BLK), F32),
                        pltpu.VMEM((N_DIL, SEQ, HEAD_DIM), F32),
                        pltpu.VMEM((N_DIL, SEQ, HEAD_DIM), F32)],
        compiler_params=_params(2),
        name="dilated_attn",
    )(*args)


def _gm_spatial_kernel(z_ref, vg_ref, ws_ref, bt_ref, y_ref, wsb_ref):
    @pl.when(pl.program_id(0) == 0)
    def _():
        ii = lax.broadcasted_iota(I32, (GM_CHUNK, GM_CHUNK), 0)
        jj = lax.broadcasted_iota(I32, (GM_CHUNK, GM_CHUNK), 1)
        for g in range(GM_GROUPS):
            wsb_ref[g] = jnp.where(ii >= jj, ws_ref[g], 0.0).astype(BF16)

    for c in range(z_ref.shape[0] // GM_CHUNK):
        rows = slice(c * GM_CHUNK, (c + 1) * GM_CHUNK)
        vn = _rms(z_ref[rows, GM_WIDTH:].astype(F32), vg_ref[...]).astype(BF16)
        for g in range(GM_GROUPS):
            cols = slice(g * GM_GROUP_DIM, (g + 1) * GM_GROUP_DIM)
            s = jnp.dot(wsb_ref[g], vn[:, cols], preferred_element_type=F32) + bt_ref[:, g:g + 1]
            y_ref[rows, cols] = (z_ref[rows, cols].astype(F32) * s).astype(y_ref.dtype)


def _gm_spatial(z, vg3, ws, bt, layer, tm=256):
    return pl.pallas_call(
        _gm_spatial_kernel,
        grid=(N_TOK // tm,),
        in_specs=[pl.BlockSpec((tm, 2 * GM_WIDTH), lambda i: (i, 0)),
                  pl.BlockSpec((None, 1, GM_WIDTH), lambda i: (layer, 0, 0)),
                  pl.BlockSpec((None, GM_GROUPS, GM_CHUNK, GM_CHUNK), lambda i: (layer, 0, 0, 0)),
                  pl.BlockSpec((None, GM_CHUNK, GM_GROUPS), lambda i: (layer, 0, 0))],
        out_specs=pl.BlockSpec((tm, GM_WIDTH), lambda i: (i, 0)),
        out_shape=jax.ShapeDtypeStruct((N_TOK, GM_WIDTH), BF16),
        scratch_shapes=[pltpu.VMEM((GM_GROUPS, GM_CHUNK, GM_CHUNK), BF16)],
        compiler_params=_params(1),
        name="gm_spatial",
    )(z, vg3, ws, bt)


def _router_kernel(h_ref, g_ref, rwt_ref, idx_ref, gate_ref, rank_ref, cnt_ref, run_ref):
    @pl.when(pl.program_id(0) == 0)
    def _():
        run_ref[...] = jnp.zeros_like(run_ref)

    f = _rms(h_ref[...], g_ref[...])
    logits = lax.dot_general(rwt_ref[...], f, (((1,), (1,)), ((), ())),
                             precision=lax.Precision.HIGHEST, preferred_element_type=F32)
    tm = logits.shape[1]
    eid = lax.broadcasted_iota(I32, logits.shape, 0)
    m1 = jnp.max(logits, axis=0, keepdims=True)
    i1 = jnp.min(jnp.where(logits == m1, eid, N_EXPERTS), axis=0, keepdims=True)
    rest = jnp.where(eid == i1, -jnp.inf, logits)
    m2 = jnp.max(rest, axis=0, keepdims=True)
    i2 = jnp.min(jnp.where(rest == m2, eid, N_EXPERTS), axis=0, keepdims=True)
    e2 = jnp.exp(m2 - m1)
    den = 1.0 + e2
    idx_ref[0:1, :] = i1
    idx_ref[1:2, :] = i2
    gate_ref[0:1, :] = 1.0 / den
    gate_ref[1:2, :] = e2 / den

    sel1 = eid == i1
    sel2 = eid == i2
    onehot = jnp.where(sel1, 1.0, jnp.where(sel2, 1.0, 0.0))
    earlier = (lax.broadcasted_iota(I32, (tm, tm), 0)
               < lax.broadcasted_iota(I32, (tm, tm), 1))
    before = jnp.dot(onehot.astype(BF16), jnp.where(earlier, 1.0, 0.0).astype(BF16),
                     preferred_element_type=F32) + run_ref[:, 0:1]
    rank_ref[0:1, :] = jnp.sum(jnp.where(sel1, before, 0.0), axis=0, keepdims=True).astype(I32)
    rank_ref[1:2, :] = jnp.sum(jnp.where(sel2, before, 0.0), axis=0, keepdims=True).astype(I32)
    run_ref[...] = run_ref[...] + jnp.sum(onehot, axis=1, keepdims=True)
    cnt_ref[...] = run_ref[...]


def _router(h, g3, layer, rwt, mlayer, tm=512):
    pair = pl.BlockSpec((TOP_K, tm), lambda i: (0, i))
    return pl.pallas_call(
        _router_kernel,
        grid=(N_TOK // tm,),
        in_specs=[pl.BlockSpec((tm, D_MODEL), lambda i: (i, 0)),
                  pl.BlockSpec((None, 1, D_MODEL), lambda i: (layer, 0, 0)),
                  pl.BlockSpec((None, N_EXPERTS, D_MODEL), lambda i: (mlayer, 0, 0))],
        out_specs=[pair, pair, pair, pl.BlockSpec((N_EXPERTS, LANES), lambda i: (0, 0))],
        out_shape=[jax.ShapeDtypeStruct((TOP_K, N_TOK), I32),
                   jax.ShapeDtypeStruct((TOP_K, N_TOK), F32),
                   jax.ShapeDtypeStruct((TOP_K, N_TOK), I32),
                   jax.ShapeDtypeStruct((N_EXPERTS, LANES), F32)],
        scratch_shapes=[pltpu.VMEM((N_EXPERTS, LANES), F32)],
        compiler_params=_params(1),
        name="router",
    )(h, g3, rwt)


def _dispatch_plan(idx, rank, cnt):
    counts = cnt[:, 0].astype(I32)
    tiles_e = (counts + FFN_TILE - 1) // FFN_TILE
    items_e = (tiles_e + FFN_TILES_PER_CHUNK - 1) // FFN_TILES_PER_CHUNK
    items_end = jnp.cumsum(items_e)
    items_start = items_end - items_e
    n_items = items_end[-1]
    first_row = items_start * FFN_CHUNK
    pos = rank
    for e in range(N_EXPERTS):
        pos = pos + jnp.where(idx == e, first_row[e], 0)
    it = jnp.arange(MOE_MAX_ITEMS, dtype=I32)
    it_c = jnp.clip(it, 0, jnp.maximum(n_items - 1, 0))
    item_expert = jnp.minimum(jnp.searchsorted(items_end, it_c, side="right"),
                              N_EXPERTS - 1).astype(I32)
    local = it_c - items_start[item_expert]
    item_tiles = jnp.clip(tiles_e[item_expert] - local * FFN_TILES_PER_CHUNK,
                          0, FFN_TILES_PER_CHUNK)
    item_tiles = jnp.where(it < n_items, item_tiles, 0).astype(I32)
    tok = jnp.broadcast_to(jnp.arange(N_TOK, dtype=I32), (TOP_K, N_TOK))
    src = jnp.zeros((MOE_ROWS,), I32).at[pos.reshape(-1)].set(tok.reshape(-1))
    tile_id = jnp.arange(MOE_ROWS // FFN_TILE, dtype=I32)
    tile_valid = (tile_id % FFN_TILES_PER_CHUNK
                  < item_tiles[tile_id // FFN_TILES_PER_CHUNK]).astype(I32)
    return pos, src, tile_valid, item_expert, item_tiles, it_c


def _row_copy(src_hbm, row, dst_ref, slot, r, sem_ref):
    return pltpu.make_async_copy(src_hbm.at[pl.ds(row, 1), :],
                                 dst_ref.at[slot, pl.ds(r, 1), :], sem_ref.at[slot])


def _dispatch_kernel(src_ref, valid_ref, h_hbm, g_ref, o_ref, buf_ref, sem_ref):
    p = pl.program_id(0)
    last = pl.num_programs(0) - 1

    def issue(tile, slot):
        def body(r, carry):
            _row_copy(h_hbm, src_ref[tile * FFN_TILE + r], buf_ref, slot, r, sem_ref).start()
            return carry
        lax.fori_loop(0, FFN_TILE, body, 0, unroll=DMA_ISSUE_UNROLL)

    @pl.when((p == 0) & (valid_ref[0] > 0))
    def _():
        issue(0, 0)

    nxt = jnp.minimum(p + 1, last)

    @pl.when((p < last) & (valid_ref[nxt] > 0))
    def _():
        issue(nxt, nxt % 2)

    slot = p % 2

    @pl.when(valid_ref[p] > 0)
    def _():
        for r in range(FFN_TILE):
            _row_copy(h_hbm, 0, buf_ref, slot, r, sem_ref).wait()
        o_ref[...] = _rms(buf_ref[slot], g_ref[...]).astype(o_ref.dtype)

    @pl.when(valid_ref[p] == 0)
    def _():
        o_ref[...] = jnp.zeros_like(o_ref)


def _dispatch(h, src, tile_valid, g3, layer):
    return pl.pallas_call(
        _dispatch_kernel,
        grid_spec=pltpu.PrefetchScalarGridSpec(
            num_scalar_prefetch=2,
            grid=(MOE_ROWS // FFN_TILE,),
            in_specs=[pl.BlockSpec(memory_space=pl.ANY),
                      pl.BlockSpec((None, 1, D_MODEL), lambda p, s, v: (layer, 0, 0))],
            out_specs=pl.BlockSpec((FFN_TILE, D_MODEL), lambda p, s, v: (p, 0)),
            scratch_shapes=[pltpu.VMEM((2, FFN_TILE, D_MODEL), F32),
                            pltpu.SemaphoreType.DMA((2,))]),
        out_shape=jax.ShapeDtypeStruct((MOE_ROWS, D_MODEL), BF16),
        compiler_params=_params(1),
        name="moe_dispatch",
    )(src, tile_valid, h, g3)


def _ffn_kernel(exp_ref, tiles_ref, blk_ref, x_ref, wg_ref, wu_ref, wd_ref, y_hbm,
                wgu_ref, wdb_ref, acc_ref, zero_ref, sem_ref, zsem_ref):
    it = pl.program_id(0)
    j = pl.program_id(1)
    last_j = pl.num_programs(1) - 1
    n_tiles = tiles_ref[it]
    tf = wg_ref.shape[1]
    chunk = x_ref.shape[0]

    def out_copy(i):
        r = pl.multiple_of(i * FFN_TILE, FFN_TILE)
        row = pl.multiple_of(blk_ref[it] * chunk + r, FFN_TILE)
        return pltpu.make_async_copy(acc_ref.at[pl.ds(r, FFN_TILE), :],
                                     y_hbm.at[pl.ds(row, FFN_TILE), :], sem_ref.at[i])

    def ffn_rows(start, n_rows, first):
        rows = pl.ds(pl.multiple_of(start, n_rows), n_rows)
        gu = jnp.dot(x_ref[rows, :], wgu_ref[...], preferred_element_type=F32)
        gate = gu[:, :tf]
        act = ((gate * jax.nn.sigmoid(gate)) * gu[:, tf:]).astype(BF16)
        d = jnp.dot(act, wdb_ref[...], preferred_element_type=F32)
        if first:
            acc_ref[rows, :] = d
        else:
            acc_ref[rows, :] += d

    def run(first, last):
        def pair(p, carry):
            ffn_rows(p * (2 * FFN_TILE), 2 * FFN_TILE, first)
            if last:
                out_copy(2 * p).start()
                out_copy(2 * p + 1).start()
            return carry
        lax.fori_loop(0, n_tiles // 2, pair, 0)

        @pl.when(n_tiles % 2 == 1)
        def _():
            ffn_rows((n_tiles - 1) * FFN_TILE, FFN_TILE, first)
            if last:
                out_copy(n_tiles - 1).start()

        if last:
            def drain(i, carry):
                out_copy(i).wait()
                return carry
            lax.fori_loop(0, n_tiles, drain, 0)

    zero_rows = zero_ref.shape[0]

    def zero_copy(i, q):
        row = pl.multiple_of(it * chunk + i * FFN_TILE + q * zero_rows, zero_rows)
        return pltpu.make_async_copy(zero_ref, y_hbm.at[pl.ds(row, zero_rows), :], zsem_ref.at[0])

    def zero_fill():
        def start(i, carry):
            for q in range(FFN_TILE // zero_rows):
                zero_copy(i, q).start()
            return carry

        def wait(i, carry):
            for q in range(FFN_TILE // zero_rows):
                zero_copy(i, q).wait()
            return carry
        lax.fori_loop(n_tiles, chunk // FFN_TILE, start, 0)
        lax.fori_loop(n_tiles, chunk // FFN_TILE, wait, 0)

    @pl.when((it == 0) & (j == 0))
    def _():
        zero_ref[...] = jnp.zeros_like(zero_ref)

    @pl.when(j == jnp.where(n_tiles > 0, last_j, 0))
    def _():
        zero_fill()

    @pl.when(n_tiles > 0)
    def _():
        wgu_ref[:, :tf] = wg_ref[...].astype(BF16)
        wgu_ref[:, tf:] = wu_ref[...].astype(BF16)
        wdb_ref[...] = wd_ref[...].astype(BF16)

        @pl.when(j == 0)
        def _():
            run(True, False)

        @pl.when((j > 0) & (j < last_j))
        def _():
            run(False, False)

        @pl.when(j == last_j)
        def _():
            run(False, True)


def _grouped_ffn(xs, wg, wu, wd, layer, item_expert, item_tiles, item_blk, out_rows, chunk):
    n_items = item_expert.shape[0]
    nj = D_FF // FFN_TF

    def col(it, j, e, t, b):
        return jnp.where(t[it] > 0, j, nj - 1)

    return pl.pallas_call(
        _ffn_kernel,
        grid_spec=pltpu.PrefetchScalarGridSpec(
            num_scalar_prefetch=3,
            grid=(n_items, nj),
            in_specs=[
                pl.BlockSpec((chunk, D_MODEL), lambda it, j, e, t, b: (b[it], 0)),
                pl.BlockSpec((None, None, D_MODEL, FFN_TF),
                             lambda it, j, e, t, b: (layer, e[it], 0, col(it, j, e, t, b))),
                pl.BlockSpec((None, None, D_MODEL, FFN_TF),
                             lambda it, j, e, t, b: (layer, e[it], 0, col(it, j, e, t, b))),
                pl.BlockSpec((None, None, FFN_TF, D_MODEL),
                             lambda it, j, e, t, b: (layer, e[it], col(it, j, e, t, b), 0)),
            ],
            out_specs=pl.BlockSpec(memory_space=pl.ANY),
            scratch_shapes=[pltpu.VMEM((D_MODEL, 2 * FFN_TF), BF16),
                            pltpu.VMEM((FFN_TF, D_MODEL), BF16),
                            pltpu.VMEM((chunk, D_MODEL), F32),
                            pltpu.VMEM((FFN_ZERO_ROWS, D_MODEL), F32),
                            pltpu.SemaphoreType.DMA((chunk // FFN_TILE,)),
                            pltpu.SemaphoreType.DMA((1,))]),
        out_shape=jax.ShapeDtypeStruct((out_rows, D_MODEL), F32),
        compiler_params=_params(2),
        name="grouped_ffn",
    )(item_expert, item_tiles, item_blk, xs, wg, wu, wd)


def _dense_ffn(f, wg, wu, wd, layer):
    n_items = N_TOK // DENSE_CHUNK
    return _grouped_ffn(
        f, wg[:, None], wu[:, None], wd[:, None], layer,
        jnp.zeros((n_items,), I32), jnp.full((n_items,), DENSE_CHUNK // FFN_TILE, I32),
        jnp.arange(n_items, dtype=I32), N_TOK, DENSE_CHUNK)


def _combine_kernel(pos_ref, h_ref, gate_ref, g_ref, y_hbm, *rest, final):
    if final:
        o_ref, buf_ref, sem_ref = rest
    else:
        hn_ref, a_ref, buf_ref, sem_ref = rest
    t = pl.program_id(0)
    last = pl.num_programs(0) - 1
    tm = h_ref.shape[0]

    def issue(tile, slot):
        def body(r, carry):
            for k in range(TOP_K):
                _row_copy(y_hbm, pos_ref[k * N_TOK + tile * tm + r],
                          buf_ref, slot, k * tm + r, sem_ref).start()
            return carry
        lax.fori_loop(0, tm, body, 0, unroll=DMA_ISSUE_UNROLL)

    @pl.when(t == 0)
    def _():
        issue(0, 0)

    @pl.when(t < last)
    def _():
        issue(t + 1, (t + 1) % 2)

    slot = t % 2
    for r in range(TOP_K * tm):
        _row_copy(y_hbm, 0, buf_ref, slot, r, sem_ref).wait()
    hn = (h_ref[...] + gate_ref[:, 0:1] * buf_ref[slot, pl.ds(0, tm), :]
          + gate_ref[:, 1:2] * buf_ref[slot, pl.ds(tm, tm), :])
    if final:
        o_ref[...] = _rms(hn, g_ref[...])
    else:
        hn_ref[...] = hn
        a_ref[...] = _rms(hn, g_ref[...]).astype(a_ref.dtype)


def _combine(h, y, pos, gates_t, g3, layer, final, tm=256):
    row = pl.BlockSpec((tm, D_MODEL), lambda t, p: (t, 0))
    if final:
        out_specs = row
        out_shape = jax.ShapeDtypeStruct((N_TOK, D_MODEL), F32)
    else:
        out_specs = [row, row]
        out_shape = [jax.ShapeDtypeStruct((N_TOK, D_MODEL), F32),
                     jax.ShapeDtypeStruct((N_TOK, D_MODEL), BF16)]
    return pl.pallas_call(
        functools.partial(_combine_kernel, final=final),
        grid_spec=pltpu.PrefetchScalarGridSpec(
            num_scalar_prefetch=1,
            grid=(N_TOK // tm,),
            in_specs=[row,
                      pl.BlockSpec((tm, TOP_K), lambda t, p: (t, 0)),
                      pl.BlockSpec((None, 1, D_MODEL), lambda t, p: (layer, 0, 0)),
                      pl.BlockSpec(memory_space=pl.ANY)],
            out_specs=out_specs,
            scratch_shapes=[pltpu.VMEM((2, TOP_K * tm, D_MODEL), F32),
                            pltpu.SemaphoreType.DMA((2,))]),
        out_shape=out_shape,
        compiler_params=_params(1),
        name="moe_combine",
    )(pos.reshape(-1), h, gates_t, g3, y)


def _alibi_slopes():
    n = N_DIL * N_HEADS
    s = jnp.exp2(-8.0 * jnp.arange(1, n + 1, dtype=F32) / n)
    return s.reshape(N_HEADS, N_DIL).T


def kernel(x, mix_norm_g, ffn_norm_g, attn_w_in, attn_w_out, gm_w_in, gm_v_norm_g, gm_w_s,
           gm_b_s, gm_w_out, dense_w_gate, dense_w_up, dense_w_down, router_w, moe_w_gate,
           moe_w_up, moe_w_down, final_norm_g):
    h = x.reshape(N_TOK, D_MODEL)
    mix_g = mix_norm_g.reshape(DEPTH, 1, D_MODEL)
    ffn_g = ffn_norm_g.reshape(DEPTH, 1, D_MODEL)
    final_g = final_norm_g.reshape(1, 1, D_MODEL)
    gm_vg = gm_v_norm_g.reshape(-1, 1, GM_WIDTH)
    gm_bt = jnp.swapaxes(gm_b_s, 1, 2)
    router_wt = jnp.swapaxes(router_w, 1, 2)
    slopes = _alibi_slopes()

    a = _rmsnorm(h, mix_g, 0, BF16)
    for i in range(DEPTH):
        j = i // 2
        if i % 2 == 0:
            groups = [_mm_qkv(a, attn_w_in, j, dil, g * (QKV_GROUP_COLS // 1024))
                      for g, (win, dil) in enumerate(DIL_CONFIGS)]
            h = _mm_res(_attention(groups, slopes), attn_w_out, j, h)
            f = _rmsnorm(h, ffn_g, i, BF16)
            y = _dense_ffn(f, dense_w_gate, dense_w_up, dense_w_down, j)
            h, a = _add_rms(h, y, mix_g, i + 1)
        else:
            z = _mm_act(a, gm_w_in, j, 2 * GM_WIDTH, "gelu")
            y = _gm_spatial(z, gm_vg, gm_w_s, gm_bt, j)
            h = _mm_res(y, gm_w_out, j, h)
            idx, gates, rank, cnt = _router(h, ffn_g, i, router_wt, j)
            pos, src, tile_valid, item_expert, item_tiles, item_blk = _dispatch_plan(idx, rank, cnt)
            xs = _dispatch(h, src, tile_valid, ffn_g, i)
            y = _grouped_ffn(xs, moe_w_gate, moe_w_up, moe_w_down, j,
                             item_expert, item_tiles, item_blk, MOE_ROWS, FFN_CHUNK)
            if i == DEPTH - 1:
                return _combine(h, y, pos, gates.T, final_g, 0, True).reshape(BATCH, SEQ, D_MODEL)
            h, a = _combine(h, y, pos, gates.T, mix_g, i + 1, False)
```

```python
import functools

import jax
import jax.numpy as jnp
from jax import lax
from jax.experimental import pallas as pl
from jax.experimental.pallas import tpu as pltpu

D_MODEL = 2048
BATCH = 2
SEQ = 4096
DEPTH = 4
N_TOK = BATCH * SEQ
HEAD_DIM = 128
N_HEADS = D_MODEL // HEAD_DIM
DIL_CONFIGS = ((128, 1), (512, 4), (2048, 16))
N_DIL = len(DIL_CONFIGS)
ATT_BLK = 128
ATT_SKEW = 3
QKV_GROUP_COLS = 3 * N_HEADS * HEAD_DIM
GM_CHUNK = 128
GM_WIDTH = D_MODEL
GM_GROUP_DIM = 128
GM_GROUPS = GM_WIDTH // GM_GROUP_DIM
D_FF = 7 * D_MODEL // 2
N_EXPERTS = 8
TOP_K = 2
EPS = 1e-6
NEG_INF = -1e30

LANES = 128
VMEM_LIMIT = 56 * 1024 * 1024
CAST_ROWS = 256

FFN_TILE = 128
FFN_TRIP_TILES = (4, 2, 1)
DISPATCH_TILE = 256
FFN_CHUNK = 2304
FFN_TILES_PER_CHUNK = FFN_CHUNK // FFN_TILE
DENSE_CHUNK = 2048
FFN_TF = 256
DMA_ISSUE_UNROLL = 8
FFN_ZERO_ROWS = 64
MOE_MAX_ITEMS = -(-TOP_K * N_TOK // FFN_CHUNK) + N_EXPERTS
MOE_ROWS = MOE_MAX_ITEMS * FFN_CHUNK

F32 = jnp.float32
BF16 = jnp.bfloat16
I32 = jnp.int32


def _params(n_axes):
    return pltpu.CompilerParams(
        dimension_semantics=("arbitrary",) * n_axes, vmem_limit_bytes=VMEM_LIMIT)


def _cast_weight(w_ref, wb_ref):
    def body(c, carry):
        r = pl.multiple_of(c * CAST_ROWS, CAST_ROWS)
        wb_ref[pl.ds(r, CAST_ROWS), :] = w_ref[pl.ds(r, CAST_ROWS), :].astype(BF16)
        return carry
    lax.fori_loop(0, w_ref.shape[0] // CAST_ROWS, body, 0)


def _gelu_tanh(x):
    return 0.5 * x * (1.0 + jnp.tanh(0.7978845608028654 * (x + 0.044715 * (x * x * x))))


def _rms(x, g):
    ms = jnp.mean(x * x, axis=-1, keepdims=True)
    return (x * lax.rsqrt(ms + EPS)) * g


def _rms_kernel(x_ref, g_ref, o_ref):
    o_ref[...] = _rms(x_ref[...], g_ref[...]).astype(o_ref.dtype)


def _rmsnorm(h, g3, layer, out_dtype, tm=512):
    return pl.pallas_call(
        _rms_kernel,
        grid=(N_TOK // tm,),
        in_specs=[pl.BlockSpec((tm, D_MODEL), lambda i: (i, 0)),
                  pl.BlockSpec((None, 1, D_MODEL), lambda i: (layer, 0, 0))],
        out_specs=pl.BlockSpec((tm, D_MODEL), lambda i: (i, 0)),
        out_shape=jax.ShapeDtypeStruct((N_TOK, D_MODEL), out_dtype),
        compiler_params=_params(1),
        name="rmsnorm",
    )(h, g3)


def _add_rms_kernel(h_ref, y_ref, g_ref, hn_ref, a_ref):
    hn = h_ref[...] + y_ref[...]
    hn_ref[...] = hn
    a_ref[...] = _rms(hn, g_ref[...]).astype(a_ref.dtype)


def _add_rms(h, y, g3, layer, tm=512):
    row = pl.BlockSpec((tm, D_MODEL), lambda i: (i, 0))
    return pl.pallas_call(
        _add_rms_kernel,
        grid=(N_TOK // tm,),
        in_specs=[row, row, pl.BlockSpec((None, 1, D_MODEL), lambda i: (layer, 0, 0))],
        out_specs=[row, row],
        out_shape=[jax.ShapeDtypeStruct((N_TOK, D_MODEL), F32),
                   jax.ShapeDtypeStruct((N_TOK, D_MODEL), BF16)],
        compiler_params=_params(1),
        name="add_rms",
    )(h, y, g3)


def _w_spec(layer, k, tn, col_off):
    return pl.BlockSpec((None, k, tn), lambda j, i: (layer, 0, col_off + j))


def _mm_act_kernel(a_ref, w_ref, o_ref, wb_ref, *, act):
    @pl.when(pl.program_id(1) == 0)
    def _():
        _cast_weight(w_ref, wb_ref)
    acc = jnp.dot(a_ref[...], wb_ref[...], preferred_element_type=F32)
    if act == "gelu":
        acc = _gelu_tanh(acc)
    o_ref[...] = acc.astype(o_ref.dtype)


def _mm_act(a, w, layer, n_cols, act, tm=1024, tn=1024, col_off=0):
    k = a.shape[1]
    return pl.pallas_call(
        functools.partial(_mm_act_kernel, act=act),
        grid=(n_cols // tn, N_TOK // tm),
        in_specs=[pl.BlockSpec((tm, k), lambda j, i: (i, 0)),
                  _w_spec(layer, k, tn, col_off)],
        out_specs=pl.BlockSpec((tm, tn), lambda j, i: (i, j)),
        out_shape=jax.ShapeDtypeStruct((N_TOK, n_cols), BF16),
        scratch_shapes=[pltpu.VMEM((k, tn), BF16)],
        compiler_params=_params(2),
        name="mm_act",
    )(a, w)


def _mm_qkv_kernel(a_ref, w_ref, o_ref, wb_ref, *rest, dil):
    @pl.when(pl.program_id(1) == 0)
    def _():
        _cast_weight(w_ref, wb_ref)
    acc = jnp.dot(a_ref[...], wb_ref[...], preferred_element_type=F32)
    heads = o_ref.shape[0]
    if dil == 1:
        for hh in range(heads):
            o_ref[hh, 0] = acc[:, hh * HEAD_DIM:(hh + 1) * HEAD_DIM].astype(o_ref.dtype)
    else:
        acc_ref, = rest
        rows = acc_ref.shape[1] // dil
        for hh in range(heads):
            acc_ref[hh] = acc[:, hh * HEAD_DIM:(hh + 1) * HEAD_DIM]

        def regroup(hh, carry):
            for r in range(dil):
                o_ref[hh, r] = acc_ref[hh, pl.ds(r, rows, stride=dil), :].astype(o_ref.dtype)
            return carry
        lax.fori_loop(0, heads, regroup, 0, unroll=dil <= 4)


def _mm_qkv(a, w, layer, dil, col_off, tm=1024, tn=1024):
    k = a.shape[1]
    tiles_per_b = SEQ // tm
    heads = tn // HEAD_DIM
    scratch = [pltpu.VMEM((k, tn), BF16)]
    if dil > 1:
        scratch.append(pltpu.VMEM((heads, tm, HEAD_DIM), F32))
    out = pl.pallas_call(
        functools.partial(_mm_qkv_kernel, dil=dil),
        grid=(QKV_GROUP_COLS // tn, N_TOK // tm),
        in_specs=[pl.BlockSpec((tm, k), lambda j, i: (i, 0)),
                  _w_spec(layer, k, tn, col_off)],
        out_specs=pl.BlockSpec((None, heads, dil, tm // dil, HEAD_DIM),
                               lambda j, i: (i // tiles_per_b, j, 0, i % tiles_per_b, 0)),
        out_shape=jax.ShapeDtypeStruct(
            (BATCH, 3 * N_HEADS, dil, SEQ // dil, HEAD_DIM), BF16),
        scratch_shapes=scratch,
        compiler_params=_params(2),
        name="mm_qkv",
    )(a, w)
    return out.reshape(BATCH, 3 * N_HEADS, SEQ, HEAD_DIM)


def _mm_res_kernel(a_ref, w_ref, res_ref, *rest, with_norm):
    if with_norm:
        g_ref, o_ref, f_ref, wb_ref = rest
    else:
        o_ref, wb_ref = rest

    @pl.when(pl.program_id(1) == 0)
    def _():
        _cast_weight(w_ref, wb_ref)
    if len(a_ref.shape) == 3:
        a = jnp.concatenate([a_ref[hh] for hh in range(a_ref.shape[0])], axis=1)
    else:
        a = a_ref[...]
    out = res_ref[...] + jnp.dot(a, wb_ref[...], preferred_element_type=F32)
    o_ref[...] = out
    if with_norm:
        f_ref[...] = _rms(out, g_ref[...]).astype(f_ref.dtype)


def _mm_res(a, w, layer, res, tm=512, tn=1024, norm=None):
    if norm is not None:
        assert tn == D_MODEL
        return _mm_res_norm(a, w, layer, res, tm, norm)
    if a.ndim == 4:
        tiles_per_b = SEQ // tm
        k = a.shape[1] * a.shape[3]
        a_spec = pl.BlockSpec((None, a.shape[1], tm, a.shape[3]),
                              lambda j, i: (i // tiles_per_b, 0, i % tiles_per_b, 0))
    else:
        k = a.shape[1]
        a_spec = pl.BlockSpec((tm, k), lambda j, i: (i, 0))
    return pl.pallas_call(
        functools.partial(_mm_res_kernel, with_norm=False),
        grid=(D_MODEL // tn, N_TOK // tm),
        in_specs=[a_spec,
                  _w_spec(layer, k, tn, 0),
                  pl.BlockSpec((tm, tn), lambda j, i: (i, j))],
        out_specs=pl.BlockSpec((tm, tn), lambda j, i: (i, j)),
        out_shape=jax.ShapeDtypeStruct((N_TOK, D_MODEL), F32),
        scratch_shapes=[pltpu.VMEM((k, tn), BF16)],
        compiler_params=_params(2),
        name="mm_res",
    )(a, w, res)


def _mm_res_norm(a, w, layer, res, tm, norm):
    g3, g_layer = norm
    tiles_per_b = SEQ // tm
    k = a.shape[1] * a.shape[3]
    row = pl.BlockSpec((tm, D_MODEL), lambda j, i: (i, 0))
    return pl.pallas_call(
        functools.partial(_mm_res_kernel, with_norm=True),
        grid=(1, N_TOK // tm),
        in_specs=[pl.BlockSpec((None, a.shape[1], tm, a.shape[3]),
                               lambda j, i: (i // tiles_per_b, 0, i % tiles_per_b, 0)),
                  _w_spec(layer, k, D_MODEL, 0),
                  row,
                  pl.BlockSpec((None, 1, D_MODEL), lambda j, i: (g_layer, 0, 0))],
        out_specs=[row, row],
        out_shape=[jax.ShapeDtypeStruct((N_TOK, D_MODEL), F32),
                   jax.ShapeDtypeStruct((N_TOK, D_MODEL), BF16)],
        scratch_shapes=[pltpu.VMEM((k, D_MODEL), BF16)],
        compiler_params=_params(2),
        name="mm_res_norm",
    )(a, w, res, g3)


def _attn_kernel(slopes_ref, *refs):
    qkv_refs = refs[:9]
    o_ref, tab_ref, oscr_ref, lscr_ref = refs[9:]
    h = pl.program_id(1)
    ii = lax.broadcasted_iota(I32, (ATT_BLK, 2 * ATT_BLK), 0)
    jj = lax.broadcasted_iota(I32, (ATT_BLK, 2 * ATT_BLK), 1)
    delta = ii + ATT_BLK - jj
    scale = HEAD_DIM ** -0.5
    nt = (((1,), (1,)), ((), ()))
    n_blocks = SEQ // ATT_BLK
    ones = jnp.ones((2 * ATT_BLK, HEAD_DIM), BF16)
    for g, (win, dil) in enumerate(DIL_CONFIGS):
        q_ref, k_ref, v_ref = qkv_refs[3 * g:3 * g + 3]
        nb = n_blocks // dil
        tab_ref[...] = jnp.where((delta >= 0) & (delta <= ATT_BLK),
                                 -slopes_ref[g, h] * (dil * delta).astype(F32), NEG_INF)
        s, m, p, done = {}, {}, {}, []

        def keys(c, nb=nb):
            first = c % nb == 0
            return (slice((c - (not first)) * ATT_BLK, (c + 1) * ATT_BLK),
                    slice(ATT_BLK if first else 0, 2 * ATT_BLK))

        def scores(c):
            rows, cols = keys(c)
            q = q_ref[c * ATT_BLK:(c + 1) * ATT_BLK, :]
            s[c] = lax.dot_general(q, k_ref[rows, :], nt,
                                   preferred_element_type=F32) * scale + tab_ref[:, cols]

        def softmax(c):
            m[c] = jnp.max(s[c], axis=-1, keepdims=True)
            p[c] = jnp.exp(s.pop(c) - m[c]).astype(BF16)

        def values(c):
            rows, _ = keys(c)
            vv = jnp.concatenate([v_ref[rows, :], ones[:rows.stop - rows.start]], axis=1)
            od = jnp.dot(p.pop(c), vv, preferred_element_type=F32)
            den = od[:, HEAD_DIM:]
            done.append((c, od[:, :HEAD_DIM] / den, m.pop(c) + jnp.log(den)))

        for step in range(n_blocks + 2 * ATT_SKEW):
            if step < n_blocks:
                scores(step)
            if 0 <= step - ATT_SKEW < n_blocks:
                softmax(step - ATT_SKEW)
            if 0 <= step - 2 * ATT_SKEW < n_blocks:
                values(step - 2 * ATT_SKEW)
        for c, o, lse in done:
            start = (c % nb) * (ATT_BLK * dil) + c // nb
            dst = pl.ds(start, ATT_BLK) if dil == 1 else pl.ds(start, ATT_BLK, stride=dil)
            oscr_ref[g, dst, :] = o
            lscr_ref[g, dst, :] = jnp.broadcast_to(lse, (ATT_BLK, HEAD_DIM))

    rows = 256

    def merge(t, carry):
        sl = pl.ds(pl.multiple_of(t * rows, rows), rows)
        l0 = lscr_ref[0, sl, :]
        l1 = lscr_ref[1, sl, :]
        l2 = lscr_ref[2, sl, :]
        m = jnp.maximum(jnp.maximum(l0, l1), l2)
        e0 = jnp.exp(l0 - m)
        e1 = jnp.exp(l1 - m)
        e2 = jnp.exp(l2 - m)
        den = e0 + e1 + e2
        o = (e0 * oscr_ref[0, sl, :] + e1 * oscr_ref[1, sl, :] + e2 * oscr_ref[2, sl, :]) / den
        o_ref[sl, :] = o.astype(o_ref.dtype)
        return carry

    lax.fori_loop(0, SEQ // rows, merge, 0)


def _attention(qkv_groups, slopes):
    in_specs = [pl.BlockSpec(memory_space=pltpu.SMEM)]
    args = [slopes]
    for qkv in qkv_groups:
        for part in range(3):
            in_specs.append(pl.BlockSpec(
                (None, None, SEQ, HEAD_DIM),
                lambda b, h, part=part: (b, part * N_HEADS + h, 0, 0)))
            args.append(qkv)
    return pl.pallas_call(
        _attn_kernel,
        grid=(BATCH, N_HEADS),
        in_specs=in_specs,
        out_specs=pl.BlockSpec((None, None, SEQ, HEAD_DIM), lambda b, h: (b, h, 0, 0)),
        out_shape=jax.ShapeDtypeStruct((BATCH, N_HEADS, SEQ, HEAD_DIM), BF16),
        scratch_shapes=[pltpu.VMEM((ATT_BLK, 2 * ATT_BLK), F32),
                        pltpu.VMEM((N_DIL, SEQ, HEAD_DIM), F32),
                        pltpu.VMEM((N_DIL, SEQ, HEAD_DIM), F32)],
        compiler_params=_params(2),
        name="dilated_attn",
    )(*args)


def _gm_spatial_kernel(z_ref, vg_ref, ws_ref, bt_ref, y_ref, wsb_ref):
    @pl.when(pl.program_id(0) == 0)
    def _():
        ii = lax.broadcasted_iota(I32, (GM_CHUNK, GM_CHUNK), 0)
        jj = lax.broadcasted_iota(I32, (GM_CHUNK, GM_CHUNK), 1)
        for g in range(GM_GROUPS):
            wsb_ref[g] = jnp.where(ii >= jj, ws_ref[g], 0.0).astype(BF16)

    for c in range(z_ref.shape[0] // GM_CHUNK):
        rows = slice(c * GM_CHUNK, (c + 1) * GM_CHUNK)
        vn = _rms(z_ref[rows, GM_WIDTH:].astype(F32), vg_ref[...]).astype(BF16)
        for g in range(GM_GROUPS):
            cols = slice(g * GM_GROUP_DIM, (g + 1) * GM_GROUP_DIM)
            s = jnp.dot(wsb_ref[g], vn[:, cols], preferred_element_type=F32) + bt_ref[:, g:g + 1]
            y_ref[rows, cols] = (z_ref[rows, cols].astype(F32) * s).astype(y_ref.dtype)


def _gm_spatial(z, vg3, ws, bt, layer, tm=256):
    return pl.pallas_call(
        _gm_spatial_kernel,
        grid=(N_TOK // tm,),
        in_specs=[pl.BlockSpec((tm, 2 * GM_WIDTH), lambda i: (i, 0)),
                  pl.BlockSpec((None, 1, GM_WIDTH), lambda i: (layer, 0, 0)),
                  pl.BlockSpec((None, GM_GROUPS, GM_CHUNK, GM_CHUNK), lambda i: (layer, 0, 0, 0)),
                  pl.BlockSpec((None, GM_CHUNK, GM_GROUPS), lambda i: (layer, 0, 0))],
        out_specs=pl.BlockSpec((tm, GM_WIDTH), lambda i: (i, 0)),
        out_shape=jax.ShapeDtypeStruct((N_TOK, GM_WIDTH), BF16),
        scratch_shapes=[pltpu.VMEM((GM_GROUPS, GM_CHUNK, GM_CHUNK), BF16)],
        compiler_params=_params(1),
        name="gm_spatial",
    )(z, vg3, ws, bt)


def _router_kernel(h_ref, g_ref, rwt_ref, idx_ref, gate_ref, rank_ref, cnt_ref, run_ref):
    @pl.when(pl.program_id(0) == 0)
    def _():
        run_ref[...] = jnp.zeros_like(run_ref)

    f = _rms(h_ref[...], g_ref[...])
    logits = lax.dot_general(rwt_ref[...], f, (((1,), (1,)), ((), ())),
                             precision=lax.Precision.HIGHEST, preferred_element_type=F32)
    tm = logits.shape[1]
    eid = lax.broadcasted_iota(I32, logits.shape, 0)
    m1 = jnp.max(logits, axis=0, keepdims=True)
    i1 = jnp.min(jnp.where(logits == m1, eid, N_EXPERTS), axis=0, keepdims=True)
    rest = jnp.where(eid == i1, -jnp.inf, logits)
    m2 = jnp.max(rest, axis=0, keepdims=True)
    i2 = jnp.min(jnp.where(rest == m2, eid, N_EXPERTS), axis=0, keepdims=True)
    e2 = jnp.exp(m2 - m1)
    den = 1.0 + e2
    idx_ref[0:1, :] = i1
    idx_ref[1:2, :] = i2
    gate_ref[0:1, :] = 1.0 / den
    gate_ref[1:2, :] = e2 / den

    sel1 = eid == i1
    sel2 = eid == i2
    onehot = jnp.where(sel1, 1.0, jnp.where(sel2, 1.0, 0.0))
    earlier = (lax.broadcasted_iota(I32, (tm, tm), 0)
               < lax.broadcasted_iota(I32, (tm, tm), 1))
    before = jnp.dot(onehot.astype(BF16), jnp.where(earlier, 1.0, 0.0).astype(BF16),
                     preferred_element_type=F32) + run_ref[:, 0:1]
    rank_ref[0:1, :] = jnp.sum(jnp.where(sel1, before, 0.0), axis=0, keepdims=True).astype(I32)
    rank_ref[1:2, :] = jnp.sum(jnp.where(sel2, before, 0.0), axis=0, keepdims=True).astype(I32)
    run_ref[...] = run_ref[...] + jnp.sum(onehot, axis=1, keepdims=True)
    cnt_ref[...] = run_ref[...]


def _router(h, g3, layer, rwt, mlayer, tm=512):
    pair = pl.BlockSpec((TOP_K, tm), lambda i: (0, i))
    return pl.pallas_call(
        _router_kernel,
        grid=(N_TOK // tm,),
        in_specs=[pl.BlockSpec((tm, D_MODEL), lambda i: (i, 0)),
                  pl.BlockSpec((None, 1, D_MODEL), lambda i: (layer, 0, 0)),
                  pl.BlockSpec((None, N_EXPERTS, D_MODEL), lambda i: (mlayer, 0, 0))],
        out_specs=[pair, pair, pair, pl.BlockSpec((N_EXPERTS, LANES), lambda i: (0, 0))],
        out_shape=[jax.ShapeDtypeStruct((TOP_K, N_TOK), I32),
                   jax.ShapeDtypeStruct((TOP_K, N_TOK), F32),
                   jax.ShapeDtypeStruct((TOP_K, N_TOK), I32),
                   jax.ShapeDtypeStruct((N_EXPERTS, LANES), F32)],
        scratch_shapes=[pltpu.VMEM((N_EXPERTS, LANES), F32)],
        compiler_params=_params(1),
        name="router",
    )(h, g3, rwt)


def _dispatch_plan(idx, rank, cnt):
    counts = cnt[:, 0].astype(I32)
    tiles_e = (counts + FFN_TILE - 1) // FFN_TILE
    items_e = (tiles_e + FFN_TILES_PER_CHUNK - 1) // FFN_TILES_PER_CHUNK
    items_end = jnp.cumsum(items_e)
    items_start = items_end - items_e
    n_items = items_end[-1]
    first_row = items_start * FFN_CHUNK
    pos = rank
    for e in range(N_EXPERTS):
        pos = pos + jnp.where(idx == e, first_row[e], 0)
    it = jnp.arange(MOE_MAX_ITEMS, dtype=I32)
    it_c = jnp.clip(it, 0, jnp.maximum(n_items - 1, 0))
    item_expert = jnp.minimum(jnp.searchsorted(items_end, it_c, side="right"),
                              N_EXPERTS - 1).astype(I32)
    local = it_c - items_start[item_expert]
    item_tiles = jnp.clip(tiles_e[item_expert] - local * FFN_TILES_PER_CHUNK,
                          0, FFN_TILES_PER_CHUNK)
    item_tiles = jnp.where(it < n_items, item_tiles, 0).astype(I32)
    tok = jnp.broadcast_to(jnp.arange(N_TOK, dtype=I32), (TOP_K, N_TOK))
    src = jnp.zeros((MOE_ROWS,), I32).at[pos.reshape(-1)].set(tok.reshape(-1))
    per_chunk = FFN_CHUNK // DISPATCH_TILE
    tile_id = jnp.arange(MOE_ROWS // DISPATCH_TILE, dtype=I32)
    tile_valid = ((tile_id % per_chunk) * (DISPATCH_TILE // FFN_TILE)
                  < item_tiles[tile_id // per_chunk]).astype(I32)
    return pos, src, tile_valid, item_expert, item_tiles, it_c


def _row_copy(src_hbm, row, dst_ref, slot, r, sem_ref):
    return pltpu.make_async_copy(src_hbm.at[pl.ds(row, 1), :],
                                 dst_ref.at[slot, pl.ds(r, 1), :], sem_ref.at[slot])


def _dispatch_kernel(src_ref, valid_ref, h_hbm, g_ref, o_ref, buf_ref, sem_ref):
    p = pl.program_id(0)
    last = pl.num_programs(0) - 1

    def issue(tile, slot):
        def body(r, carry):
            _row_copy(h_hbm, src_ref[tile * DISPATCH_TILE + r], buf_ref, slot, r, sem_ref).start()
            return carry
        lax.fori_loop(0, DISPATCH_TILE, body, 0, unroll=DMA_ISSUE_UNROLL)

    @pl.when((p == 0) & (valid_ref[0] > 0))
    def _():
        issue(0, 0)

    nxt = jnp.minimum(p + 1, last)

    @pl.when((p < last) & (valid_ref[nxt] > 0))
    def _():
        issue(nxt, nxt % 2)

    slot = p % 2

    @pl.when(valid_ref[p] > 0)
    def _():
        for r in range(DISPATCH_TILE):
            _row_copy(h_hbm, 0, buf_ref, slot, r, sem_ref).wait()
        o_ref[...] = _rms(buf_ref[slot], g_ref[...]).astype(o_ref.dtype)

    @pl.when(valid_ref[p] == 0)
    def _():
        o_ref[...] = jnp.zeros_like(o_ref)


def _dispatch(h, src, tile_valid, g3, layer):
    return pl.pallas_call(
        _dispatch_kernel,
        grid_spec=pltpu.PrefetchScalarGridSpec(
            num_scalar_prefetch=2,
            grid=(MOE_ROWS // DISPATCH_TILE,),
            in_specs=[pl.BlockSpec(memory_space=pl.ANY),
                      pl.BlockSpec((None, 1, D_MODEL), lambda p, s, v: (layer, 0, 0))],
            out_specs=pl.BlockSpec((DISPATCH_TILE, D_MODEL), lambda p, s, v: (p, 0)),
            scratch_shapes=[pltpu.VMEM((2, DISPATCH_TILE, D_MODEL), F32),
                            pltpu.SemaphoreType.DMA((2,))]),
        out_shape=jax.ShapeDtypeStruct((MOE_ROWS, D_MODEL), BF16),
        compiler_params=_params(1),
        name="moe_dispatch",
    )(src, tile_valid, h, g3)


def _ffn_kernel(exp_ref, tiles_ref, blk_ref, x_ref, wg_ref, wu_ref, wd_ref, y_hbm,
                wgu_ref, wdb_ref, acc_ref, zero_ref, sem_ref, zsem_ref):
    it = pl.program_id(0)
    j = pl.program_id(1)
    last_j = pl.num_programs(1) - 1
    n_tiles = tiles_ref[it]
    tf = wg_ref.shape[1]
    chunk = x_ref.shape[0]

    def out_copy(i):
        r = pl.multiple_of(i * FFN_TILE, FFN_TILE)
        row = pl.multiple_of(blk_ref[it] * chunk + r, FFN_TILE)
        return pltpu.make_async_copy(acc_ref.at[pl.ds(r, FFN_TILE), :],
                                     y_hbm.at[pl.ds(row, FFN_TILE), :], sem_ref.at[i])

    def ffn_rows(start, n_rows, first):
        rows = pl.ds(pl.multiple_of(start, n_rows), n_rows)
        gu = jnp.dot(x_ref[rows, :], wgu_ref[...], preferred_element_type=F32)
        gate = gu[:, :tf]
        act = ((gate * jax.nn.sigmoid(gate)) * gu[:, tf:]).astype(BF16)
        d = jnp.dot(act, wdb_ref[...], preferred_element_type=F32)
        if first:
            acc_ref[rows, :] = d
        else:
            acc_ref[rows, :] += d

    def run(first, last):
        def trip(tile0, tiles):
            ffn_rows(tile0 * FFN_TILE, tiles * FFN_TILE, first)
            if last:
                for i in range(tiles):
                    out_copy(tile0 + i).start()

        big = FFN_TRIP_TILES[0]

        def big_trip(p, carry):
            trip(pl.multiple_of(p * big, big), big)
            return carry
        lax.fori_loop(0, n_tiles // big, big_trip, 0)
        done = n_tiles // big * big
        for tiles in FFN_TRIP_TILES[1:]:
            take = ((n_tiles - done) // tiles) > 0

            @pl.when(take)
            def _(done=done, tiles=tiles):
                trip(pl.multiple_of(done, tiles), tiles)
            done = done + jnp.where(take, tiles, 0)

        if last:
            def drain(i, carry):
                out_copy(i).wait()
                return carry
            lax.fori_loop(0, n_tiles, drain, 0)

    zero_rows = zero_ref.shape[0]

    def zero_copy(i, q):
        row = pl.multiple_of(it * chunk + i * FFN_TILE + q * zero_rows, zero_rows)
        return pltpu.make_async_copy(zero_ref, y_hbm.at[pl.ds(row, zero_rows), :], zsem_ref.at[0])

    def zero_fill():
        def start(i, carry):
            for q in range(FFN_TILE // zero_rows):
                zero_copy(i, q).start()
            return carry

        def wait(i, carry):
            for q in range(FFN_TILE // zero_rows):
                zero_copy(i, q).wait()
            return carry
        lax.fori_loop(n_tiles, chunk // FFN_TILE, start, 0)
        lax.fori_loop(n_tiles, chunk // FFN_TILE, wait, 0)

    @pl.when((it == 0) & (j == 0))
    def _():
        zero_ref[...] = jnp.zeros_like(zero_ref)

    @pl.when(j == jnp.where(n_tiles > 0, last_j, 0))
    def _():
        zero_fill()

    @pl.when(n_tiles > 0)
    def _():
        wgu_ref[:, :tf] = wg_ref[...].astype(BF16)
        wgu_ref[:, tf:] = wu_ref[...].astype(BF16)
        wdb_ref[...] = wd_ref[...].astype(BF16)

        @pl.when(j == 0)
        def _():
            run(True, False)

        @pl.when((j > 0) & (j < last_j))
        def _():
            run(False, False)

        @pl.when(j == last_j)
        def _():
            run(False, True)


def _grouped_ffn(xs, wg, wu, wd, layer, item_expert, item_tiles, item_blk, out_rows, chunk):
    n_items = item_expert.shape[0]
    nj = D_FF // FFN_TF

    def col(it, j, e, t, b):
        return jnp.where(t[it] > 0, j, nj - 1)

    return pl.pallas_call(
        _ffn_kernel,
        grid_spec=pltpu.PrefetchScalarGridSpec(
            num_scalar_prefetch=3,
            grid=(n_items, nj),
            in_specs=[
                pl.BlockSpec((chunk, D_MODEL), lambda it, j, e, t, b: (b[it], 0)),
                pl.BlockSpec((None, None, D_MODEL, FFN_TF),
                             lambda it, j, e, t, b: (layer, e[it], 0, col(it, j, e, t, b))),
                pl.BlockSpec((None, None, D_MODEL, FFN_TF),
                             lambda it, j, e, t, b: (layer, e[it], 0, col(it, j, e, t, b))),
                pl.BlockSpec((None, None, FFN_TF, D_MODEL),
                             lambda it, j, e, t, b: (layer, e[it], col(it, j, e, t, b), 0)),
            ],
            out_specs=pl.BlockSpec(memory_space=pl.ANY),
            scratch_shapes=[pltpu.VMEM((D_MODEL, 2 * FFN_TF), BF16),
                            pltpu.VMEM((FFN_TF, D_MODEL), BF16),
                            pltpu.VMEM((chunk, D_MODEL), F32),
                            pltpu.VMEM((FFN_ZERO_ROWS, D_MODEL), F32),
                            pltpu.SemaphoreType.DMA((chunk // FFN_TILE,)),
                            pltpu.SemaphoreType.DMA((1,))]),
        out_shape=jax.ShapeDtypeStruct((out_rows, D_MODEL), F32),
        compiler_params=_params(2),
        name="grouped_ffn",
    )(item_expert, item_tiles, item_blk, xs, wg, wu, wd)


def _dense_ffn(f, wg, wu, wd, layer):
    n_items = N_TOK // DENSE_CHUNK
    return _grouped_ffn(
        f, wg[:, None], wu[:, None], wd[:, None], layer,
        jnp.zeros((n_items,), I32), jnp.full((n_items,), DENSE_CHUNK // FFN_TILE, I32),
        jnp.arange(n_items, dtype=I32), N_TOK, DENSE_CHUNK)


def _combine_kernel(pos_ref, h_ref, gate_ref, g_ref, y_hbm, *rest, final):
    if final:
        o_ref, buf_ref, sem_ref = rest
    else:
        hn_ref, a_ref, buf_ref, sem_ref = rest
    t = pl.program_id(0)
    last = pl.num_programs(0) - 1
    tm = h_ref.shape[0]

    def issue(tile, slot):
        def body(r, carry):
            for k in range(TOP_K):
                _row_copy(y_hbm, pos_ref[k * N_TOK + tile * tm + r],
                          buf_ref, slot, k * tm + r, sem_ref).start()
            return carry
        lax.fori_loop(0, tm, body, 0, unroll=DMA_ISSUE_UNROLL)

    @pl.when(t == 0)
    def _():
        issue(0, 0)

    @pl.when(t < last)
    def _():
        issue(t + 1, (t + 1) % 2)

    slot = t % 2
    for r in range(TOP_K * tm):
        _row_copy(y_hbm, 0, buf_ref, slot, r, sem_ref).wait()
    hn = (h_ref[...] + gate_ref[:, 0:1] * buf_ref[slot, pl.ds(0, tm), :]
          + gate_ref[:, 1:2] * buf_ref[slot, pl.ds(tm, tm), :])
    if final:
        o_ref[...] = _rms(hn, g_ref[...])
    else:
        hn_ref[...] = hn
        a_ref[...] = _rms(hn, g_ref[...]).astype(a_ref.dtype)


def _combine(h, y, pos, gates_t, g3, layer, final, tm=256):
    row = pl.BlockSpec((tm, D_MODEL), lambda t, p: (t, 0))
    if final:
        out_specs = row
        out_shape = jax.ShapeDtypeStruct((N_TOK, D_MODEL), F32)
    else:
        out_specs = [row, row]
        out_shape = [jax.ShapeDtypeStruct((N_TOK, D_MODEL), F32),
                     jax.ShapeDtypeStruct((N_TOK, D_MODEL), BF16)]
    return pl.pallas_call(
        functools.partial(_combine_kernel, final=final),
        grid_spec=pltpu.PrefetchScalarGridSpec(
            num_scalar_prefetch=1,
            grid=(N_TOK // tm,),
            in_specs=[row,
                      pl.BlockSpec((tm, TOP_K), lambda t, p: (t, 0)),
                      pl.BlockSpec((None, 1, D_MODEL), lambda t, p: (layer, 0, 0)),
                      pl.BlockSpec(memory_space=pl.ANY)],
            out_specs=out_specs,
            scratch_shapes=[pltpu.VMEM((2, TOP_K * tm, D_MODEL), F32),
                            pltpu.SemaphoreType.DMA((2,))]),
        out_shape=out_shape,
        compiler_params=_params(1),
        name="moe_combine",
    )(pos.reshape(-1), h, gates_t, g3, y)


def _alibi_slopes():
    n = N_DIL * N_HEADS
    s = jnp.exp2(-8.0 * jnp.arange(1, n + 1, dtype=F32) / n)
    return s.reshape(N_HEADS, N_DIL).T


def kernel(x, mix_norm_g, ffn_norm_g, attn_w_in, attn_w_out, gm_w_in, gm_v_norm_g, gm_w_s,
           gm_b_s, gm_w_out, dense_w_gate, dense_w_up, dense_w_down, router_w, moe_w_gate,
           moe_w_up, moe_w_down, final_norm_g):
    h = x.reshape(N_TOK, D_MODEL)
    mix_g = mix_norm_g.reshape(DEPTH, 1, D_MODEL)
    ffn_g = ffn_norm_g.reshape(DEPTH, 1, D_MODEL)
    final_g = final_norm_g.reshape(1, 1, D_MODEL)
    gm_vg = gm_v_norm_g.reshape(-1, 1, GM_WIDTH)
    gm_bt = jnp.swapaxes(gm_b_s, 1, 2)
    router_wt = jnp.swapaxes(router_w, 1, 2)
    slopes = _alibi_slopes()

    a = _rmsnorm(h, mix_g, 0, BF16)
    for i in range(DEPTH):
        j = i // 2
        if i % 2 == 0:
            groups = [_mm_qkv(a, attn_w_in, j, dil, g * (QKV_GROUP_COLS // 1024))
                      for g, (win, dil) in enumerate(DIL_CONFIGS)]
            h, f = _mm_res(_attention(groups, slopes), attn_w_out, j, h,
                           tm=256, tn=D_MODEL, norm=(ffn_g, i))
            y = _dense_ffn(f, dense_w_gate, dense_w_up, dense_w_down, j)
            h, a = _add_rms(h, y, mix_g, i + 1)
        else:
            z = _mm_act(a, gm_w_in, j, 2 * GM_WIDTH, "gelu")
            y = _gm_spatial(z, gm_vg, gm_w_s, gm_bt, j)
            h = _mm_res(y, gm_w_out, j, h)
            idx, gates, rank, cnt = _router(h, ffn_g, i, router_wt, j)
            pos, src, tile_valid, item_expert, item_tiles, item_blk = _dispatch_plan(idx, rank, cnt)
            xs = _dispatch(h, src, tile_valid, ffn_g, i)
            y = _grouped_ffn(xs, moe_w_gate, moe_w_up, moe_w_down, j,
                             item_expert, item_tiles, item_blk, MOE_ROWS, FFN_CHUNK)
            if i == DEPTH - 1:
                return _combine(h, y, pos, gates.T, final_g, 0, True).reshape(BATCH, SEQ, D_MODEL)
            h, a = _combine(h, y, pos, gates.T, mix_g, i + 1, False)
```

```python
import functools

import jax
import jax.numpy as jnp
from jax import lax
from jax.experimental import pallas as pl
from jax.experimental.pallas import tpu as pltpu

D_MODEL = 2048
BATCH = 2
SEQ = 4096
DEPTH = 4
N_TOK = BATCH * SEQ
HEAD_DIM = 128
N_HEADS = D_MODEL // HEAD_DIM
DIL_CONFIGS = ((128, 1), (512, 4), (2048, 16))
N_DIL = len(DIL_CONFIGS)
ATT_BLK = 128
ATT_SKEW = 3
QKV_GROUP_COLS = 3 * N_HEADS * HEAD_DIM
GM_CHUNK = 128
GM_WIDTH = D_MODEL
GM_GROUP_DIM = 128
GM_GROUPS = GM_WIDTH // GM_GROUP_DIM
D_FF = 7 * D_MODEL // 2
N_EXPERTS = 8
TOP_K = 2
EPS = 1e-6
NEG_INF = -1e30

LANES = 128
VMEM_LIMIT = 56 * 1024 * 1024
CAST_ROWS = 256

FFN_TILE = 128
FFN_TRIP_TILES = (4, 2, 1)
DISPATCH_TILE = 256
FFN_CHUNK = 2304
FFN_TILES_PER_CHUNK = FFN_CHUNK // FFN_TILE
DENSE_CHUNK = 2048
FFN_TF = 256
DMA_ISSUE_UNROLL = 8
FFN_ZERO_ROWS = 64
MOE_MAX_ITEMS = -(-TOP_K * N_TOK // FFN_CHUNK) + N_EXPERTS
MOE_ROWS = MOE_MAX_ITEMS * FFN_CHUNK

F32 = jnp.float32
BF16 = jnp.bfloat16
I32 = jnp.int32


def _params(n_axes):
    return pltpu.CompilerParams(
        dimension_semantics=("arbitrary",) * n_axes, vmem_limit_bytes=VMEM_LIMIT)


def _cast_weight(w_ref, wb_ref):
    def body(c, carry):
        r = pl.multiple_of(c * CAST_ROWS, CAST_ROWS)
        wb_ref[pl.ds(r, CAST_ROWS), :] = w_ref[pl.ds(r, CAST_ROWS), :].astype(BF16)
        return carry
    lax.fori_loop(0, w_ref.shape[0] // CAST_ROWS, body, 0)


def _gelu_tanh(x):
    return 0.5 * x * (1.0 + jnp.tanh(0.7978845608028654 * (x + 0.044715 * (x * x * x))))


def _rms(x, g):
    ms = jnp.mean(x * x, axis=-1, keepdims=True)
    return (x * lax.rsqrt(ms + EPS)) * g


def _rms_kernel(x_ref, g_ref, o_ref):
    o_ref[...] = _rms(x_ref[...], g_ref[...]).astype(o_ref.dtype)


def _rmsnorm(h, g3, layer, out_dtype, tm=512):
    return pl.pallas_call(
        _rms_kernel,
        grid=(N_TOK // tm,),
        in_specs=[pl.BlockSpec((tm, D_MODEL), lambda i: (i, 0)),
                  pl.BlockSpec((None, 1, D_MODEL), lambda i: (layer, 0, 0))],
        out_specs=pl.BlockSpec((tm, D_MODEL), lambda i: (i, 0)),
        out_shape=jax.ShapeDtypeStruct((N_TOK, D_MODEL), out_dtype),
        compiler_params=_params(1),
        name="rmsnorm",
    )(h, g3)


def _w_spec(layer, k, tn, col_off):
    return pl.BlockSpec((None, k, tn), lambda j, i: (layer, 0, col_off + j))


def _mm_act_kernel(a_ref, w_ref, o_ref, wb_ref, *, act):
    @pl.when(pl.program_id(1) == 0)
    def _():
        _cast_weight(w_ref, wb_ref)
    acc = jnp.dot(a_ref[...], wb_ref[...], preferred_element_type=F32)
    if act == "gelu":
        acc = _gelu_tanh(acc)
    o_ref[...] = acc.astype(o_ref.dtype)


def _mm_act(a, w, layer, n_cols, act, tm=1024, tn=1024, col_off=0):
    k = a.shape[1]
    return pl.pallas_call(
        functools.partial(_mm_act_kernel, act=act),
        grid=(n_cols // tn, N_TOK // tm),
        in_specs=[pl.BlockSpec((tm, k), lambda j, i: (i, 0)),
                  _w_spec(layer, k, tn, col_off)],
        out_specs=pl.BlockSpec((tm, tn), lambda j, i: (i, j)),
        out_shape=jax.ShapeDtypeStruct((N_TOK, n_cols), BF16),
        scratch_shapes=[pltpu.VMEM((k, tn), BF16)],
        compiler_params=_params(2),
        name="mm_act",
    )(a, w)


def _mm_qkv_kernel(a_ref, w_ref, o_ref, wb_ref, *rest, dil):
    @pl.when(pl.program_id(1) == 0)
    def _():
        _cast_weight(w_ref, wb_ref)
    acc = jnp.dot(a_ref[...], wb_ref[...], preferred_element_type=F32)
    heads = o_ref.shape[0]
    if dil == 1:
        for hh in range(heads):
            o_ref[hh, 0] = acc[:, hh * HEAD_DIM:(hh + 1) * HEAD_DIM].astype(o_ref.dtype)
    else:
        acc_ref, = rest
        rows = acc_ref.shape[1] // dil
        for hh in range(heads):
            acc_ref[hh] = acc[:, hh * HEAD_DIM:(hh + 1) * HEAD_DIM]

        def regroup(hh, carry):
            for r in range(dil):
                o_ref[hh, r] = acc_ref[hh, pl.ds(r, rows, stride=dil), :].astype(o_ref.dtype)
            return carry
        lax.fori_loop(0, heads, regroup, 0, unroll=dil <= 4)


def _mm_qkv(a, w, layer, dil, col_off, tm=1024, tn=1024):
    k = a.shape[1]
    tiles_per_b = SEQ // tm
    heads = tn // HEAD_DIM
    scratch = [pltpu.VMEM((k, tn), BF16)]
    if dil > 1:
        scratch.append(pltpu.VMEM((heads, tm, HEAD_DIM), F32))
    out = pl.pallas_call(
        functools.partial(_mm_qkv_kernel, dil=dil),
        grid=(QKV_GROUP_COLS // tn, N_TOK // tm),
        in_specs=[pl.BlockSpec((tm, k), lambda j, i: (i, 0)),
                  _w_spec(layer, k, tn, col_off)],
        out_specs=pl.BlockSpec((None, heads, dil, tm // dil, HEAD_DIM),
                               lambda j, i: (i // tiles_per_b, j, 0, i % tiles_per_b, 0)),
        out_shape=jax.ShapeDtypeStruct(
            (BATCH, 3 * N_HEADS, dil, SEQ // dil, HEAD_DIM), BF16),
        scratch_shapes=scratch,
        compiler_params=_params(2),
        name="mm_qkv",
    )(a, w)
    return out.reshape(BATCH, 3 * N_HEADS, SEQ, HEAD_DIM)


def _proj_norm_kernel(a_ref, w_ref, res_ref, g_ref, o_ref, f_ref, wb_ref):
    @pl.when(pl.program_id(1) == 0)
    def _():
        _cast_weight(w_ref, wb_ref)
    a = jnp.concatenate([a_ref[hh] for hh in range(a_ref.shape[0])], axis=1)
    out = res_ref[...] + jnp.dot(a, wb_ref[...], preferred_element_type=F32)
    o_ref[...] = out
    f_ref[...] = _rms(out, g_ref[...]).astype(f_ref.dtype)


def _proj_norm(a, w, layer, res, g3, g_layer, tm=256):
    tiles_per_b = SEQ // tm
    k = a.shape[1] * a.shape[3]
    row = pl.BlockSpec((tm, D_MODEL), lambda j, i: (i, 0))
    return pl.pallas_call(
        _proj_norm_kernel,
        grid=(1, N_TOK // tm),
        in_specs=[pl.BlockSpec((None, a.shape[1], tm, a.shape[3]),
                               lambda j, i: (i // tiles_per_b, 0, i % tiles_per_b, 0)),
                  _w_spec(layer, k, D_MODEL, 0),
                  row,
                  pl.BlockSpec((None, 1, D_MODEL), lambda j, i: (g_layer, 0, 0))],
        out_specs=[row, row],
        out_shape=[jax.ShapeDtypeStruct((N_TOK, D_MODEL), F32),
                   jax.ShapeDtypeStruct((N_TOK, D_MODEL), BF16)],
        scratch_shapes=[pltpu.VMEM((k, D_MODEL), BF16)],
        compiler_params=_params(2),
        name="proj_norm",
    )(a, w, res, g3)


def _attn_kernel(slopes_ref, *refs):
    qkv_refs = refs[:9]
    o_ref, tab_ref, oscr_ref, lscr_ref = refs[9:]
    h = pl.program_id(1)
    ii = lax.broadcasted_iota(I32, (ATT_BLK, 2 * ATT_BLK), 0)
    jj = lax.broadcasted_iota(I32, (ATT_BLK, 2 * ATT_BLK), 1)
    delta = ii + ATT_BLK - jj
    scale = HEAD_DIM ** -0.5
    nt = (((1,), (1,)), ((), ()))
    n_blocks = SEQ // ATT_BLK
    ones = jnp.ones((2 * ATT_BLK, HEAD_DIM), BF16)
    for g, (win, dil) in enumerate(DIL_CONFIGS):
        q_ref, k_ref, v_ref = qkv_refs[3 * g:3 * g + 3]
        nb = n_blocks // dil
        tab_ref[...] = jnp.where((delta >= 0) & (delta <= ATT_BLK),
                                 -slopes_ref[g, h] * (dil * delta).astype(F32), NEG_INF)
        s, m, p, done = {}, {}, {}, []

        def keys(c, nb=nb):
            first = c % nb == 0
            return (slice((c - (not first)) * ATT_BLK, (c + 1) * ATT_BLK),
                    slice(ATT_BLK if first else 0, 2 * ATT_BLK))

        def scores(c):
            rows, cols = keys(c)
            q = q_ref[c * ATT_BLK:(c + 1) * ATT_BLK, :]
            s[c] = lax.dot_general(q, k_ref[rows, :], nt,
                                   preferred_element_type=F32) * scale + tab_ref[:, cols]

        def softmax(c):
            m[c] = jnp.max(s[c], axis=-1, keepdims=True)
            p[c] = jnp.exp(s.pop(c) - m[c]).astype(BF16)

        def values(c):
            rows, _ = keys(c)
            vv = jnp.concatenate([v_ref[rows, :], ones[:rows.stop - rows.start]], axis=1)
            od = jnp.dot(p.pop(c), vv, preferred_element_type=F32)
            den = od[:, HEAD_DIM:]
            done.append((c, od[:, :HEAD_DIM] / den, m.pop(c) + jnp.log(den)))

        for step in range(n_blocks + 2 * ATT_SKEW):
            if step < n_blocks:
                scores(step)
            if 0 <= step - ATT_SKEW < n_blocks:
                softmax(step - ATT_SKEW)
            if 0 <= step - 2 * ATT_SKEW < n_blocks:
                values(step - 2 * ATT_SKEW)
        for c, o, lse in done:
            start = (c % nb) * (ATT_BLK * dil) + c // nb
            dst = pl.ds(start, ATT_BLK) if dil == 1 else pl.ds(start, ATT_BLK, stride=dil)
            oscr_ref[g, dst, :] = o
            lscr_ref[g, dst, :] = jnp.broadcast_to(lse, (ATT_BLK, HEAD_DIM))

    rows = 256

    def merge(t, carry):
        sl = pl.ds(pl.multiple_of(t * rows, rows), rows)
        l0 = lscr_ref[0, sl, :]
        l1 = lscr_ref[1, sl, :]
        l2 = lscr_ref[2, sl, :]
        m = jnp.maximum(jnp.maximum(l0, l1), l2)
        e0 = jnp.exp(l0 - m)
        e1 = jnp.exp(l1 - m)
        e2 = jnp.exp(l2 - m)
        den = e0 + e1 + e2
        o = (e0 * oscr_ref[0, sl, :] + e1 * oscr_ref[1, sl, :] + e2 * oscr_ref[2, sl, :]) / den
        o_ref[sl, :] = o.astype(o_ref.dtype)
        return carry

    lax.fori_loop(0, SEQ // rows, merge, 0)


def _attention(qkv_groups, slopes):
    in_specs = [pl.BlockSpec(memory_space=pltpu.SMEM)]
    args = [slopes]
    for qkv in qkv_groups:
        for part in range(3):
            in_specs.append(pl.BlockSpec(
                (None, None, SEQ, HEAD_DIM),
                lambda b, h, part=part: (b, part * N_HEADS + h, 0, 0)))
            args.append(qkv)
    return pl.pallas_call(
        _attn_kernel,
        grid=(BATCH, N_HEADS),
        in_specs=in_specs,
        out_specs=pl.BlockSpec((None, None, SEQ, HEAD_DIM), lambda b, h: (b, h, 0, 0)),
        out_shape=jax.ShapeDtypeStruct((BATCH, N_HEADS, SEQ, HEAD_DIM), BF16),
        scratch_shapes=[pltpu.VMEM((ATT_BLK, 2 * ATT_BLK), F32),
                        pltpu.VMEM((N_DIL, SEQ, HEAD_DIM), F32),
                        pltpu.VMEM((N_DIL, SEQ, HEAD_DIM), F32)],
        compiler_params=_params(2),
        name="dilated_attn",
    )(*args)


def _gm_spatial_kernel(z_ref, vg_ref, ws_ref, bt_ref, y_ref, wsb_ref):
    @pl.when(pl.program_id(0) == 0)
    def _():
        ii = lax.broadcasted_iota(I32, (GM_CHUNK, GM_CHUNK), 0)
        jj = lax.broadcasted_iota(I32, (GM_CHUNK, GM_CHUNK), 1)
        for g in range(GM_GROUPS):
            wsb_ref[g] = jnp.where(ii >= jj, ws_ref[g], 0.0).astype(BF16)

    for c in range(z_ref.shape[0] // GM_CHUNK):
        rows = slice(c * GM_CHUNK, (c + 1) * GM_CHUNK)
        vn = _rms(z_ref[rows, GM_WIDTH:].astype(F32), vg_ref[...]).astype(BF16)
        for g in range(GM_GROUPS):
            cols = slice(g * GM_GROUP_DIM, (g + 1) * GM_GROUP_DIM)
            s = jnp.dot(wsb_ref[g], vn[:, cols], preferred_element_type=F32) + bt_ref[:, g:g + 1]
            y_ref[rows, cols] = (z_ref[rows, cols].astype(F32) * s).astype(y_ref.dtype)


def _gm_spatial(z, vg3, ws, bt, layer, tm=256):
    return pl.pallas_call(
        _gm_spatial_kernel,
        grid=(N_TOK // tm,),
        in_specs=[pl.BlockSpec((tm, 2 * GM_WIDTH), lambda i: (i, 0)),
                  pl.BlockSpec((None, 1, GM_WIDTH), lambda i: (layer, 0, 0)),
                  pl.BlockSpec((None, GM_GROUPS, GM_CHUNK, GM_CHUNK), lambda i: (layer, 0, 0, 0)),
                  pl.BlockSpec((None, GM_CHUNK, GM_GROUPS), lambda i: (layer, 0, 0))],
        out_specs=pl.BlockSpec((tm, GM_WIDTH), lambda i: (i, 0)),
        out_shape=jax.ShapeDtypeStruct((N_TOK, GM_WIDTH), BF16),
        scratch_shapes=[pltpu.VMEM((GM_GROUPS, GM_CHUNK, GM_CHUNK), BF16)],
        compiler_params=_params(1),
        name="gm_spatial",
    )(z, vg3, ws, bt)


def _proj_router_kernel(a_ref, w_ref, res_ref, g_ref, rwt_ref, o_ref, idx_ref, gate_ref,
                        rank_ref, cnt_ref, wb_ref, run_ref):
    @pl.when(pl.program_id(0) == 0)
    def _():
        _cast_weight(w_ref, wb_ref)
        run_ref[...] = jnp.zeros_like(run_ref)

    h = res_ref[...] + jnp.dot(a_ref[...], wb_ref[...], preferred_element_type=F32)
    o_ref[...] = h
    f = _rms(h, g_ref[...])
    logits = lax.dot_general(rwt_ref[...], f, (((1,), (1,)), ((), ())),
                             precision=lax.Precision.HIGHEST, preferred_element_type=F32)
    tm = logits.shape[1]
    eid = lax.broadcasted_iota(I32, logits.shape, 0)
    m1 = jnp.max(logits, axis=0, keepdims=True)
    i1 = jnp.min(jnp.where(logits == m1, eid, N_EXPERTS), axis=0, keepdims=True)
    rest = jnp.where(eid == i1, -jnp.inf, logits)
    m2 = jnp.max(rest, axis=0, keepdims=True)
    i2 = jnp.min(jnp.where(rest == m2, eid, N_EXPERTS), axis=0, keepdims=True)
    e2 = jnp.exp(m2 - m1)
    den = 1.0 + e2
    idx_ref[0:1, :] = i1
    idx_ref[1:2, :] = i2
    gate_ref[0:1, :] = 1.0 / den
    gate_ref[1:2, :] = e2 / den

    sel1 = eid == i1
    sel2 = eid == i2
    onehot = jnp.where(sel1, 1.0, jnp.where(sel2, 1.0, 0.0))
    earlier = (lax.broadcasted_iota(I32, (tm, tm), 0)
               < lax.broadcasted_iota(I32, (tm, tm), 1))
    before = jnp.dot(onehot.astype(BF16), jnp.where(earlier, 1.0, 0.0).astype(BF16),
                     preferred_element_type=F32) + run_ref[:, 0:1]
    rank_ref[0:1, :] = jnp.sum(jnp.where(sel1, before, 0.0), axis=0, keepdims=True).astype(I32)
    rank_ref[1:2, :] = jnp.sum(jnp.where(sel2, before, 0.0), axis=0, keepdims=True).astype(I32)
    run_ref[...] = run_ref[...] + jnp.sum(onehot, axis=1, keepdims=True)
    cnt_ref[...] = run_ref[...]


def _proj_router(a, w, wlayer, res, g3, glayer, rwt, mlayer, tm=256):
    k = a.shape[1]
    row = pl.BlockSpec((tm, D_MODEL), lambda i: (i, 0))
    pair = pl.BlockSpec((TOP_K, tm), lambda i: (0, i))
    return pl.pallas_call(
        _proj_router_kernel,
        grid=(N_TOK // tm,),
        in_specs=[pl.BlockSpec((tm, k), lambda i: (i, 0)),
                  pl.BlockSpec((None, k, D_MODEL), lambda i: (wlayer, 0, 0)),
                  row,
                  pl.BlockSpec((None, 1, D_MODEL), lambda i: (glayer, 0, 0)),
                  pl.BlockSpec((None, N_EXPERTS, D_MODEL), lambda i: (mlayer, 0, 0))],
        out_specs=[row, pair, pair, pair, pl.BlockSpec((N_EXPERTS, LANES), lambda i: (0, 0))],
        out_shape=[jax.ShapeDtypeStruct((N_TOK, D_MODEL), F32),
                   jax.ShapeDtypeStruct((TOP_K, N_TOK), I32),
                   jax.ShapeDtypeStruct((TOP_K, N_TOK), F32),
                   jax.ShapeDtypeStruct((TOP_K, N_TOK), I32),
                   jax.ShapeDtypeStruct((N_EXPERTS, LANES), F32)],
        scratch_shapes=[pltpu.VMEM((k, D_MODEL), BF16), pltpu.VMEM((N_EXPERTS, LANES), F32)],
        compiler_params=_params(1),
        name="proj_router",
    )(a, w, res, g3, rwt)


def _dispatch_plan(idx, rank, cnt):
    counts = cnt[:, 0].astype(I32)
    tiles_e = (counts + FFN_TILE - 1) // FFN_TILE
    items_e = (tiles_e + FFN_TILES_PER_CHUNK - 1) // FFN_TILES_PER_CHUNK
    items_end = jnp.cumsum(items_e)
    items_start = items_end - items_e
    n_items = items_end[-1]
    first_row = items_start * FFN_CHUNK
    pos = rank
    for e in range(N_EXPERTS):
        pos = pos + jnp.where(idx == e, first_row[e], 0)
    it = jnp.arange(MOE_MAX_ITEMS, dtype=I32)
    it_c = jnp.clip(it, 0, jnp.maximum(n_items - 1, 0))
    item_expert = jnp.minimum(jnp.searchsorted(items_end, it_c, side="right"),
                              N_EXPERTS - 1).astype(I32)
    local = it_c - items_start[item_expert]
    item_tiles = jnp.clip(tiles_e[item_expert] - local * FFN_TILES_PER_CHUNK,
                          0, FFN_TILES_PER_CHUNK)
    item_tiles = jnp.where(it < n_items, item_tiles, 0).astype(I32)
    tok = jnp.broadcast_to(jnp.arange(N_TOK, dtype=I32), (TOP_K, N_TOK))
    src = jnp.zeros((MOE_ROWS,), I32).at[pos.reshape(-1)].set(tok.reshape(-1))
    per_chunk = FFN_CHUNK // DISPATCH_TILE
    tile_id = jnp.arange(MOE_ROWS // DISPATCH_TILE, dtype=I32)
    tile_valid = ((tile_id % per_chunk) * (DISPATCH_TILE // FFN_TILE)
                  < item_tiles[tile_id // per_chunk]).astype(I32)
    return pos, src, tile_valid, item_expert, item_tiles, it_c


def _row_copy(src_hbm, row, dst_ref, slot, r, sem_ref):
    return pltpu.make_async_copy(src_hbm.at[pl.ds(row, 1), :],
                                 dst_ref.at[slot, pl.ds(r, 1), :], sem_ref.at[slot])


def _dispatch_kernel(src_ref, valid_ref, h_hbm, g_ref, o_ref, buf_ref, sem_ref):
    p = pl.program_id(0)
    last = pl.num_programs(0) - 1

    def issue(tile, slot):
        def body(r, carry):
            _row_copy(h_hbm, src_ref[tile * DISPATCH_TILE + r], buf_ref, slot, r, sem_ref).start()
            return carry
        lax.fori_loop(0, DISPATCH_TILE, body, 0, unroll=DMA_ISSUE_UNROLL)

    @pl.when((p == 0) & (valid_ref[0] > 0))
    def _():
        issue(0, 0)

    nxt = jnp.minimum(p + 1, last)

    @pl.when((p < last) & (valid_ref[nxt] > 0))
    def _():
        issue(nxt, nxt % 2)

    slot = p % 2

    @pl.when(valid_ref[p] > 0)
    def _():
        for r in range(DISPATCH_TILE):
            _row_copy(h_hbm, 0, buf_ref, slot, r, sem_ref).wait()
        o_ref[...] = _rms(buf_ref[slot], g_ref[...]).astype(o_ref.dtype)

    @pl.when(valid_ref[p] == 0)
    def _():
        o_ref[...] = jnp.zeros_like(o_ref)


def _dispatch(h, src, tile_valid, g3, layer):
    return pl.pallas_call(
        _dispatch_kernel,
        grid_spec=pltpu.PrefetchScalarGridSpec(
            num_scalar_prefetch=2,
            grid=(MOE_ROWS // DISPATCH_TILE,),
            in_specs=[pl.BlockSpec(memory_space=pl.ANY),
                      pl.BlockSpec((None, 1, D_MODEL), lambda p, s, v: (layer, 0, 0))],
            out_specs=pl.BlockSpec((DISPATCH_TILE, D_MODEL), lambda p, s, v: (p, 0)),
            scratch_shapes=[pltpu.VMEM((2, DISPATCH_TILE, D_MODEL), F32),
                            pltpu.SemaphoreType.DMA((2,))]),
        out_shape=jax.ShapeDtypeStruct((MOE_ROWS, D_MODEL), BF16),
        compiler_params=_params(1),
        name="moe_dispatch",
    )(src, tile_valid, h, g3)


def _ffn_kernel(exp_ref, tiles_ref, blk_ref, x_ref, wg_ref, wu_ref, wd_ref, *rest, fused):
    if fused:
        (res_hbm, g_ref, y_hbm, a_hbm, wgu_ref, wdb_ref, acc_ref, zero_ref, sem_ref, zsem_ref,
         astage_ref, asem_ref, rsem_ref) = rest
    else:
        y_hbm, wgu_ref, wdb_ref, acc_ref, zero_ref, sem_ref, zsem_ref = rest
    it = pl.program_id(0)
    j = pl.program_id(1)
    last_j = pl.num_programs(1) - 1
    n_tiles = tiles_ref[it]
    tf = wg_ref.shape[1]
    chunk = x_ref.shape[0]
    big = FFN_TRIP_TILES[0]
    row0 = blk_ref[it] * chunk

    def out_copy(i):
        r = pl.multiple_of(i * FFN_TILE, FFN_TILE)
        return pltpu.make_async_copy(
            acc_ref.at[pl.ds(r, FFN_TILE), :],
            y_hbm.at[pl.ds(pl.multiple_of(row0 + r, FFN_TILE), FFN_TILE), :], sem_ref.at[i])

    def res_copy(i):
        r = pl.multiple_of(i * FFN_TILE, FFN_TILE)
        return pltpu.make_async_copy(
            res_hbm.at[pl.ds(pl.multiple_of(row0 + r, FFN_TILE), FFN_TILE), :],
            acc_ref.at[pl.ds(r, FFN_TILE), :], rsem_ref.at[0])

    def a_copy(p):
        r = pl.multiple_of(row0 + p * (big * FFN_TILE), big * FFN_TILE)
        return pltpu.make_async_copy(astage_ref.at[p % 2],
                                     a_hbm.at[pl.ds(r, big * FFN_TILE), :], asem_ref.at[p % 2])

    def ffn_rows(start, n_rows, first):
        rows = pl.ds(pl.multiple_of(start, n_rows), n_rows)
        gu = jnp.dot(x_ref[rows, :], wgu_ref[...], preferred_element_type=F32)
        gate = gu[:, :tf]
        act = ((gate * jax.nn.sigmoid(gate)) * gu[:, tf:]).astype(BF16)
        d = jnp.dot(act, wdb_ref[...], preferred_element_type=F32)
        if first:
            acc_ref[rows, :] = d
        else:
            acc_ref[rows, :] += d

    def run(first, last):
        def trip(tile0, tiles):
            ffn_rows(tile0 * FFN_TILE, tiles * FFN_TILE, first)
            if last:
                for i in range(tiles):
                    out_copy(tile0 + i).start()

        def big_trip(p, carry):
            trip(pl.multiple_of(p * big, big), big)
            if last and fused:
                @pl.when(p >= 2)
                def _():
                    a_copy(p - 2).wait()
                rows = pl.ds(pl.multiple_of(p * (big * FFN_TILE), big * FFN_TILE), big * FFN_TILE)
                astage_ref[p % 2] = _rms(acc_ref[rows, :], g_ref[...]).astype(BF16)
                a_copy(p).start()
            return carry
        n_big = n_tiles // big
        lax.fori_loop(0, n_big, big_trip, 0)
        if not fused:
            done = n_big * big
            for tiles in FFN_TRIP_TILES[1:]:
                take = ((n_tiles - done) // tiles) > 0

                @pl.when(take)
                def _(done=done, tiles=tiles):
                    trip(pl.multiple_of(done, tiles), tiles)
                done = done + jnp.where(take, tiles, 0)

        if last:
            def drain(i, carry):
                out_copy(i).wait()
                return carry
            lax.fori_loop(0, n_tiles, drain, 0)
            if fused:
                def drain_a(p, carry):
                    a_copy(p).wait()
                    return carry
                lax.fori_loop(jnp.maximum(n_big - 2, 0), n_big, drain_a, 0)

    zero_rows = zero_ref.shape[0]

    def zero_copy(i, q):
        row = pl.multiple_of(it * chunk + i * FFN_TILE + q * zero_rows, zero_rows)
        return pltpu.make_async_copy(zero_ref, y_hbm.at[pl.ds(row, zero_rows), :], zsem_ref.at[0])

    def zero_fill():
        def start(i, carry):
            for q in range(FFN_TILE // zero_rows):
                zero_copy(i, q).start()
            return carry

        def wait(i, carry):
            for q in range(FFN_TILE // zero_rows):
                zero_copy(i, q).wait()
            return carry
        lax.fori_loop(n_tiles, chunk // FFN_TILE, start, 0)
        lax.fori_loop(n_tiles, chunk // FFN_TILE, wait, 0)

    @pl.when((it == 0) & (j == 0))
    def _():
        zero_ref[...] = jnp.zeros_like(zero_ref)

    @pl.when(j == jnp.where(n_tiles > 0, last_j, 0))
    def _():
        zero_fill()

    @pl.when(n_tiles > 0)
    def _():
        wgu_ref[:, :tf] = wg_ref[...].astype(BF16)
        wgu_ref[:, tf:] = wu_ref[...].astype(BF16)
        wdb_ref[...] = wd_ref[...].astype(BF16)

        @pl.when(j == 0)
        def _():
            if fused:
                def start(i, carry):
                    res_copy(i).start()
                    return carry

                def wait(i, carry):
                    res_copy(i).wait()
                    return carry
                lax.fori_loop(0, n_tiles, start, 0)
                lax.fori_loop(0, n_tiles, wait, 0)
            run(not fused, False)

        @pl.when((j > 0) & (j < last_j))
        def _():
            run(False, False)

        @pl.when(j == last_j)
        def _():
            run(False, True)


def _grouped_ffn(xs, wg, wu, wd, layer, item_expert, item_tiles, item_blk, out_rows, chunk,
                 residual=None):
    n_items = item_expert.shape[0]
    nj = D_FF // FFN_TF
    fused = residual is not None

    def col(it, j, e, t, b):
        return jnp.where(t[it] > 0, j, nj - 1)

    in_specs = [
        pl.BlockSpec((chunk, D_MODEL), lambda it, j, e, t, b: (b[it], 0)),
        pl.BlockSpec((None, None, D_MODEL, FFN_TF),
                     lambda it, j, e, t, b: (layer, e[it], 0, col(it, j, e, t, b))),
        pl.BlockSpec((None, None, D_MODEL, FFN_TF),
                     lambda it, j, e, t, b: (layer, e[it], 0, col(it, j, e, t, b))),
        pl.BlockSpec((None, None, FFN_TF, D_MODEL),
                     lambda it, j, e, t, b: (layer, e[it], col(it, j, e, t, b), 0)),
    ]
    args = [item_expert, item_tiles, item_blk, xs, wg, wu, wd]
    out_specs = pl.BlockSpec(memory_space=pl.ANY)
    out_shape = jax.ShapeDtypeStruct((out_rows, D_MODEL), F32)
    scratch = [pltpu.VMEM((D_MODEL, 2 * FFN_TF), BF16),
               pltpu.VMEM((FFN_TF, D_MODEL), BF16),
               pltpu.VMEM((chunk, D_MODEL), F32),
               pltpu.VMEM((FFN_ZERO_ROWS, D_MODEL), F32),
               pltpu.SemaphoreType.DMA((chunk // FFN_TILE,)),
               pltpu.SemaphoreType.DMA((1,))]
    if fused:
        res, g3, g_layer = residual
        in_specs += [pl.BlockSpec(memory_space=pl.ANY),
                     pl.BlockSpec((None, 1, D_MODEL), lambda it, j, e, t, b: (g_layer, 0, 0))]
        args += [res, g3]
        out_specs = [out_specs, pl.BlockSpec(memory_space=pl.ANY)]
        out_shape = [out_shape, jax.ShapeDtypeStruct((out_rows, D_MODEL), BF16)]
        scratch += [pltpu.VMEM((2, FFN_TRIP_TILES[0] * FFN_TILE, D_MODEL), BF16),
                    pltpu.SemaphoreType.DMA((2,)),
                    pltpu.SemaphoreType.DMA((1,))]
    return pl.pallas_call(
        functools.partial(_ffn_kernel, fused=fused),
        grid_spec=pltpu.PrefetchScalarGridSpec(
            num_scalar_prefetch=3,
            grid=(n_items, nj),
            in_specs=in_specs,
            out_specs=out_specs,
            scratch_shapes=scratch),
        out_shape=out_shape,
        compiler_params=_params(2),
        name="grouped_ffn",
    )(*args)


def _dense_ffn(f, wg, wu, wd, layer, h, g3, g_layer):
    n_items = N_TOK // DENSE_CHUNK
    assert DENSE_CHUNK % (FFN_TRIP_TILES[0] * FFN_TILE) == 0
    return _grouped_ffn(
        f, wg[:, None], wu[:, None], wd[:, None], layer,
        jnp.zeros((n_items,), I32), jnp.full((n_items,), DENSE_CHUNK // FFN_TILE, I32),
        jnp.arange(n_items, dtype=I32), N_TOK, DENSE_CHUNK, residual=(h, g3, g_layer))


def _combine_kernel(pos_ref, h_ref, gate_ref, g_ref, y_hbm, *rest, final):
    if final:
        o_ref, buf_ref, sem_ref = rest
    else:
        hn_ref, a_ref, buf_ref, sem_ref = rest
    t = pl.program_id(0)
    last = pl.num_programs(0) - 1
    tm = h_ref.shape[0]

    def issue(tile, slot):
        def body(r, carry):
            for k in range(TOP_K):
                _row_copy(y_hbm, pos_ref[k * N_TOK + tile * tm + r],
                          buf_ref, slot, k * tm + r, sem_ref).start()
            return carry
        lax.fori_loop(0, tm, body, 0, unroll=DMA_ISSUE_UNROLL)

    @pl.when(t == 0)
    def _():
        issue(0, 0)

    @pl.when(t < last)
    def _():
        issue(t + 1, (t + 1) % 2)

    slot = t % 2
    for r in range(TOP_K * tm):
        _row_copy(y_hbm, 0, buf_ref, slot, r, sem_ref).wait()
    hn = (h_ref[...] + gate_ref[:, 0:1] * buf_ref[slot, pl.ds(0, tm), :]
          + gate_ref[:, 1:2] * buf_ref[slot, pl.ds(tm, tm), :])
    if final:
        o_ref[...] = _rms(hn, g_ref[...])
    else:
        hn_ref[...] = hn
        a_ref[...] = _rms(hn, g_ref[...]).astype(a_ref.dtype)


def _combine(h, y, pos, gates_t, g3, layer, final, tm=256):
    row = pl.BlockSpec((tm, D_MODEL), lambda t, p: (t, 0))
    if final:
        out_specs = row
        out_shape = jax.ShapeDtypeStruct((N_TOK, D_MODEL), F32)
    else:
        out_specs = [row, row]
        out_shape = [jax.ShapeDtypeStruct((N_TOK, D_MODEL), F32),
                     jax.ShapeDtypeStruct((N_TOK, D_MODEL), BF16)]
    return pl.pallas_call(
        functools.partial(_combine_kernel, final=final),
        grid_spec=pltpu.PrefetchScalarGridSpec(
            num_scalar_prefetch=1,
            grid=(N_TOK // tm,),
            in_specs=[row,
                      pl.BlockSpec((tm, TOP_K), lambda t, p: (t, 0)),
                      pl.BlockSpec((None, 1, D_MODEL), lambda t, p: (layer, 0, 0)),
                      pl.BlockSpec(memory_space=pl.ANY)],
            out_specs=out_specs,
            scratch_shapes=[pltpu.VMEM((2, TOP_K * tm, D_MODEL), F32),
                            pltpu.SemaphoreType.DMA((2,))]),
        out_shape=out_shape,
        compiler_params=_params(1),
        name="moe_combine",
    )(pos.reshape(-1), h, gates_t, g3, y)


def _alibi_slopes():
    n = N_DIL * N_HEADS
    s = jnp.exp2(-8.0 * jnp.arange(1, n + 1, dtype=F32) / n)
    return s.reshape(N_HEADS, N_DIL).T


def kernel(x, mix_norm_g, ffn_norm_g, attn_w_in, attn_w_out, gm_w_in, gm_v_norm_g, gm_w_s,
           gm_b_s, gm_w_out, dense_w_gate, dense_w_up, dense_w_down, router_w, moe_w_gate,
           moe_w_up, moe_w_down, final_norm_g):
    h = x.reshape(N_TOK, D_MODEL)
    mix_g = mix_norm_g.reshape(DEPTH, 1, D_MODEL)
    ffn_g = ffn_norm_g.reshape(DEPTH, 1, D_MODEL)
    final_g = final_norm_g.reshape(1, 1, D_MODEL)
    gm_vg = gm_v_norm_g.reshape(-1, 1, GM_WIDTH)
    gm_bt = jnp.swapaxes(gm_b_s, 1, 2)
    router_wt = jnp.swapaxes(router_w, 1, 2)
    slopes = _alibi_slopes()

    a = _rmsnorm(h, mix_g, 0, BF16)
    for i in range(DEPTH):
        j = i // 2
        if i % 2 == 0:
            groups = [_mm_qkv(a, attn_w_in, j, dil, g * (QKV_GROUP_COLS // 1024))
                      for g, (win, dil) in enumerate(DIL_CONFIGS)]
            h, f = _proj_norm(_attention(groups, slopes), attn_w_out, j, h, ffn_g, i)
            h, a = _dense_ffn(f, dense_w_gate, dense_w_up, dense_w_down, j, h, mix_g, i + 1)
        else:
            z = _mm_act(a, gm_w_in, j, 2 * GM_WIDTH, "gelu")
            y = _gm_spatial(z, gm_vg, gm_w_s, gm_bt, j)
            h, idx, gates, rank, cnt = _proj_router(y, gm_w_out, j, h, ffn_g, i, router_wt, j)
            pos, src, tile_valid, item_expert, item_tiles, item_blk = _dispatch_plan(idx, rank, cnt)
            xs = _dispatch(h, src, tile_valid, ffn_g, i)
            y = _grouped_ffn(xs, moe_w_gate, moe_w_up, moe_w_down, j,
                             item_expert, item_tiles, item_blk, MOE_ROWS, FFN_CHUNK)
            if i == DEPTH - 1:
                return _combine(h, y, pos, gates.T, final_g, 0, True).reshape(BATCH, SEQ, D_MODEL)
            h, a = _combine(h, y, pos, gates.T, mix_g, i + 1, False)
```

```python
import functools

import jax
import jax.numpy as jnp
from jax import lax
from jax.experimental import pallas as pl
from jax.experimental.pallas import tpu as pltpu

D_MODEL = 2048
BATCH = 2
SEQ = 4096
DEPTH = 4
N_TOK = BATCH * SEQ
HEAD_DIM = 128
N_HEADS = D_MODEL // HEAD_DIM
DIL_CONFIGS = ((128, 1), (512, 4), (2048, 16))
N_DIL = len(DIL_CONFIGS)
ATT_BLK = 128
ATT_SKEW = 3
QKV_GROUP_COLS = 3 * N_HEADS * HEAD_DIM
GM_CHUNK = 128
GM_WIDTH = D_MODEL
GM_GROUP_DIM = 128
GM_GROUPS = GM_WIDTH // GM_GROUP_DIM
D_FF = 7 * D_MODEL // 2
N_EXPERTS = 8
TOP_K = 2
EPS = 1e-6
NEG_INF = -1e30

LANES = 128
VMEM_LIMIT = 56 * 1024 * 1024
CAST_ROWS = 256

FFN_TILE = 128
FFN_TRIP_TILES = (4, 2, 1)
DISPATCH_TILE = 256
FFN_CHUNK = 2304
FFN_TILES_PER_CHUNK = FFN_CHUNK // FFN_TILE
DENSE_CHUNK = 2048
FFN_TF = 256
DMA_ISSUE_UNROLL = 8
FFN_ZERO_ROWS = 64
MOE_MAX_ITEMS = -(-TOP_K * N_TOK // FFN_CHUNK) + N_EXPERTS
MOE_ROWS = MOE_MAX_ITEMS * FFN_CHUNK

F32 = jnp.float32
BF16 = jnp.bfloat16
I32 = jnp.int32


def _params(n_axes):
    return pltpu.CompilerParams(
        dimension_semantics=("arbitrary",) * n_axes, vmem_limit_bytes=VMEM_LIMIT)


def _cast_weight(w_ref, wb_ref):
    def body(c, carry):
        r = pl.multiple_of(c * CAST_ROWS, CAST_ROWS)
        wb_ref[pl.ds(r, CAST_ROWS), :] = w_ref[pl.ds(r, CAST_ROWS), :].astype(BF16)
        return carry
    lax.fori_loop(0, w_ref.shape[0] // CAST_ROWS, body, 0)


def _gelu_tanh(x):
    return 0.5 * x * (1.0 + jnp.tanh(0.7978845608028654 * (x + 0.044715 * (x * x * x))))


def _rms(x, g):
    ms = jnp.mean(x * x, axis=-1, keepdims=True)
    return (x * lax.rsqrt(ms + EPS)) * g


def _rms_kernel(x_ref, g_ref, o_ref):
    o_ref[...] = _rms(x_ref[...], g_ref[...]).astype(o_ref.dtype)


def _rmsnorm(h, g3, layer, out_dtype, tm=512):
    return pl.pallas_call(
        _rms_kernel,
        grid=(N_TOK // tm,),
        in_specs=[pl.BlockSpec((tm, D_MODEL), lambda i: (i, 0)),
                  pl.BlockSpec((None, 1, D_MODEL), lambda i: (layer, 0, 0))],
        out_specs=pl.BlockSpec((tm, D_MODEL), lambda i: (i, 0)),
        out_shape=jax.ShapeDtypeStruct((N_TOK, D_MODEL), out_dtype),
        compiler_params=_params(1),
        name="rmsnorm",
    )(h, g3)


def _w_spec(layer, k, tn, col_off):
    return pl.BlockSpec((None, k, tn), lambda j, i: (layer, 0, col_off + j))


def _mm_act_kernel(a_ref, w_ref, o_ref, wb_ref, *, act):
    @pl.when(pl.program_id(1) == 0)
    def _():
        _cast_weight(w_ref, wb_ref)
    acc = jnp.dot(a_ref[...], wb_ref[...], preferred_element_type=F32)
    if act == "gelu":
        acc = _gelu_tanh(acc)
    o_ref[...] = acc.astype(o_ref.dtype)


def _mm_act(a, w, layer, n_cols, act, tm=1024, tn=1024, col_off=0):
    k = a.shape[1]
    return pl.pallas_call(
        functools.partial(_mm_act_kernel, act=act),
        grid=(n_cols // tn, N_TOK // tm),
        in_specs=[pl.BlockSpec((tm, k), lambda j, i: (i, 0)),
                  _w_spec(layer, k, tn, col_off)],
        out_specs=pl.BlockSpec((tm, tn), lambda j, i: (i, j)),
        out_shape=jax.ShapeDtypeStruct((N_TOK, n_cols), BF16),
        scratch_shapes=[pltpu.VMEM((k, tn), BF16)],
        compiler_params=_params(2),
        name="mm_act",
    )(a, w)


def _mm_qkv_kernel(a_ref, w_ref, o_ref, wb_ref, *rest, dil):
    @pl.when(pl.program_id(1) == 0)
    def _():
        _cast_weight(w_ref, wb_ref)
    acc = jnp.dot(a_ref[...], wb_ref[...], preferred_element_type=F32)
    heads = o_ref.shape[0]
    if dil == 1:
        for hh in range(heads):
            o_ref[hh, 0] = acc[:, hh * HEAD_DIM:(hh + 1) * HEAD_DIM].astype(o_ref.dtype)
    else:
        acc_ref, = rest
        rows = acc_ref.shape[1] // dil
        for hh in range(heads):
            acc_ref[hh] = acc[:, hh * HEAD_DIM:(hh + 1) * HEAD_DIM]

        def regroup(hh, carry):
            for r in range(dil):
                o_ref[hh, r] = acc_ref[hh, pl.ds(r, rows, stride=dil), :].astype(o_ref.dtype)
            return carry
        lax.fori_loop(0, heads, regroup, 0, unroll=dil <= 4)


def _mm_qkv(a, w, layer, dil, col_off, tm=1024, tn=1024):
    k = a.shape[1]
    tiles_per_b = SEQ // tm
    heads = tn // HEAD_DIM
    scratch = [pltpu.VMEM((k, tn), BF16)]
    if dil > 1:
        scratch.append(pltpu.VMEM((heads, tm, HEAD_DIM), F32))
    out = pl.pallas_call(
        functools.partial(_mm_qkv_kernel, dil=dil),
        grid=(QKV_GROUP_COLS // tn, N_TOK // tm),
        in_specs=[pl.BlockSpec((tm, k), lambda j, i: (i, 0)),
                  _w_spec(layer, k, tn, col_off)],
        out_specs=pl.BlockSpec((None, heads, dil, tm // dil, HEAD_DIM),
                               lambda j, i: (i // tiles_per_b, j, 0, i % tiles_per_b, 0)),
        out_shape=jax.ShapeDtypeStruct(
            (BATCH, 3 * N_HEADS, dil, SEQ // dil, HEAD_DIM), BF16),
        scratch_shapes=scratch,
        compiler_params=_params(2),
        name="mm_qkv",
    )(a, w)
    return out.reshape(BATCH, 3 * N_HEADS, SEQ, HEAD_DIM)


def _proj_kernel(a_ref, w_ref, res_ref, o_ref, wb_ref):
    @pl.when(pl.program_id(1) == 0)
    def _():
        _cast_weight(w_ref, wb_ref)
    o_ref[...] = res_ref[...] + jnp.dot(a_ref[...], wb_ref[...], preferred_element_type=F32)


def _proj(a, w, layer, res, tm=512, tn=1024):
    k = a.shape[1]
    return pl.pallas_call(
        _proj_kernel,
        grid=(D_MODEL // tn, N_TOK // tm),
        in_specs=[pl.BlockSpec((tm, k), lambda j, i: (i, 0)),
                  _w_spec(layer, k, tn, 0),
                  pl.BlockSpec((tm, tn), lambda j, i: (i, j))],
        out_specs=pl.BlockSpec((tm, tn), lambda j, i: (i, j)),
        out_shape=jax.ShapeDtypeStruct((N_TOK, D_MODEL), F32),
        scratch_shapes=[pltpu.VMEM((k, tn), BF16)],
        compiler_params=_params(2),
        name="proj",
    )(a, w, res)


def _proj_norm_kernel(a_ref, w_ref, res_ref, g_ref, o_ref, f_ref, wb_ref):
    @pl.when(pl.program_id(1) == 0)
    def _():
        _cast_weight(w_ref, wb_ref)
    a = jnp.concatenate([a_ref[hh] for hh in range(a_ref.shape[0])], axis=1)
    out = res_ref[...] + jnp.dot(a, wb_ref[...], preferred_element_type=F32)
    o_ref[...] = out
    f_ref[...] = _rms(out, g_ref[...]).astype(f_ref.dtype)


def _proj_norm(a, w, layer, res, g3, g_layer, tm=256):
    tiles_per_b = SEQ // tm
    k = a.shape[1] * a.shape[3]
    row = pl.BlockSpec((tm, D_MODEL), lambda j, i: (i, 0))
    return pl.pallas_call(
        _proj_norm_kernel,
        grid=(1, N_TOK // tm),
        in_specs=[pl.BlockSpec((None, a.shape[1], tm, a.shape[3]),
                               lambda j, i: (i // tiles_per_b, 0, i % tiles_per_b, 0)),
                  _w_spec(layer, k, D_MODEL, 0),
                  row,
                  pl.BlockSpec((None, 1, D_MODEL), lambda j, i: (g_layer, 0, 0))],
        out_specs=[row, row],
        out_shape=[jax.ShapeDtypeStruct((N_TOK, D_MODEL), F32),
                   jax.ShapeDtypeStruct((N_TOK, D_MODEL), BF16)],
        scratch_shapes=[pltpu.VMEM((k, D_MODEL), BF16)],
        compiler_params=_params(2),
        name="proj_norm",
    )(a, w, res, g3)


def _attn_kernel(slopes_ref, *refs):
    qkv_refs = refs[:9]
    o_ref, tab_ref, oscr_ref, lscr_ref = refs[9:]
    h = pl.program_id(1)
    ii = lax.broadcasted_iota(I32, (ATT_BLK, 2 * ATT_BLK), 0)
    jj = lax.broadcasted_iota(I32, (ATT_BLK, 2 * ATT_BLK), 1)
    delta = ii + ATT_BLK - jj
    scale = HEAD_DIM ** -0.5
    nt = (((1,), (1,)), ((), ()))
    n_blocks = SEQ // ATT_BLK
    ones = jnp.ones((2 * ATT_BLK, HEAD_DIM), BF16)
    for g, (win, dil) in enumerate(DIL_CONFIGS):
        q_ref, k_ref, v_ref = qkv_refs[3 * g:3 * g + 3]
        nb = n_blocks // dil
        tab_ref[...] = jnp.where((delta >= 0) & (delta <= ATT_BLK),
                                 -slopes_ref[g, h] * (dil * delta).astype(F32), NEG_INF)
        s, m, p, done = {}, {}, {}, []

        def keys(c, nb=nb):
            first = c % nb == 0
            return (slice((c - (not first)) * ATT_BLK, (c + 1) * ATT_BLK),
                    slice(ATT_BLK if first else 0, 2 * ATT_BLK))

        def scores(c):
            rows, cols = keys(c)
            q = q_ref[c * ATT_BLK:(c + 1) * ATT_BLK, :]
            s[c] = lax.dot_general(q, k_ref[rows, :], nt,
                                   preferred_element_type=F32) * scale + tab_ref[:, cols]

        def softmax(c):
            m[c] = jnp.max(s[c], axis=-1, keepdims=True)
            p[c] = jnp.exp(s.pop(c) - m[c]).astype(BF16)

        def values(c):
            rows, _ = keys(c)
            vv = jnp.concatenate([v_ref[rows, :], ones[:rows.stop - rows.start]], axis=1)
            od = jnp.dot(p.pop(c), vv, preferred_element_type=F32)
            den = od[:, HEAD_DIM:]
            done.append((c, od[:, :HEAD_DIM] / den, m.pop(c) + jnp.log(den)))

        for step in range(n_blocks + 2 * ATT_SKEW):
            if step < n_blocks:
                scores(step)
            if 0 <= step - ATT_SKEW < n_blocks:
                softmax(step - ATT_SKEW)
            if 0 <= step - 2 * ATT_SKEW < n_blocks:
                values(step - 2 * ATT_SKEW)
        for c, o, lse in done:
            start = (c % nb) * (ATT_BLK * dil) + c // nb
            dst = pl.ds(start, ATT_BLK) if dil == 1 else pl.ds(start, ATT_BLK, stride=dil)
            oscr_ref[g, dst, :] = o
            lscr_ref[g, dst, :] = jnp.broadcast_to(lse, (ATT_BLK, HEAD_DIM))

    rows = 256

    def merge(t, carry):
        sl = pl.ds(pl.multiple_of(t * rows, rows), rows)
        l0 = lscr_ref[0, sl, :]
        l1 = lscr_ref[1, sl, :]
        l2 = lscr_ref[2, sl, :]
        m = jnp.maximum(jnp.maximum(l0, l1), l2)
        e0 = jnp.exp(l0 - m)
        e1 = jnp.exp(l1 - m)
        e2 = jnp.exp(l2 - m)
        den = e0 + e1 + e2
        o = (e0 * oscr_ref[0, sl, :] + e1 * oscr_ref[1, sl, :] + e2 * oscr_ref[2, sl, :]) / den
        o_ref[sl, :] = o.astype(o_ref.dtype)
        return carry

    lax.fori_loop(0, SEQ // rows, merge, 0)


def _attention(qkv_groups, slopes):
    in_specs = [pl.BlockSpec(memory_space=pltpu.SMEM)]
    args = [slopes]
    for qkv in qkv_groups:
        for part in range(3):
            in_specs.append(pl.BlockSpec(
                (None, None, SEQ, HEAD_DIM),
                lambda b, h, part=part: (b, part * N_HEADS + h, 0, 0)))
            args.append(qkv)
    return pl.pallas_call(
        _attn_kernel,
        grid=(BATCH, N_HEADS),
        in_specs=in_specs,
        out_specs=pl.BlockSpec((None, None, SEQ, HEAD_DIM), lambda b, h: (b, h, 0, 0)),
        out_shape=jax.ShapeDtypeStruct((BATCH, N_HEADS, SEQ, HEAD_DIM), BF16),
        scratch_shapes=[pltpu.VMEM((ATT_BLK, 2 * ATT_BLK), F32),
                        pltpu.VMEM((N_DIL, SEQ, HEAD_DIM), F32),
                        pltpu.VMEM((N_DIL, SEQ, HEAD_DIM), F32)],
        compiler_params=_params(2),
        name="dilated_attn",
    )(*args)


def _gm_spatial_kernel(z_ref, vg_ref, ws_ref, bt_ref, y_ref, wsb_ref):
    @pl.when(pl.program_id(0) == 0)
    def _():
        ii = lax.broadcasted_iota(I32, (GM_CHUNK, GM_CHUNK), 0)
        jj = lax.broadcasted_iota(I32, (GM_CHUNK, GM_CHUNK), 1)
        for g in range(GM_GROUPS):
            wsb_ref[g] = jnp.where(ii >= jj, ws_ref[g], 0.0).astype(BF16)

    for c in range(z_ref.shape[0] // GM_CHUNK):
        rows = slice(c * GM_CHUNK, (c + 1) * GM_CHUNK)
        vn = _rms(z_ref[rows, GM_WIDTH:].astype(F32), vg_ref[...]).astype(BF16)
        for g in range(GM_GROUPS):
            cols = slice(g * GM_GROUP_DIM, (g + 1) * GM_GROUP_DIM)
            s = jnp.dot(wsb_ref[g], vn[:, cols], preferred_element_type=F32) + bt_ref[:, g:g + 1]
            y_ref[rows, cols] = (z_ref[rows, cols].astype(F32) * s).astype(y_ref.dtype)


def _gm_spatial(z, vg3, ws, bt, layer, tm=256):
    return pl.pallas_call(
        _gm_spatial_kernel,
        grid=(N_TOK // tm,),
        in_specs=[pl.BlockSpec((tm, 2 * GM_WIDTH), lambda i: (i, 0)),
                  pl.BlockSpec((None, 1, GM_WIDTH), lambda i: (layer, 0, 0)),
                  pl.BlockSpec((None, GM_GROUPS, GM_CHUNK, GM_CHUNK), lambda i: (layer, 0, 0, 0)),
                  pl.BlockSpec((None, GM_CHUNK, GM_GROUPS), lambda i: (layer, 0, 0))],
        out_specs=pl.BlockSpec((tm, GM_WIDTH), lambda i: (i, 0)),
        out_shape=jax.ShapeDtypeStruct((N_TOK, GM_WIDTH), BF16),
        scratch_shapes=[pltpu.VMEM((GM_GROUPS, GM_CHUNK, GM_CHUNK), BF16)],
        compiler_params=_params(1),
        name="gm_spatial",
    )(z, vg3, ws, bt)


def _router_kernel(h_ref, g_ref, rwt_ref, idx_ref, gate_ref, rank_ref, cnt_ref, run_ref):
    @pl.when(pl.program_id(0) == 0)
    def _():
        run_ref[...] = jnp.zeros_like(run_ref)

    f = _rms(h_ref[...], g_ref[...])
    logits = lax.dot_general(rwt_ref[...], f, (((1,), (1,)), ((), ())),
                             precision=lax.Precision.HIGHEST, preferred_element_type=F32)
    tm = logits.shape[1]
    eid = lax.broadcasted_iota(I32, logits.shape, 0)
    m1 = jnp.max(logits, axis=0, keepdims=True)
    i1 = jnp.min(jnp.where(logits == m1, eid, N_EXPERTS), axis=0, keepdims=True)
    rest = jnp.where(eid == i1, -jnp.inf, logits)
    m2 = jnp.max(rest, axis=0, keepdims=True)
    i2 = jnp.min(jnp.where(rest == m2, eid, N_EXPERTS), axis=0, keepdims=True)
    e2 = jnp.exp(m2 - m1)
    den = 1.0 + e2
    idx_ref[0:1, :] = i1
    idx_ref[1:2, :] = i2
    gate_ref[0:1, :] = 1.0 / den
    gate_ref[1:2, :] = e2 / den

    sel1 = eid == i1
    sel2 = eid == i2
    onehot = jnp.where(sel1, 1.0, jnp.where(sel2, 1.0, 0.0))
    earlier = (lax.broadcasted_iota(I32, (tm, tm), 0)
               < lax.broadcasted_iota(I32, (tm, tm), 1))
    before = jnp.dot(onehot.astype(BF16), jnp.where(earlier, 1.0, 0.0).astype(BF16),
                     preferred_element_type=F32) + run_ref[:, 0:1]
    rank_ref[0:1, :] = jnp.sum(jnp.where(sel1, before, 0.0), axis=0, keepdims=True).astype(I32)
    rank_ref[1:2, :] = jnp.sum(jnp.where(sel2, before, 0.0), axis=0, keepdims=True).astype(I32)
    run_ref[...] = run_ref[...] + jnp.sum(onehot, axis=1, keepdims=True)
    cnt_ref[...] = run_ref[...]


def _router(h, g3, glayer, rwt, mlayer, tm=512):
    pair = pl.BlockSpec((TOP_K, tm), lambda i: (0, i))
    return pl.pallas_call(
        _router_kernel,
        grid=(N_TOK // tm,),
        in_specs=[pl.BlockSpec((tm, D_MODEL), lambda i: (i, 0)),
                  pl.BlockSpec((None, 1, D_MODEL), lambda i: (glayer, 0, 0)),
                  pl.BlockSpec((None, N_EXPERTS, D_MODEL), lambda i: (mlayer, 0, 0))],
        out_specs=[pair, pair, pair, pl.BlockSpec((N_EXPERTS, LANES), lambda i: (0, 0))],
        out_shape=[jax.ShapeDtypeStruct((TOP_K, N_TOK), I32),
                   jax.ShapeDtypeStruct((TOP_K, N_TOK), F32),
                   jax.ShapeDtypeStruct((TOP_K, N_TOK), I32),
                   jax.ShapeDtypeStruct((N_EXPERTS, LANES), F32)],
        scratch_shapes=[pltpu.VMEM((N_EXPERTS, LANES), F32)],
        compiler_params=_params(1),
        name="router",
    )(h, g3, rwt)


def _dispatch_plan(idx, rank, cnt):
    counts = cnt[:, 0].astype(I32)
    tiles_e = (counts + FFN_TILE - 1) // FFN_TILE
    items_e = (tiles_e + FFN_TILES_PER_CHUNK - 1) // FFN_TILES_PER_CHUNK
    items_end = jnp.cumsum(items_e)
    items_start = items_end - items_e
    n_items = items_end[-1]
    first_row = items_start * FFN_CHUNK
    pos = rank
    for e in range(N_EXPERTS):
        pos = pos + jnp.where(idx == e, first_row[e], 0)
    it = jnp.arange(MOE_MAX_ITEMS, dtype=I32)
    it_c = jnp.clip(it, 0, jnp.maximum(n_items - 1, 0))
    item_expert = jnp.minimum(jnp.searchsorted(items_end, it_c, side="right"),
                              N_EXPERTS - 1).astype(I32)
    local = it_c - items_start[item_expert]
    item_tiles = jnp.clip(tiles_e[item_expert] - local * FFN_TILES_PER_CHUNK,
                          0, FFN_TILES_PER_CHUNK)
    item_tiles = jnp.where(it < n_items, item_tiles, 0).astype(I32)
    tok = jnp.broadcast_to(jnp.arange(N_TOK, dtype=I32), (TOP_K, N_TOK))
    src = jnp.zeros((MOE_ROWS,), I32).at[pos.reshape(-1)].set(tok.reshape(-1))
    per_chunk = FFN_CHUNK // DISPATCH_TILE
    tile_id = jnp.arange(MOE_ROWS // DISPATCH_TILE, dtype=I32)
    tile_valid = ((tile_id % per_chunk) * (DISPATCH_TILE // FFN_TILE)
                  < item_tiles[tile_id // per_chunk]).astype(I32)
    return pos, src, tile_valid, item_expert, item_tiles, it_c


def _row_copy(src_hbm, row, dst_ref, slot, r, sem_ref):
    return pltpu.make_async_copy(src_hbm.at[pl.ds(row, 1), :],
                                 dst_ref.at[slot, pl.ds(r, 1), :], sem_ref.at[slot])


def _dispatch_kernel(src_ref, valid_ref, h_hbm, g_ref, o_ref, buf_ref, sem_ref):
    p = pl.program_id(0)
    last = pl.num_programs(0) - 1

    def issue(tile, slot):
        def body(r, carry):
            _row_copy(h_hbm, src_ref[tile * DISPATCH_TILE + r], buf_ref, slot, r, sem_ref).start()
            return carry
        lax.fori_loop(0, DISPATCH_TILE, body, 0, unroll=DMA_ISSUE_UNROLL)

    @pl.when((p == 0) & (valid_ref[0] > 0))
    def _():
        issue(0, 0)

    nxt = jnp.minimum(p + 1, last)

    @pl.when((p < last) & (valid_ref[nxt] > 0))
    def _():
        issue(nxt, nxt % 2)

    slot = p % 2

    @pl.when(valid_ref[p] > 0)
    def _():
        for r in range(DISPATCH_TILE):
            _row_copy(h_hbm, 0, buf_ref, slot, r, sem_ref).wait()
        o_ref[...] = _rms(buf_ref[slot], g_ref[...]).astype(o_ref.dtype)

    @pl.when(valid_ref[p] == 0)
    def _():
        o_ref[...] = jnp.zeros_like(o_ref)


def _dispatch(h, src, tile_valid, g3, layer):
    return pl.pallas_call(
        _dispatch_kernel,
        grid_spec=pltpu.PrefetchScalarGridSpec(
            num_scalar_prefetch=2,
            grid=(MOE_ROWS // DISPATCH_TILE,),
            in_specs=[pl.BlockSpec(memory_space=pl.ANY),
                      pl.BlockSpec((None, 1, D_MODEL), lambda p, s, v: (layer, 0, 0))],
            out_specs=pl.BlockSpec((DISPATCH_TILE, D_MODEL), lambda p, s, v: (p, 0)),
            scratch_shapes=[pltpu.VMEM((2, DISPATCH_TILE, D_MODEL), F32),
                            pltpu.SemaphoreType.DMA((2,))]),
        out_shape=jax.ShapeDtypeStruct((MOE_ROWS, D_MODEL), BF16),
        compiler_params=_params(1),
        name="moe_dispatch",
    )(src, tile_valid, h, g3)


def _ffn_kernel(exp_ref, tiles_ref, blk_ref, x_ref, wg_ref, wu_ref, wd_ref, *rest, fused):
    if fused:
        (res_hbm, g_ref, y_hbm, a_hbm, wgu_ref, wdb_ref, acc_ref, zero_ref, sem_ref, zsem_ref,
         astage_ref, asem_ref, rsem_ref) = rest
    else:
        y_hbm, wgu_ref, wdb_ref, acc_ref, zero_ref, sem_ref, zsem_ref = rest
    it = pl.program_id(0)
    j = pl.program_id(1)
    last_j = pl.num_programs(1) - 1
    n_tiles = tiles_ref[it]
    tf = wg_ref.shape[1]
    chunk = x_ref.shape[0]
    big = FFN_TRIP_TILES[0]
    row0 = blk_ref[it] * chunk

    def out_copy(i):
        r = pl.multiple_of(i * FFN_TILE, FFN_TILE)
        return pltpu.make_async_copy(
            acc_ref.at[pl.ds(r, FFN_TILE), :],
            y_hbm.at[pl.ds(pl.multiple_of(row0 + r, FFN_TILE), FFN_TILE), :], sem_ref.at[i])

    def res_copy(i):
        r = pl.multiple_of(i * FFN_TILE, FFN_TILE)
        return pltpu.make_async_copy(
            res_hbm.at[pl.ds(pl.multiple_of(row0 + r, FFN_TILE), FFN_TILE), :],
            acc_ref.at[pl.ds(r, FFN_TILE), :], rsem_ref.at[0])

    def a_copy(p):
        r = pl.multiple_of(row0 + p * (big * FFN_TILE), big * FFN_TILE)
        return pltpu.make_async_copy(astage_ref.at[p % 2],
                                     a_hbm.at[pl.ds(r, big * FFN_TILE), :], asem_ref.at[p % 2])

    def ffn_rows(start, n_rows, first):
        rows = pl.ds(pl.multiple_of(start, FFN_TILE), n_rows)
        gu = jnp.dot(x_ref[rows, :], wgu_ref[...], preferred_element_type=F32)
        gate = gu[:, :tf]
        act = ((gate * jax.nn.sigmoid(gate)) * gu[:, tf:]).astype(BF16)
        d = jnp.dot(act, wdb_ref[...], preferred_element_type=F32)
        if first:
            acc_ref[rows, :] = d
        else:
            acc_ref[rows, :] += d

    def run(first, last):
        def trip(tile0, tiles):
            ffn_rows(tile0 * FFN_TILE, tiles * FFN_TILE, first)
            if last:
                for i in range(tiles):
                    out_copy(tile0 + i).start()

        def big_trip(p, carry):
            trip(pl.multiple_of(p * big, big), big)
            if last and fused:
                @pl.when(p >= 2)
                def _():
                    a_copy(p - 2).wait()
                rows = pl.ds(pl.multiple_of(p * (big * FFN_TILE), big * FFN_TILE), big * FFN_TILE)
                astage_ref[p % 2] = _rms(acc_ref[rows, :], g_ref[...]).astype(BF16)
                a_copy(p).start()
            return carry
        n_big = n_tiles // big
        if fused:
            lax.fori_loop(0, n_big, big_trip, 0)
        else:
            merge = (n_tiles - n_big * big == 1) & (n_big > 0)
            n_loop = n_big - jnp.where(merge, 1, 0)
            lax.fori_loop(0, n_loop, big_trip, 0)

            @pl.when(merge)
            def _():
                trip(pl.multiple_of(n_loop * big, big), big + 1)
            done = jnp.where(merge, n_tiles, n_big * big)
            for tiles in FFN_TRIP_TILES[1:]:
                take = ((n_tiles - done) // tiles) > 0

                @pl.when(take)
                def _(done=done, tiles=tiles):
                    trip(pl.multiple_of(done, tiles), tiles)
                done = done + jnp.where(take, tiles, 0)

        if last:
            def drain(i, carry):
                out_copy(i).wait()
                return carry
            lax.fori_loop(0, n_tiles, drain, 0)
            if fused:
                def drain_a(p, carry):
                    a_copy(p).wait()
                    return carry
                lax.fori_loop(jnp.maximum(n_big - 2, 0), n_big, drain_a, 0)

    zero_rows = zero_ref.shape[0]

    def zero_copy(i, q):
        row = pl.multiple_of(it * chunk + i * FFN_TILE + q * zero_rows, zero_rows)
        return pltpu.make_async_copy(zero_ref, y_hbm.at[pl.ds(row, zero_rows), :], zsem_ref.at[0])

    def zero_fill():
        def start(i, carry):
            for q in range(FFN_TILE // zero_rows):
                zero_copy(i, q).start()
            return carry

        def wait(i, carry):
            for q in range(FFN_TILE // zero_rows):
                zero_copy(i, q).wait()
            return carry
        lax.fori_loop(n_tiles, chunk // FFN_TILE, start, 0)
        lax.fori_loop(n_tiles, chunk // FFN_TILE, wait, 0)

    @pl.when((it == 0) & (j == 0))
    def _():
        zero_ref[...] = jnp.zeros_like(zero_ref)

    @pl.when(j == jnp.where(n_tiles > 0, last_j, 0))
    def _():
        zero_fill()

    @pl.when(n_tiles > 0)
    def _():
        wgu_ref[:, :tf] = wg_ref[...].astype(BF16)
        wgu_ref[:, tf:] = wu_ref[...].astype(BF16)
        wdb_ref[...] = wd_ref[...].astype(BF16)

        @pl.when(j == 0)
        def _():
            if fused:
                def start(i, carry):
                    res_copy(i).start()
                    return carry

                def wait(i, carry):
                    res_copy(i).wait()
                    return carry
                lax.fori_loop(0, n_tiles, start, 0)
                lax.fori_loop(0, n_tiles, wait, 0)
            run(not fused, False)

        @pl.when((j > 0) & (j < last_j))
        def _():
            run(False, False)

        @pl.when(j == last_j)
        def _():
            run(False, True)


def _grouped_ffn(xs, wg, wu, wd, layer, item_expert, item_tiles, item_blk, out_rows, chunk,
                 residual=None):
    n_items = item_expert.shape[0]
    nj = D_FF // FFN_TF
    fused = residual is not None

    def col(it, j, e, t, b):
        return jnp.where(t[it] > 0, j, nj - 1)

    in_specs = [
        pl.BlockSpec((chunk, D_MODEL), lambda it, j, e, t, b: (b[it], 0)),
        pl.BlockSpec((None, None, D_MODEL, FFN_TF),
                     lambda it, j, e, t, b: (layer, e[it], 0, col(it, j, e, t, b))),
        pl.BlockSpec((None, None, D_MODEL, FFN_TF),
                     lambda it, j, e, t, b: (layer, e[it], 0, col(it, j, e, t, b))),
        pl.BlockSpec((None, None, FFN_TF, D_MODEL),
                     lambda it, j, e, t, b: (layer, e[it], col(it, j, e, t, b), 0)),
    ]
    args = [item_expert, item_tiles, item_blk, xs, wg, wu, wd]
    out_specs = pl.BlockSpec(memory_space=pl.ANY)
    out_shape = jax.ShapeDtypeStruct((out_rows, D_MODEL), F32)
    scratch = [pltpu.VMEM((D_MODEL, 2 * FFN_TF), BF16),
               pltpu.VMEM((FFN_TF, D_MODEL), BF16),
               pltpu.VMEM((chunk, D_MODEL), F32),
               pltpu.VMEM((FFN_ZERO_ROWS, D_MODEL), F32),
               pltpu.SemaphoreType.DMA((chunk // FFN_TILE,)),
               pltpu.SemaphoreType.DMA((1,))]
    if fused:
        res, g3, g_layer = residual
        in_specs += [pl.BlockSpec(memory_space=pl.ANY),
                     pl.BlockSpec((None, 1, D_MODEL), lambda it, j, e, t, b: (g_layer, 0, 0))]
        args += [res, g3]
        out_specs = [out_specs, pl.BlockSpec(memory_space=pl.ANY)]
        out_shape = [out_shape, jax.ShapeDtypeStruct((out_rows, D_MODEL), BF16)]
        scratch += [pltpu.VMEM((2, FFN_TRIP_TILES[0] * FFN_TILE, D_MODEL), BF16),
                    pltpu.SemaphoreType.DMA((2,)),
                    pltpu.SemaphoreType.DMA((1,))]
    return pl.pallas_call(
        functools.partial(_ffn_kernel, fused=fused),
        grid_spec=pltpu.PrefetchScalarGridSpec(
            num_scalar_prefetch=3,
            grid=(n_items, nj),
            in_specs=in_specs,
            out_specs=out_specs,
            scratch_shapes=scratch),
        out_shape=out_shape,
        compiler_params=_params(2),
        name="grouped_ffn",
    )(*args)


def _dense_ffn(f, wg, wu, wd, layer, h, g3, g_layer):
    n_items = N_TOK // DENSE_CHUNK
    assert DENSE_CHUNK % (FFN_TRIP_TILES[0] * FFN_TILE) == 0
    return _grouped_ffn(
        f, wg[:, None], wu[:, None], wd[:, None], layer,
        jnp.zeros((n_items,), I32), jnp.full((n_items,), DENSE_CHUNK // FFN_TILE, I32),
        jnp.arange(n_items, dtype=I32), N_TOK, DENSE_CHUNK, residual=(h, g3, g_layer))


def _combine_kernel(pos_ref, h_ref, gate_ref, g_ref, y_hbm, *rest, final):
    if final:
        o_ref, buf_ref, sem_ref = rest
    else:
        hn_ref, a_ref, buf_ref, sem_ref = rest
    t = pl.program_id(0)
    last = pl.num_programs(0) - 1
    tm = h_ref.shape[0]

    def issue(tile, slot):
        def body(r, carry):
            for k in range(TOP_K):
                _row_copy(y_hbm, pos_ref[k * N_TOK + tile * tm + r],
                          buf_ref, slot, k * tm + r, sem_ref).start()
            return carry
        lax.fori_loop(0, tm, body, 0, unroll=DMA_ISSUE_UNROLL)

    @pl.when(t == 0)
    def _():
        issue(0, 0)

    @pl.when(t < last)
    def _():
        issue(t + 1, (t + 1) % 2)

    slot = t % 2
    for r in range(TOP_K * tm):
        _row_copy(y_hbm, 0, buf_ref, slot, r, sem_ref).wait()
    hn = (h_ref[...] + gate_ref[:, 0:1] * buf_ref[slot, pl.ds(0, tm), :]
          + gate_ref[:, 1:2] * buf_ref[slot, pl.ds(tm, tm), :])
    if final:
        o_ref[...] = _rms(hn, g_ref[...])
    else:
        hn_ref[...] = hn
        a_ref[...] = _rms(hn, g_ref[...]).astype(a_ref.dtype)


def _combine(h, y, pos, gates_t, g3, layer, final, tm=256):
    row = pl.BlockSpec((tm, D_MODEL), lambda t, p: (t, 0))
    if final:
        out_specs = row
        out_shape = jax.ShapeDtypeStruct((N_TOK, D_MODEL), F32)
    else:
        out_specs = [row, row]
        out_shape = [jax.ShapeDtypeStruct((N_TOK, D_MODEL), F32),
                     jax.ShapeDtypeStruct((N_TOK, D_MODEL), BF16)]
    return pl.pallas_call(
        functools.partial(_combine_kernel, final=final),
        grid_spec=pltpu.PrefetchScalarGridSpec(
            num_scalar_prefetch=1,
            grid=(N_TOK // tm,),
            in_specs=[row,
                      pl.BlockSpec((tm, TOP_K), lambda t, p: (t, 0)),
                      pl.BlockSpec((None, 1, D_MODEL), lambda t, p: (layer, 0, 0)),
                      pl.BlockSpec(memory_space=pl.ANY)],
            out_specs=out_specs,
            scratch_shapes=[pltpu.VMEM((2, TOP_K * tm, D_MODEL), F32),
                            pltpu.SemaphoreType.DMA((2,))]),
        out_shape=out_shape,
        compiler_params=_params(1),
        name="moe_combine",
    )(pos.reshape(-1), h, gates_t, g3, y)


def _alibi_slopes():
    n = N_DIL * N_HEADS
    s = jnp.exp2(-8.0 * jnp.arange(1, n + 1, dtype=F32) / n)
    return s.reshape(N_HEADS, N_DIL).T


def kernel(x, mix_norm_g, ffn_norm_g, attn_w_in, attn_w_out, gm_w_in, gm_v_norm_g, gm_w_s,
           gm_b_s, gm_w_out, dense_w_gate, dense_w_up, dense_w_down, router_w, moe_w_gate,
           moe_w_up, moe_w_down, final_norm_g):
    h = x.reshape(N_TOK, D_MODEL)
    mix_g = mix_norm_g.reshape(DEPTH, 1, D_MODEL)
    ffn_g = ffn_norm_g.reshape(DEPTH, 1, D_MODEL)
    final_g = final_norm_g.reshape(1, 1, D_MODEL)
    gm_vg = gm_v_norm_g.reshape(-1, 1, GM_WIDTH)
    gm_bt = jnp.swapaxes(gm_b_s, 1, 2)
    router_wt = jnp.swapaxes(router_w, 1, 2)
    slopes = _alibi_slopes()

    a = _rmsnorm(h, mix_g, 0, BF16)
    for i in range(DEPTH):
        j = i // 2
        if i % 2 == 0:
            groups = [_mm_qkv(a, attn_w_in, j, dil, g * (QKV_GROUP_COLS // 1024))
                      for g, (win, dil) in enumerate(DIL_CONFIGS)]
            h, f = _proj_norm(_attention(groups, slopes), attn_w_out, j, h, ffn_g, i)
            h, a = _dense_ffn(f, dense_w_gate, dense_w_up, dense_w_down, j, h, mix_g, i + 1)
        else:
            z = _mm_act(a, gm_w_in, j, 2 * GM_WIDTH, "gelu")
            y = _gm_spatial(z, gm_vg, gm_w_s, gm_bt, j)
            h = _proj(y, gm_w_out, j, h)
            idx, gates, rank, cnt = _router(h, ffn_g, i, router_wt, j)
            pos, src, tile_valid, item_expert, item_tiles, item_blk = _dispatch_plan(idx, rank, cnt)
            xs = _dispatch(h, src, tile_valid, ffn_g, i)
            y = _grouped_ffn(xs, moe_w_gate, moe_w_up, moe_w_down, j,
                             item_expert, item_tiles, item_blk, MOE_ROWS, FFN_CHUNK)
            if i == DEPTH - 1:
                return _combine(h, y, pos, gates.T, final_g, 0, True).reshape(BATCH, SEQ, D_MODEL)
            h, a = _combine(h, y, pos, gates.T, mix_g, i + 1, False)
```

```python
import functools

import jax
import jax.numpy as jnp
from jax import lax
from jax.experimental import pallas as pl
from jax.experimental.pallas import tpu as pltpu

D_MODEL = 2048
BATCH = 2
SEQ = 4096
DEPTH = 4
N_TOK = BATCH * SEQ
HEAD_DIM = 128
N_HEADS = D_MODEL // HEAD_DIM
DIL_CONFIGS = ((128, 1), (512, 4), (2048, 16))
N_DIL = len(DIL_CONFIGS)
ATT_BLK = 128
ATT_SKEW = 3
QKV_GROUP_COLS = 3 * N_HEADS * HEAD_DIM
GM_CHUNK = 128
GM_WIDTH = D_MODEL
GM_GROUP_DIM = 128
GM_GROUPS = GM_WIDTH // GM_GROUP_DIM
D_FF = 7 * D_MODEL // 2
N_EXPERTS = 8
TOP_K = 2
EPS = 1e-6
NEG_INF = -1e30

LANES = 128
VMEM_LIMIT = 56 * 1024 * 1024
CAST_ROWS = 256

FFN_TILE = 128
FFN_TRIP_TILES = (4, 2, 1)
DISPATCH_TILE = 256
FFN_CHUNK = 2304
FFN_TILES_PER_CHUNK = FFN_CHUNK // FFN_TILE
DENSE_CHUNK = 2048
FFN_TF = 256
DMA_ISSUE_UNROLL = 8
FFN_ZERO_ROWS = 64
MOE_MAX_ITEMS = -(-TOP_K * N_TOK // FFN_CHUNK) + N_EXPERTS
MOE_COMMON_ITEMS = N_EXPERTS + 1
MOE_ROWS = TOP_K * N_TOK + N_EXPERTS * FFN_TILE + FFN_CHUNK
assert MOE_ROWS % DISPATCH_TILE == 0

F32 = jnp.float32
BF16 = jnp.bfloat16
I32 = jnp.int32


def _params(n_axes):
    return pltpu.CompilerParams(
        dimension_semantics=("arbitrary",) * n_axes, vmem_limit_bytes=VMEM_LIMIT)


def _cast_weight(w_ref, wb_ref):
    def body(c, carry):
        r = pl.multiple_of(c * CAST_ROWS, CAST_ROWS)
        wb_ref[pl.ds(r, CAST_ROWS), :] = w_ref[pl.ds(r, CAST_ROWS), :].astype(BF16)
        return carry
    lax.fori_loop(0, w_ref.shape[0] // CAST_ROWS, body, 0)


def _gelu_tanh(x):
    return 0.5 * x * (1.0 + jnp.tanh(0.7978845608028654 * (x + 0.044715 * (x * x * x))))


def _rms(x, g):
    ms = jnp.mean(x * x, axis=-1, keepdims=True)
    return (x * lax.rsqrt(ms + EPS)) * g


def _rms_kernel(x_ref, g_ref, o_ref):
    o_ref[...] = _rms(x_ref[...], g_ref[...]).astype(o_ref.dtype)


def _rmsnorm(h, g3, layer, out_dtype, tm=512):
    return pl.pallas_call(
        _rms_kernel,
        grid=(N_TOK // tm,),
        in_specs=[pl.BlockSpec((tm, D_MODEL), lambda i: (i, 0)),
                  pl.BlockSpec((None, 1, D_MODEL), lambda i: (layer, 0, 0))],
        out_specs=pl.BlockSpec((tm, D_MODEL), lambda i: (i, 0)),
        out_shape=jax.ShapeDtypeStruct((N_TOK, D_MODEL), out_dtype),
        compiler_params=_params(1),
        name="rmsnorm",
    )(h, g3)


def _w_spec(layer, k, tn, col_off):
    return pl.BlockSpec((None, k, tn), lambda j, i: (layer, 0, col_off + j))


def _mm_act_kernel(a_ref, w_ref, o_ref, wb_ref, *, act):
    @pl.when(pl.program_id(1) == 0)
    def _():
        _cast_weight(w_ref, wb_ref)
    acc = jnp.dot(a_ref[...], wb_ref[...], preferred_element_type=F32)
    if act == "gelu":
        acc = _gelu_tanh(acc)
    o_ref[...] = acc.astype(o_ref.dtype)


def _mm_act(a, w, layer, n_cols, act, tm=1024, tn=1024, col_off=0):
    k = a.shape[1]
    return pl.pallas_call(
        functools.partial(_mm_act_kernel, act=act),
        grid=(n_cols // tn, N_TOK // tm),
        in_specs=[pl.BlockSpec((tm, k), lambda j, i: (i, 0)),
                  _w_spec(layer, k, tn, col_off)],
        out_specs=pl.BlockSpec((tm, tn), lambda j, i: (i, j)),
        out_shape=jax.ShapeDtypeStruct((N_TOK, n_cols), BF16),
        scratch_shapes=[pltpu.VMEM((k, tn), BF16)],
        compiler_params=_params(2),
        name="mm_act",
    )(a, w)


def _mm_qkv_kernel(a_ref, w_ref, o_ref, wb_ref, *rest, dil):
    @pl.when(pl.program_id(1) == 0)
    def _():
        _cast_weight(w_ref, wb_ref)
    acc = jnp.dot(a_ref[...], wb_ref[...], preferred_element_type=F32)
    heads = o_ref.shape[0]
    if dil == 1:
        for hh in range(heads):
            o_ref[hh, 0] = acc[:, hh * HEAD_DIM:(hh + 1) * HEAD_DIM].astype(o_ref.dtype)
    else:
        acc_ref, = rest
        rows = acc_ref.shape[1] // dil
        for hh in range(heads):
            acc_ref[hh] = acc[:, hh * HEAD_DIM:(hh + 1) * HEAD_DIM]

        def regroup(hh, carry):
            for r in range(dil):
                o_ref[hh, r] = acc_ref[hh, pl.ds(r, rows, stride=dil), :].astype(o_ref.dtype)
            return carry
        lax.fori_loop(0, heads, regroup, 0, unroll=dil <= 4)


def _mm_qkv(a, w, layer, dil, col_off, tm=1024, tn=1024):
    k = a.shape[1]
    tiles_per_b = SEQ // tm
    heads = tn // HEAD_DIM
    scratch = [pltpu.VMEM((k, tn), BF16)]
    if dil > 1:
        scratch.append(pltpu.VMEM((heads, tm, HEAD_DIM), F32))
    out = pl.pallas_call(
        functools.partial(_mm_qkv_kernel, dil=dil),
        grid=(QKV_GROUP_COLS // tn, N_TOK // tm),
        in_specs=[pl.BlockSpec((tm, k), lambda j, i: (i, 0)),
                  _w_spec(layer, k, tn, col_off)],
        out_specs=pl.BlockSpec((None, heads, dil, tm // dil, HEAD_DIM),
                               lambda j, i: (i // tiles_per_b, j, 0, i % tiles_per_b, 0)),
        out_shape=jax.ShapeDtypeStruct(
            (BATCH, 3 * N_HEADS, dil, SEQ // dil, HEAD_DIM), BF16),
        scratch_shapes=scratch,
        compiler_params=_params(2),
        name="mm_qkv",
    )(a, w)
    return out.reshape(BATCH, 3 * N_HEADS, SEQ, HEAD_DIM)


def _proj_kernel(a_ref, w_ref, res_ref, o_ref, wb_ref):
    @pl.when(pl.program_id(1) == 0)
    def _():
        _cast_weight(w_ref, wb_ref)
    o_ref[...] = res_ref[...] + jnp.dot(a_ref[...], wb_ref[...], preferred_element_type=F32)


def _proj(a, w, layer, res, tm=512, tn=1024):
    k = a.shape[1]
    return pl.pallas_call(
        _proj_kernel,
        grid=(D_MODEL // tn, N_TOK // tm),
        in_specs=[pl.BlockSpec((tm, k), lambda j, i: (i, 0)),
                  _w_spec(layer, k, tn, 0),
                  pl.BlockSpec((tm, tn), lambda j, i: (i, j))],
        out_specs=pl.BlockSpec((tm, tn), lambda j, i: (i, j)),
        out_shape=jax.ShapeDtypeStruct((N_TOK, D_MODEL), F32),
        scratch_shapes=[pltpu.VMEM((k, tn), BF16)],
        compiler_params=_params(2),
        name="proj",
    )(a, w, res)


def _proj_norm_kernel(a_ref, w_ref, res_ref, g_ref, o_ref, f_ref, wb_ref):
    @pl.when(pl.program_id(1) == 0)
    def _():
        _cast_weight(w_ref, wb_ref)
    a = jnp.concatenate([a_ref[hh] for hh in range(a_ref.shape[0])], axis=1)
    out = res_ref[...] + jnp.dot(a, wb_ref[...], preferred_element_type=F32)
    o_ref[...] = out
    f_ref[...] = _rms(out, g_ref[...]).astype(f_ref.dtype)


def _proj_norm(a, w, layer, res, g3, g_layer, tm=256):
    tiles_per_b = SEQ // tm
    k = a.shape[1] * a.shape[3]
    row = pl.BlockSpec((tm, D_MODEL), lambda j, i: (i, 0))
    return pl.pallas_call(
        _proj_norm_kernel,
        grid=(1, N_TOK // tm),
        in_specs=[pl.BlockSpec((None, a.shape[1], tm, a.shape[3]),
                               lambda j, i: (i // tiles_per_b, 0, i % tiles_per_b, 0)),
                  _w_spec(layer, k, D_MODEL, 0),
                  row,
                  pl.BlockSpec((None, 1, D_MODEL), lambda j, i: (g_layer, 0, 0))],
        out_specs=[row, row],
        out_shape=[jax.ShapeDtypeStruct((N_TOK, D_MODEL), F32),
                   jax.ShapeDtypeStruct((N_TOK, D_MODEL), BF16)],
        scratch_shapes=[pltpu.VMEM((k, D_MODEL), BF16)],
        compiler_params=_params(2),
        name="proj_norm",
    )(a, w, res, g3)


def _attn_kernel(slopes_ref, *refs):
    qkv_refs = refs[:9]
    o_ref, tab_ref, oscr_ref, lscr_ref = refs[9:]
    h = pl.program_id(1)
    ii = lax.broadcasted_iota(I32, (ATT_BLK, 2 * ATT_BLK), 0)
    jj = lax.broadcasted_iota(I32, (ATT_BLK, 2 * ATT_BLK), 1)
    delta = ii + ATT_BLK - jj
    scale = HEAD_DIM ** -0.5
    nt = (((1,), (1,)), ((), ()))
    n_blocks = SEQ // ATT_BLK
    ones = jnp.ones((2 * ATT_BLK, HEAD_DIM), BF16)
    for g, (win, dil) in enumerate(DIL_CONFIGS):
        q_ref, k_ref, v_ref = qkv_refs[3 * g:3 * g + 3]
        nb = n_blocks // dil
        tab_ref[...] = jnp.where((delta >= 0) & (delta <= ATT_BLK),
                                 -slopes_ref[g, h] * (dil * delta).astype(F32), NEG_INF)
        s, m, p, done = {}, {}, {}, []

        def keys(c, nb=nb):
            first = c % nb == 0
            return (slice((c - (not first)) * ATT_BLK, (c + 1) * ATT_BLK),
                    slice(ATT_BLK if first else 0, 2 * ATT_BLK))

        def scores(c):
            rows, cols = keys(c)
            q = q_ref[c * ATT_BLK:(c + 1) * ATT_BLK, :]
            s[c] = lax.dot_general(q, k_ref[rows, :], nt,
                                   preferred_element_type=F32) * scale + tab_ref[:, cols]

        def softmax(c):
            m[c] = jnp.max(s[c], axis=-1, keepdims=True)
            p[c] = jnp.exp(s.pop(c) - m[c]).astype(BF16)

        def values(c):
            rows, _ = keys(c)
            vv = jnp.concatenate([v_ref[rows, :], ones[:rows.stop - rows.start]], axis=1)
            od = jnp.dot(p.pop(c), vv, preferred_element_type=F32)
            den = od[:, HEAD_DIM:]
            done.append((c, od[:, :HEAD_DIM] / den, m.pop(c) + jnp.log(den)))

        for step in range(n_blocks + 2 * ATT_SKEW):
            if step < n_blocks:
                scores(step)
            if 0 <= step - ATT_SKEW < n_blocks:
                softmax(step - ATT_SKEW)
            if 0 <= step - 2 * ATT_SKEW < n_blocks:
                values(step - 2 * ATT_SKEW)
        for c, o, lse in done:
            start = (c % nb) * (ATT_BLK * dil) + c // nb
            dst = pl.ds(start, ATT_BLK) if dil == 1 else pl.ds(start, ATT_BLK, stride=dil)
            oscr_ref[g, dst, :] = o
            lscr_ref[g, dst, :] = jnp.broadcast_to(lse, (ATT_BLK, HEAD_DIM))

    rows = 256

    def merge(t, carry):
        sl = pl.ds(pl.multiple_of(t * rows, rows), rows)
        l0 = lscr_ref[0, sl, :]
        l1 = lscr_ref[1, sl, :]
        l2 = lscr_ref[2, sl, :]
        m = jnp.maximum(jnp.maximum(l0, l1), l2)
        e0 = jnp.exp(l0 - m)
        e1 = jnp.exp(l1 - m)
        e2 = jnp.exp(l2 - m)
        den = e0 + e1 + e2
        o = (e0 * oscr_ref[0, sl, :] + e1 * oscr_ref[1, sl, :] + e2 * oscr_ref[2, sl, :]) / den
        o_ref[sl, :] = o.astype(o_ref.dtype)
        return carry

    lax.fori_loop(0, SEQ // rows, merge, 0)


def _attention(qkv_groups, slopes):
    in_specs = [pl.BlockSpec(memory_space=pltpu.SMEM)]
    args = [slopes]
    for qkv in qkv_groups:
        for part in range(3):
            in_specs.append(pl.BlockSpec(
                (None, None, SEQ, HEAD_DIM),
                lambda b, h, part=part: (b, part * N_HEADS + h, 0, 0)))
            args.append(qkv)
    return pl.pallas_call(
        _attn_kernel,
        grid=(BATCH, N_HEADS),
        in_specs=in_specs,
        out_specs=pl.BlockSpec((None, None, SEQ, HEAD_DIM), lambda b, h: (b, h, 0, 0)),
        out_shape=jax.ShapeDtypeStruct((BATCH, N_HEADS, SEQ, HEAD_DIM), BF16),
        scratch_shapes=[pltpu.VMEM((ATT_BLK, 2 * ATT_BLK), F32),
                        pltpu.VMEM((N_DIL, SEQ, HEAD_DIM), F32),
                        pltpu.VMEM((N_DIL, SEQ, HEAD_DIM), F32)],
        compiler_params=_params(2),
        name="dilated_attn",
    )(*args)


def _gm_spatial_kernel(z_ref, vg_ref, ws_ref, bt_ref, y_ref, wsb_ref):
    @pl.when(pl.program_id(0) == 0)
    def _():
        ii = lax.broadcasted_iota(I32, (GM_CHUNK, GM_CHUNK), 0)
        jj = lax.broadcasted_iota(I32, (GM_CHUNK, GM_CHUNK), 1)
        for g in range(GM_GROUPS):
            wsb_ref[g] = jnp.where(ii >= jj, ws_ref[g], 0.0).astype(BF16)

    for c in range(z_ref.shape[0] // GM_CHUNK):
        rows = slice(c * GM_CHUNK, (c + 1) * GM_CHUNK)
        vn = _rms(z_ref[rows, GM_WIDTH:].astype(F32), vg_ref[...]).astype(BF16)
        for g in range(GM_GROUPS):
            cols = slice(g * GM_GROUP_DIM, (g + 1) * GM_GROUP_DIM)
            s = jnp.dot(wsb_ref[g], vn[:, cols], preferred_element_type=F32) + bt_ref[:, g:g + 1]
            y_ref[rows, cols] = (z_ref[rows, cols].astype(F32) * s).astype(y_ref.dtype)


def _gm_spatial(z, vg3, ws, bt, layer, tm=256):
    return pl.pallas_call(
        _gm_spatial_kernel,
        grid=(N_TOK // tm,),
        in_specs=[pl.BlockSpec((tm, 2 * GM_WIDTH), lambda i: (i, 0)),
                  pl.BlockSpec((None, 1, GM_WIDTH), lambda i: (layer, 0, 0)),
                  pl.BlockSpec((None, GM_GROUPS, GM_CHUNK, GM_CHUNK), lambda i: (layer, 0, 0, 0)),
                  pl.BlockSpec((None, GM_CHUNK, GM_GROUPS), lambda i: (layer, 0, 0))],
        out_specs=pl.BlockSpec((tm, GM_WIDTH), lambda i: (i, 0)),
        out_shape=jax.ShapeDtypeStruct((N_TOK, GM_WIDTH), BF16),
        scratch_shapes=[pltpu.VMEM((GM_GROUPS, GM_CHUNK, GM_CHUNK), BF16)],
        compiler_params=_params(1),
        name="gm_spatial",
    )(z, vg3, ws, bt)


def _router_kernel(h_ref, g_ref, rwt_ref, idx_ref, gate_ref, rank_ref, cnt_ref, run_ref):
    @pl.when(pl.program_id(0) == 0)
    def _():
        run_ref[...] = jnp.zeros_like(run_ref)

    f = _rms(h_ref[...], g_ref[...])
    logits = lax.dot_general(rwt_ref[...], f, (((1,), (1,)), ((), ())),
                             precision=lax.Precision.HIGHEST, preferred_element_type=F32)
    tm = logits.shape[1]
    eid = lax.broadcasted_iota(I32, logits.shape, 0)
    m1 = jnp.max(logits, axis=0, keepdims=True)
    i1 = jnp.min(jnp.where(logits == m1, eid, N_EXPERTS), axis=0, keepdims=True)
    rest = jnp.where(eid == i1, -jnp.inf, logits)
    m2 = jnp.max(rest, axis=0, keepdims=True)
    i2 = jnp.min(jnp.where(rest == m2, eid, N_EXPERTS), axis=0, keepdims=True)
    e2 = jnp.exp(m2 - m1)
    den = 1.0 + e2
    idx_ref[0:1, :] = i1
    idx_ref[1:2, :] = i2
    gate_ref[0:1, :] = 1.0 / den
    gate_ref[1:2, :] = e2 / den

    sel1 = eid == i1
    sel2 = eid == i2
    onehot = jnp.where(sel1, 1.0, jnp.where(sel2, 1.0, 0.0))
    earlier = (lax.broadcasted_iota(I32, (tm, tm), 0)
               < lax.broadcasted_iota(I32, (tm, tm), 1))
    before = jnp.dot(onehot.astype(BF16), jnp.where(earlier, 1.0, 0.0).astype(BF16),
                     preferred_element_type=F32) + run_ref[:, 0:1]
    rank_ref[0:1, :] = jnp.sum(jnp.where(sel1, before, 0.0), axis=0, keepdims=True).astype(I32)
    rank_ref[1:2, :] = jnp.sum(jnp.where(sel2, before, 0.0), axis=0, keepdims=True).astype(I32)
    run_ref[...] = run_ref[...] + jnp.sum(onehot, axis=1, keepdims=True)
    cnt_ref[...] = run_ref[...]


def _router(h, g3, glayer, rwt, mlayer, tm=512):
    pair = pl.BlockSpec((TOP_K, tm), lambda i: (0, i))
    return pl.pallas_call(
        _router_kernel,
        grid=(N_TOK // tm,),
        in_specs=[pl.BlockSpec((tm, D_MODEL), lambda i: (i, 0)),
                  pl.BlockSpec((None, 1, D_MODEL), lambda i: (glayer, 0, 0)),
                  pl.BlockSpec((None, N_EXPERTS, D_MODEL), lambda i: (mlayer, 0, 0))],
        out_specs=[pair, pair, pair, pl.BlockSpec((N_EXPERTS, LANES), lambda i: (0, 0))],
        out_shape=[jax.ShapeDtypeStruct((TOP_K, N_TOK), I32),
                   jax.ShapeDtypeStruct((TOP_K, N_TOK), F32),
                   jax.ShapeDtypeStruct((TOP_K, N_TOK), I32),
                   jax.ShapeDtypeStruct((N_EXPERTS, LANES), F32)],
        scratch_shapes=[pltpu.VMEM((N_EXPERTS, LANES), F32)],
        compiler_params=_params(1),
        name="router",
    )(h, g3, rwt)


def _dispatch_plan(idx, rank, cnt):
    counts = cnt[:, 0].astype(I32)
    tiles_e = (counts + FFN_TILE - 1) // FFN_TILE
    items_e = (tiles_e + FFN_TILES_PER_CHUNK - 1) // FFN_TILES_PER_CHUNK
    items_end = jnp.cumsum(items_e)
    items_start = items_end - items_e
    n_items = items_end[-1]
    first_row = (jnp.cumsum(tiles_e) - tiles_e) * FFN_TILE
    used_rows = jnp.sum(tiles_e) * FFN_TILE
    pos = rank
    for e in range(N_EXPERTS):
        pos = pos + jnp.where(idx == e, first_row[e], 0)
    it = jnp.arange(MOE_MAX_ITEMS, dtype=I32)
    it_c = jnp.clip(it, 0, jnp.maximum(n_items - 1, 0))
    item_expert = jnp.minimum(jnp.searchsorted(items_end, it_c, side="right"),
                              N_EXPERTS - 1).astype(I32)
    local = it_c - items_start[item_expert]
    item_tiles = jnp.clip(tiles_e[item_expert] - local * FFN_TILES_PER_CHUNK,
                          0, FFN_TILES_PER_CHUNK)
    item_tiles = jnp.where(it < n_items, item_tiles, 0).astype(I32)
    item_row0 = (first_row[item_expert] + local * FFN_CHUNK).astype(I32)
    tok = jnp.broadcast_to(jnp.arange(N_TOK, dtype=I32), (TOP_K, N_TOK))
    src = jnp.zeros((MOE_ROWS,), I32).at[pos.reshape(-1)].set(tok.reshape(-1))
    tile_valid = (jnp.arange(MOE_ROWS // DISPATCH_TILE, dtype=I32) * DISPATCH_TILE
                  < used_rows).astype(I32)
    meta = jnp.stack([n_items, used_rows]).astype(I32)
    return pos, src, tile_valid, item_expert, item_tiles, item_row0, meta


def _row_copy(src_hbm, row, dst_ref, slot, r, sem_ref):
    return pltpu.make_async_copy(src_hbm.at[pl.ds(row, 1), :],
                                 dst_ref.at[slot, pl.ds(r, 1), :], sem_ref.at[slot])


def _dispatch_kernel(src_ref, valid_ref, h_hbm, g_ref, o_ref, buf_ref, sem_ref):
    p = pl.program_id(0)
    last = pl.num_programs(0) - 1

    def issue(tile, slot):
        def body(r, carry):
            _row_copy(h_hbm, src_ref[tile * DISPATCH_TILE + r], buf_ref, slot, r, sem_ref).start()
            return carry
        lax.fori_loop(0, DISPATCH_TILE, body, 0, unroll=DMA_ISSUE_UNROLL)

    @pl.when((p == 0) & (valid_ref[0] > 0))
    def _():
        issue(0, 0)

    nxt = jnp.minimum(p + 1, last)

    @pl.when((p < last) & (valid_ref[nxt] > 0))
    def _():
        issue(nxt, nxt % 2)

    slot = p % 2

    @pl.when(valid_ref[p] > 0)
    def _():
        for r in range(DISPATCH_TILE):
            _row_copy(h_hbm, 0, buf_ref, slot, r, sem_ref).wait()
        o_ref[...] = _rms(buf_ref[slot], g_ref[...]).astype(o_ref.dtype)

    @pl.when(valid_ref[p] == 0)
    def _():
        o_ref[...] = jnp.zeros_like(o_ref)


def _dispatch(h, src, tile_valid, g3, layer):
    return pl.pallas_call(
        _dispatch_kernel,
        grid_spec=pltpu.PrefetchScalarGridSpec(
            num_scalar_prefetch=2,
            grid=(MOE_ROWS // DISPATCH_TILE,),
            in_specs=[pl.BlockSpec(memory_space=pl.ANY),
                      pl.BlockSpec((None, 1, D_MODEL), lambda p, s, v: (layer, 0, 0))],
            out_specs=pl.BlockSpec((DISPATCH_TILE, D_MODEL), lambda p, s, v: (p, 0)),
            scratch_shapes=[pltpu.VMEM((2, DISPATCH_TILE, D_MODEL), F32),
                            pltpu.SemaphoreType.DMA((2,))]),
        out_shape=jax.ShapeDtypeStruct((MOE_ROWS, D_MODEL), BF16),
        compiler_params=_params(1),
        name="moe_dispatch",
    )(src, tile_valid, h, g3)


def _ffn_kernel(exp_ref, tiles_ref, row0_ref, meta_ref, x_hbm, wg_ref, wu_ref, wd_ref, *rest,
                fused):
    if fused:
        (res_hbm, g_ref, y_hbm, a_hbm, xbuf_ref, wgu_ref, wdb_ref, acc_ref, zero_ref, xsem_ref,
         sem_ref, zsem_ref, astage_ref, asem_ref, rsem_ref) = rest
    else:
        y_hbm, xbuf_ref, wgu_ref, wdb_ref, acc_ref, zero_ref, xsem_ref, sem_ref, zsem_ref = rest
    it = pl.program_id(0)
    j = pl.program_id(1)
    last_j = pl.num_programs(1) - 1
    n_tiles = tiles_ref[it]
    n_items = meta_ref[0]
    used_rows = meta_ref[1]
    tf = wg_ref.shape[1]
    chunk = xbuf_ref.shape[1]
    big = FFN_TRIP_TILES[0]
    row0 = row0_ref[it]
    x_ref = xbuf_ref.at[it % 2]

    def x_copy(item):
        return pltpu.make_async_copy(
            x_hbm.at[pl.ds(pl.multiple_of(row0_ref[item], FFN_TILE), chunk), :],
            xbuf_ref.at[item % 2], xsem_ref.at[item % 2])

    def out_copy(i):
        r = pl.multiple_of(i * FFN_TILE, FFN_TILE)
        return pltpu.make_async_copy(
            acc_ref.at[pl.ds(r, FFN_TILE), :],
            y_hbm.at[pl.ds(pl.multiple_of(row0 + r, FFN_TILE), FFN_TILE), :], sem_ref.at[i])

    def res_copy(i):
        r = pl.multiple_of(i * FFN_TILE, FFN_TILE)
        return pltpu.make_async_copy(
            res_hbm.at[pl.ds(pl.multiple_of(row0 + r, FFN_TILE), FFN_TILE), :],
            acc_ref.at[pl.ds(r, FFN_TILE), :], rsem_ref.at[0])

    def a_copy(p):
        r = pl.multiple_of(row0 + p * (big * FFN_TILE), FFN_TILE)
        return pltpu.make_async_copy(astage_ref.at[p % 2],
                                     a_hbm.at[pl.ds(r, big * FFN_TILE), :], asem_ref.at[p % 2])

    def ffn_rows(start, n_rows, first):
        rows = pl.ds(pl.multiple_of(start, FFN_TILE), n_rows)
        gu = jnp.dot(x_ref[rows, :], wgu_ref[...], preferred_element_type=F32)
        gate = gu[:, :tf]
        act = ((gate * jax.nn.sigmoid(gate)) * gu[:, tf:]).astype(BF16)
        d = jnp.dot(act, wdb_ref[...], preferred_element_type=F32)
        if first:
            acc_ref[rows, :] = d
        else:
            acc_ref[rows, :] += d

    def run(first, last):
        def trip(tile0, tiles):
            ffn_rows(tile0 * FFN_TILE, tiles * FFN_TILE, first)
            if last:
                for i in range(tiles):
                    out_copy(tile0 + i).start()

        def big_trip(p, carry):
            trip(pl.multiple_of(p * big, big), big)
            if last and fused:
                @pl.when(p >= 2)
                def _():
                    a_copy(p - 2).wait()
                rows = pl.ds(pl.multiple_of(p * (big * FFN_TILE), big * FFN_TILE), big * FFN_TILE)
                astage_ref[p % 2] = _rms(acc_ref[rows, :], g_ref[...]).astype(BF16)
                a_copy(p).start()
            return carry
        n_big = n_tiles // big
        if fused:
            lax.fori_loop(0, n_big, big_trip, 0)
        else:
            merge = (n_tiles - n_big * big == 1) & (n_big > 0)
            n_loop = n_big - jnp.where(merge, 1, 0)
            lax.fori_loop(0, n_loop, big_trip, 0)

            @pl.when(merge)
            def _():
                trip(pl.multiple_of(n_loop * big, big), big + 1)
            done = jnp.where(merge, n_tiles, n_big * big)
            for tiles in FFN_TRIP_TILES[1:]:
                take = ((n_tiles - done) // tiles) > 0

                @pl.when(take)
                def _(done=done, tiles=tiles):
                    trip(pl.multiple_of(done, tiles), tiles)
                done = done + jnp.where(take, tiles, 0)

        if last:
            def drain(i, carry):
                out_copy(i).wait()
                return carry
            lax.fori_loop(0, n_tiles, drain, 0)
            if fused:
                def drain_a(p, carry):
                    a_copy(p).wait()
                    return carry
                lax.fori_loop(jnp.maximum(n_big - 2, 0), n_big, drain_a, 0)

    zero_rows = zero_ref.shape[0]

    def zero_copy(k):
        row = pl.multiple_of(used_rows + k * zero_rows, zero_rows)
        return pltpu.make_async_copy(zero_ref, y_hbm.at[pl.ds(row, zero_rows), :], zsem_ref.at[0])

    @pl.when((it == n_items - 1) & (j == last_j))
    def _():
        zero_ref[...] = jnp.zeros_like(zero_ref)
        pieces = (y_hbm.shape[0] - used_rows) // zero_rows

        def start(k, carry):
            zero_copy(k).start()
            return carry

        def wait(k, carry):
            zero_copy(k).wait()
            return carry
        lax.fori_loop(0, pieces, start, 0)
        lax.fori_loop(0, pieces, wait, 0)

    @pl.when(n_tiles > 0)
    def _():
        wgu_ref[:, :tf] = wg_ref[...].astype(BF16)
        wgu_ref[:, tf:] = wu_ref[...].astype(BF16)
        wdb_ref[...] = wd_ref[...].astype(BF16)

        @pl.when(j == 0)
        def _():
            @pl.when(it == 0)
            def _():
                x_copy(0).start()
            x_copy(it).wait()

            @pl.when(it + 1 < n_items)
            def _():
                x_copy(it + 1).start()
            if fused:
                def start(i, carry):
                    res_copy(i).start()
                    return carry

                def wait(i, carry):
                    res_copy(i).wait()
                    return carry
                lax.fori_loop(0, n_tiles, start, 0)
                lax.fori_loop(0, n_tiles, wait, 0)
            run(not fused, False)

        @pl.when((j > 0) & (j < last_j))
        def _():
            run(False, False)

        @pl.when(j == last_j)
        def _():
            run(False, True)


def _grouped_ffn(xs, wg, wu, wd, layer, item_expert, item_tiles, item_row0, meta, chunk,
                 residual=None):
    n_items = item_expert.shape[0]
    out_rows = xs.shape[0]
    nj = D_FF // FFN_TF
    fused = residual is not None

    def col(it, j, e, t, r, m):
        return jnp.where(t[it] > 0, j, nj - 1)

    in_specs = [
        pl.BlockSpec(memory_space=pl.ANY),
        pl.BlockSpec((None, None, D_MODEL, FFN_TF),
                     lambda it, j, e, t, r, m: (layer, e[it], 0, col(it, j, e, t, r, m))),
        pl.BlockSpec((None, None, D_MODEL, FFN_TF),
                     lambda it, j, e, t, r, m: (layer, e[it], 0, col(it, j, e, t, r, m))),
        pl.BlockSpec((None, None, FFN_TF, D_MODEL),
                     lambda it, j, e, t, r, m: (layer, e[it], col(it, j, e, t, r, m), 0)),
    ]
    args = [item_expert, item_tiles, item_row0, meta, xs, wg, wu, wd]
    out_specs = pl.BlockSpec(memory_space=pl.ANY)
    out_shape = jax.ShapeDtypeStruct((out_rows, D_MODEL), F32)
    scratch = [pltpu.VMEM((2, chunk, D_MODEL), BF16),
               pltpu.VMEM((D_MODEL, 2 * FFN_TF), BF16),
               pltpu.VMEM((FFN_TF, D_MODEL), BF16),
               pltpu.VMEM((chunk, D_MODEL), F32),
               pltpu.VMEM((FFN_ZERO_ROWS, D_MODEL), F32),
               pltpu.SemaphoreType.DMA((2,)),
               pltpu.SemaphoreType.DMA((chunk // FFN_TILE,)),
               pltpu.SemaphoreType.DMA((1,))]
    if fused:
        res, g3, g_layer = residual
        in_specs += [pl.BlockSpec(memory_space=pl.ANY),
                     pl.BlockSpec((None, 1, D_MODEL),
                                  lambda it, j, e, t, r, m: (g_layer, 0, 0))]
        args += [res, g3]
        out_specs = [out_specs, pl.BlockSpec(memory_space=pl.ANY)]
        out_shape = [out_shape, jax.ShapeDtypeStruct((out_rows, D_MODEL), BF16)]
        scratch += [pltpu.VMEM((2, FFN_TRIP_TILES[0] * FFN_TILE, D_MODEL), BF16),
                    pltpu.SemaphoreType.DMA((2,)),
                    pltpu.SemaphoreType.DMA((1,))]
    return pl.pallas_call(
        functools.partial(_ffn_kernel, fused=fused),
        grid_spec=pltpu.PrefetchScalarGridSpec(
            num_scalar_prefetch=4,
            grid=(n_items, nj),
            in_specs=in_specs,
            out_specs=out_specs,
            scratch_shapes=scratch),
        out_shape=out_shape,
        compiler_params=_params(2),
        name="grouped_ffn",
    )(*args)


def _dense_ffn(f, wg, wu, wd, layer, h, g3, g_layer):
    n_items = N_TOK // DENSE_CHUNK
    assert DENSE_CHUNK % (FFN_TRIP_TILES[0] * FFN_TILE) == 0
    return _grouped_ffn(
        f, wg[:, None], wu[:, None], wd[:, None], layer,
        jnp.zeros((n_items,), I32), jnp.full((n_items,), DENSE_CHUNK // FFN_TILE, I32),
        jnp.arange(n_items, dtype=I32) * DENSE_CHUNK, jnp.array([n_items, N_TOK], I32),
        DENSE_CHUNK, residual=(h, g3, g_layer))


def _combine_kernel(pos_ref, h_ref, gate_ref, g_ref, y_hbm, *rest, final):
    if final:
        o_ref, buf_ref, sem_ref = rest
    else:
        hn_ref, a_ref, buf_ref, sem_ref = rest
    t = pl.program_id(0)
    last = pl.num_programs(0) - 1
    tm = h_ref.shape[0]

    def issue(tile, slot):
        def body(r, carry):
            for k in range(TOP_K):
                _row_copy(y_hbm, pos_ref[k * N_TOK + tile * tm + r],
                          buf_ref, slot, k * tm + r, sem_ref).start()
            return carry
        lax.fori_loop(0, tm, body, 0, unroll=DMA_ISSUE_UNROLL)

    @pl.when(t == 0)
    def _():
        issue(0, 0)

    @pl.when(t < last)
    def _():
        issue(t + 1, (t + 1) % 2)

    slot = t % 2
    for r in range(TOP_K * tm):
        _row_copy(y_hbm, 0, buf_ref, slot, r, sem_ref).wait()
    hn = (h_ref[...] + gate_ref[:, 0:1] * buf_ref[slot, pl.ds(0, tm), :]
          + gate_ref[:, 1:2] * buf_ref[slot, pl.ds(tm, tm), :])
    if final:
        o_ref[...] = _rms(hn, g_ref[...])
    else:
        hn_ref[...] = hn
        a_ref[...] = _rms(hn, g_ref[...]).astype(a_ref.dtype)


def _combine(h, y, pos, gates_t, g3, layer, final, tm=256):
    row = pl.BlockSpec((tm, D_MODEL), lambda t, p: (t, 0))
    if final:
        out_specs = row
        out_shape = jax.ShapeDtypeStruct((N_TOK, D_MODEL), F32)
    else:
        out_specs = [row, row]
        out_shape = [jax.ShapeDtypeStruct((N_TOK, D_MODEL), F32),
                     jax.ShapeDtypeStruct((N_TOK, D_MODEL), BF16)]
    return pl.pallas_call(
        functools.partial(_combine_kernel, final=final),
        grid_spec=pltpu.PrefetchScalarGridSpec(
            num_scalar_prefetch=1,
            grid=(N_TOK // tm,),
            in_specs=[row,
                      pl.BlockSpec((tm, TOP_K), lambda t, p: (t, 0)),
                      pl.BlockSpec((None, 1, D_MODEL), lambda t, p: (layer, 0, 0)),
                      pl.BlockSpec(memory_space=pl.ANY)],
            out_specs=out_specs,
            scratch_shapes=[pltpu.VMEM((2, TOP_K * tm, D_MODEL), F32),
                            pltpu.SemaphoreType.DMA((2,))]),
        out_shape=out_shape,
        compiler_params=_params(1),
        name="moe_combine",
    )(pos.reshape(-1), h, gates_t, g3, y)


def _alibi_slopes():
    n = N_DIL * N_HEADS
    s = jnp.exp2(-8.0 * jnp.arange(1, n + 1, dtype=F32) / n)
    return s.reshape(N_HEADS, N_DIL).T


def kernel(x, mix_norm_g, ffn_norm_g, attn_w_in, attn_w_out, gm_w_in, gm_v_norm_g, gm_w_s,
           gm_b_s, gm_w_out, dense_w_gate, dense_w_up, dense_w_down, router_w, moe_w_gate,
           moe_w_up, moe_w_down, final_norm_g):
    h = x.reshape(N_TOK, D_MODEL)
    mix_g = mix_norm_g.reshape(DEPTH, 1, D_MODEL)
    ffn_g = ffn_norm_g.reshape(DEPTH, 1, D_MODEL)
    final_g = final_norm_g.reshape(1, 1, D_MODEL)
    gm_vg = gm_v_norm_g.reshape(-1, 1, GM_WIDTH)
    gm_bt = jnp.swapaxes(gm_b_s, 1, 2)
    router_wt = jnp.swapaxes(router_w, 1, 2)
    slopes = _alibi_slopes()

    a = _rmsnorm(h, mix_g, 0, BF16)
    for i in range(DEPTH):
        j = i // 2
        if i % 2 == 0:
            groups = [_mm_qkv(a, attn_w_in, j, dil, g * (QKV_GROUP_COLS // 1024))
                      for g, (win, dil) in enumerate(DIL_CONFIGS)]
            h, f = _proj_norm(_attention(groups, slopes), attn_w_out, j, h, ffn_g, i)
            h, a = _dense_ffn(f, dense_w_gate, dense_w_up, dense_w_down, j, h, mix_g, i + 1)
        else:
            z = _mm_act(a, gm_w_in, j, 2 * GM_WIDTH, "gelu")
            y = _gm_spatial(z, gm_vg, gm_w_s, gm_bt, j)
            h = _proj(y, gm_w_out, j, h)
            idx, gates, rank, cnt = _router(h, ffn_g, i, router_wt, j)
            pos, src, tile_valid, item_expert, item_tiles, item_row0, meta = _dispatch_plan(
                idx, rank, cnt)
            xs = _dispatch(h, src, tile_valid, ffn_g, i)
            def moe_ffn(n, xs=xs, items=(item_expert, item_tiles, item_row0), meta=meta, j=j):
                return _grouped_ffn(xs, moe_w_gate, moe_w_up, moe_w_down, j,
                                    *(v[:n] for v in items), meta, FFN_CHUNK)
            y = lax.cond(meta[0] <= MOE_COMMON_ITEMS,
                         functools.partial(moe_ffn, MOE_COMMON_ITEMS),
                         functools.partial(moe_ffn, MOE_MAX_ITEMS))
            if i == DEPTH - 1:
                return _combine(h, y, pos, gates.T, final_g, 0, True).reshape(BATCH, SEQ, D_MODEL)
            h, a = _combine(h, y, pos, gates.T, mix_g, i + 1, False)
```

```python
import functools

import jax
import jax.numpy as jnp
from jax import lax
from jax.experimental import pallas as pl
from jax.experimental.pallas import tpu as pltpu

D_MODEL = 2048
BATCH = 2
SEQ = 4096
DEPTH = 4
N_TOK = BATCH * SEQ
HEAD_DIM = 128
N_HEADS = D_MODEL // HEAD_DIM
DIL_CONFIGS = ((128, 1), (512, 4), (2048, 16))
N_DIL = len(DIL_CONFIGS)
ATT_BLK = 128
ATT_SKEW = 3
QKV_GROUP_COLS = 3 * N_HEADS * HEAD_DIM
GM_CHUNK = 128
GM_WIDTH = D_MODEL
GM_GROUP_DIM = 128
GM_GROUPS = GM_WIDTH // GM_GROUP_DIM
D_FF = 7 * D_MODEL // 2
N_EXPERTS = 8
TOP_K = 2
EPS = 1e-6
NEG_INF = -1e30

LANES = 128
VMEM_LIMIT = 56 * 1024 * 1024
CAST_ROWS = 256

FFN_TILE = 128
FFN_TRIP_TILES = (4, 2, 1)
DISPATCH_TILE = 256
FFN_CHUNK = 2304
FFN_TILES_PER_CHUNK = FFN_CHUNK // FFN_TILE
DENSE_CHUNK = 2048
FFN_TF = 256
DMA_ISSUE_UNROLL = 8
FFN_ZERO_ROWS = 64
MOE_MAX_ITEMS = -(-TOP_K * N_TOK // FFN_CHUNK) + N_EXPERTS
MOE_ROWS = TOP_K * N_TOK + N_EXPERTS * FFN_TILE + FFN_CHUNK
assert MOE_ROWS % DISPATCH_TILE == 0

F32 = jnp.float32
BF16 = jnp.bfloat16
I32 = jnp.int32


def _params(n_axes):
    return pltpu.CompilerParams(
        dimension_semantics=("arbitrary",) * n_axes, vmem_limit_bytes=VMEM_LIMIT)


def _cast_weight(w_ref, wb_ref):
    def body(c, carry):
        r = pl.multiple_of(c * CAST_ROWS, CAST_ROWS)
        wb_ref[pl.ds(r, CAST_ROWS), :] = w_ref[pl.ds(r, CAST_ROWS), :].astype(BF16)
        return carry
    lax.fori_loop(0, w_ref.shape[0] // CAST_ROWS, body, 0)


def _gelu_tanh(x):
    return 0.5 * x * (1.0 + jnp.tanh(0.7978845608028654 * (x + 0.044715 * (x * x * x))))


def _rms(x, g):
    ms = jnp.mean(x * x, axis=-1, keepdims=True)
    return (x * lax.rsqrt(ms + EPS)) * g


def _rms_kernel(x_ref, g_ref, o_ref):
    o_ref[...] = _rms(x_ref[...], g_ref[...]).astype(o_ref.dtype)


def _rmsnorm(h, g3, layer, out_dtype, tm=512):
    return pl.pallas_call(
        _rms_kernel,
        grid=(N_TOK // tm,),
        in_specs=[pl.BlockSpec((tm, D_MODEL), lambda i: (i, 0)),
                  pl.BlockSpec((None, 1, D_MODEL), lambda i: (layer, 0, 0))],
        out_specs=pl.BlockSpec((tm, D_MODEL), lambda i: (i, 0)),
        out_shape=jax.ShapeDtypeStruct((N_TOK, D_MODEL), out_dtype),
        compiler_params=_params(1),
        name="rmsnorm",
    )(h, g3)


def _w_spec(layer, k, tn, col_off):
    return pl.BlockSpec((None, k, tn), lambda j, i: (layer, 0, col_off + j))


def _mm_act_kernel(a_ref, w_ref, o_ref, wb_ref, *, act):
    @pl.when(pl.program_id(1) == 0)
    def _():
        _cast_weight(w_ref, wb_ref)
    acc = jnp.dot(a_ref[...], wb_ref[...], preferred_element_type=F32)
    if act == "gelu":
        acc = _gelu_tanh(acc)
    o_ref[...] = acc.astype(o_ref.dtype)


def _mm_act(a, w, layer, n_cols, act, tm=1024, tn=1024, col_off=0):
    k = a.shape[1]
    return pl.pallas_call(
        functools.partial(_mm_act_kernel, act=act),
        grid=(n_cols // tn, N_TOK // tm),
        in_specs=[pl.BlockSpec((tm, k), lambda j, i: (i, 0)),
                  _w_spec(layer, k, tn, col_off)],
        out_specs=pl.BlockSpec((tm, tn), lambda j, i: (i, j)),
        out_shape=jax.ShapeDtypeStruct((N_TOK, n_cols), BF16),
        scratch_shapes=[pltpu.VMEM((k, tn), BF16)],
        compiler_params=_params(2),
        name="mm_act",
    )(a, w)


def _mm_qkv_kernel(a_ref, w_ref, o_ref, wb_ref, *rest, dil):
    @pl.when(pl.program_id(1) == 0)
    def _():
        _cast_weight(w_ref, wb_ref)
    acc = jnp.dot(a_ref[...], wb_ref[...], preferred_element_type=F32)
    heads = o_ref.shape[0]
    if dil == 1:
        for hh in range(heads):
            o_ref[hh, 0] = acc[:, hh * HEAD_DIM:(hh + 1) * HEAD_DIM].astype(o_ref.dtype)
    else:
        acc_ref, = rest
        rows = acc_ref.shape[1] // dil
        for hh in range(heads):
            acc_ref[hh] = acc[:, hh * HEAD_DIM:(hh + 1) * HEAD_DIM]

        def regroup(hh, carry):
            for r in range(dil):
                o_ref[hh, r] = acc_ref[hh, pl.ds(r, rows, stride=dil), :].astype(o_ref.dtype)
            return carry
        lax.fori_loop(0, heads, regroup, 0, unroll=dil <= 4)


def _mm_qkv(a, w, layer, dil, col_off, tm=1024, tn=1024):
    k = a.shape[1]
    tiles_per_b = SEQ // tm
    heads = tn // HEAD_DIM
    scratch = [pltpu.VMEM((k, tn), BF16)]
    if dil > 1:
        scratch.append(pltpu.VMEM((heads, tm, HEAD_DIM), F32))
    out = pl.pallas_call(
        functools.partial(_mm_qkv_kernel, dil=dil),
        grid=(QKV_GROUP_COLS // tn, N_TOK // tm),
        in_specs=[pl.BlockSpec((tm, k), lambda j, i: (i, 0)),
                  _w_spec(layer, k, tn, col_off)],
        out_specs=pl.BlockSpec((None, heads, dil, tm // dil, HEAD_DIM),
                               lambda j, i: (i // tiles_per_b, j, 0, i % tiles_per_b, 0)),
        out_shape=jax.ShapeDtypeStruct(
            (BATCH, 3 * N_HEADS, dil, SEQ // dil, HEAD_DIM), BF16),
        scratch_shapes=scratch,
        compiler_params=_params(2),
        name="mm_qkv",
    )(a, w)
    return out.reshape(BATCH, 3 * N_HEADS, SEQ, HEAD_DIM)


def _proj_kernel(a_ref, w_ref, res_ref, o_ref, wb_ref):
    @pl.when(pl.program_id(1) == 0)
    def _():
        _cast_weight(w_ref, wb_ref)
    o_ref[...] = res_ref[...] + jnp.dot(a_ref[...], wb_ref[...], preferred_element_type=F32)


def _proj(a, w, layer, res, tm=512, tn=1024):
    k = a.shape[1]
    return pl.pallas_call(
        _proj_kernel,
        grid=(D_MODEL // tn, N_TOK // tm),
        in_specs=[pl.BlockSpec((tm, k), lambda j, i: (i, 0)),
                  _w_spec(layer, k, tn, 0),
                  pl.BlockSpec((tm, tn), lambda j, i: (i, j))],
        out_specs=pl.BlockSpec((tm, tn), lambda j, i: (i, j)),
        out_shape=jax.ShapeDtypeStruct((N_TOK, D_MODEL), F32),
        scratch_shapes=[pltpu.VMEM((k, tn), BF16)],
        compiler_params=_params(2),
        name="proj",
    )(a, w, res)


def _proj_norm_kernel(a_ref, w_ref, res_ref, g_ref, o_ref, f_ref, wb_ref):
    @pl.when(pl.program_id(1) == 0)
    def _():
        _cast_weight(w_ref, wb_ref)
    a = jnp.concatenate([a_ref[hh] for hh in range(a_ref.shape[0])], axis=1)
    out = res_ref[...] + jnp.dot(a, wb_ref[...], preferred_element_type=F32)
    o_ref[...] = out
    f_ref[...] = _rms(out, g_ref[...]).astype(f_ref.dtype)


def _proj_norm(a, w, layer, res, g3, g_layer, tm=256):
    tiles_per_b = SEQ // tm
    k = a.shape[1] * a.shape[3]
    row = pl.BlockSpec((tm, D_MODEL), lambda j, i: (i, 0))
    return pl.pallas_call(
        _proj_norm_kernel,
        grid=(1, N_TOK // tm),
        in_specs=[pl.BlockSpec((None, a.shape[1], tm, a.shape[3]),
                               lambda j, i: (i // tiles_per_b, 0, i % tiles_per_b, 0)),
                  _w_spec(layer, k, D_MODEL, 0),
                  row,
                  pl.BlockSpec((None, 1, D_MODEL), lambda j, i: (g_layer, 0, 0))],
        out_specs=[row, row],
        out_shape=[jax.ShapeDtypeStruct((N_TOK, D_MODEL), F32),
                   jax.ShapeDtypeStruct((N_TOK, D_MODEL), BF16)],
        scratch_shapes=[pltpu.VMEM((k, D_MODEL), BF16)],
        compiler_params=_params(2),
        name="proj_norm",
    )(a, w, res, g3)


def _attn_kernel(slopes_ref, *refs):
    qkv_refs = refs[:9]
    o_ref, tab_ref, oscr_ref, lscr_ref = refs[9:]
    h = pl.program_id(1)
    ii = lax.broadcasted_iota(I32, (ATT_BLK, 2 * ATT_BLK), 0)
    jj = lax.broadcasted_iota(I32, (ATT_BLK, 2 * ATT_BLK), 1)
    delta = ii + ATT_BLK - jj
    scale = HEAD_DIM ** -0.5
    nt = (((1,), (1,)), ((), ()))
    n_blocks = SEQ // ATT_BLK
    ones = jnp.ones((2 * ATT_BLK, HEAD_DIM), BF16)
    for g, (win, dil) in enumerate(DIL_CONFIGS):
        q_ref, k_ref, v_ref = qkv_refs[3 * g:3 * g + 3]
        nb = n_blocks // dil
        tab_ref[...] = jnp.where((delta >= 0) & (delta <= ATT_BLK),
                                 -slopes_ref[g, h] * (dil * delta).astype(F32), NEG_INF)
        s, m, p, done = {}, {}, {}, []

        def keys(c, nb=nb):
            first = c % nb == 0
            return (slice((c - (not first)) * ATT_BLK, (c + 1) * ATT_BLK),
                    slice(ATT_BLK if first else 0, 2 * ATT_BLK))

        def scores(c):
            rows, cols = keys(c)
            q = q_ref[c * ATT_BLK:(c + 1) * ATT_BLK, :]
            s[c] = lax.dot_general(q, k_ref[rows, :], nt,
                                   preferred_element_type=F32) * scale + tab_ref[:, cols]

        def softmax(c):
            m[c] = jnp.max(s[c], axis=-1, keepdims=True)
            p[c] = jnp.exp(s.pop(c) - m[c]).astype(BF16)

        def values(c):
            rows, _ = keys(c)
            vv = jnp.concatenate([v_ref[rows, :], ones[:rows.stop - rows.start]], axis=1)
            od = jnp.dot(p.pop(c), vv, preferred_element_type=F32)
            den = od[:, HEAD_DIM:]
            done.append((c, od[:, :HEAD_DIM] / den, m.pop(c) + jnp.log(den)))

        for step in range(n_blocks + 2 * ATT_SKEW):
            if step < n_blocks:
                scores(step)
            if 0 <= step - ATT_SKEW < n_blocks:
                softmax(step - ATT_SKEW)
            if 0 <= step - 2 * ATT_SKEW < n_blocks:
                values(step - 2 * ATT_SKEW)
        for c, o, lse in done:
            start = (c % nb) * (ATT_BLK * dil) + c // nb
            dst = pl.ds(start, ATT_BLK) if dil == 1 else pl.ds(start, ATT_BLK, stride=dil)
            oscr_ref[g, dst, :] = o
            lscr_ref[g, dst, :] = jnp.broadcast_to(lse, (ATT_BLK, HEAD_DIM))

    rows = 256

    def merge(t, carry):
        sl = pl.ds(pl.multiple_of(t * rows, rows), rows)
        l0 = lscr_ref[0, sl, :]
        l1 = lscr_ref[1, sl, :]
        l2 = lscr_ref[2, sl, :]
        m = jnp.maximum(jnp.maximum(l0, l1), l2)
        e0 = jnp.exp(l0 - m)
        e1 = jnp.exp(l1 - m)
        e2 = jnp.exp(l2 - m)
        den = e0 + e1 + e2
        o = (e0 * oscr_ref[0, sl, :] + e1 * oscr_ref[1, sl, :] + e2 * oscr_ref[2, sl, :]) / den
        o_ref[sl, :] = o.astype(o_ref.dtype)
        return carry

    lax.fori_loop(0, SEQ // rows, merge, 0)


def _attention(qkv_groups, slopes):
    in_specs = [pl.BlockSpec(memory_space=pltpu.SMEM)]
    args = [slopes]
    for qkv in qkv_groups:
        for part in range(3):
            in_specs.append(pl.BlockSpec(
                (None, None, SEQ, HEAD_DIM),
                lambda b, h, part=part: (b, part * N_HEADS + h, 0, 0)))
            args.append(qkv)
    return pl.pallas_call(
        _attn_kernel,
        grid=(BATCH, N_HEADS),
        in_specs=in_specs,
        out_specs=pl.BlockSpec((None, None, SEQ, HEAD_DIM), lambda b, h: (b, h, 0, 0)),
        out_shape=jax.ShapeDtypeStruct((BATCH, N_HEADS, SEQ, HEAD_DIM), BF16),
        scratch_shapes=[pltpu.VMEM((ATT_BLK, 2 * ATT_BLK), F32),
                        pltpu.VMEM((N_DIL, SEQ, HEAD_DIM), F32),
                        pltpu.VMEM((N_DIL, SEQ, HEAD_DIM), F32)],
        compiler_params=_params(2),
        name="dilated_attn",
    )(*args)


def _gm_spatial_kernel(z_ref, vg_ref, ws_ref, bt_ref, y_ref, wsb_ref):
    @pl.when(pl.program_id(0) == 0)
    def _():
        ii = lax.broadcasted_iota(I32, (GM_CHUNK, GM_CHUNK), 0)
        jj = lax.broadcasted_iota(I32, (GM_CHUNK, GM_CHUNK), 1)
        for g in range(GM_GROUPS):
            wsb_ref[g] = jnp.where(ii >= jj, ws_ref[g], 0.0).astype(BF16)

    for c in range(z_ref.shape[0] // GM_CHUNK):
        rows = slice(c * GM_CHUNK, (c + 1) * GM_CHUNK)
        vn = _rms(z_ref[rows, GM_WIDTH:].astype(F32), vg_ref[...]).astype(BF16)
        for g in range(GM_GROUPS):
            cols = slice(g * GM_GROUP_DIM, (g + 1) * GM_GROUP_DIM)
            s = jnp.dot(wsb_ref[g], vn[:, cols], preferred_element_type=F32) + bt_ref[:, g:g + 1]
            y_ref[rows, cols] = (z_ref[rows, cols].astype(F32) * s).astype(y_ref.dtype)


def _gm_spatial(z, vg3, ws, bt, layer, tm=256):
    return pl.pallas_call(
        _gm_spatial_kernel,
        grid=(N_TOK // tm,),
        in_specs=[pl.BlockSpec((tm, 2 * GM_WIDTH), lambda i: (i, 0)),
                  pl.BlockSpec((None, 1, GM_WIDTH), lambda i: (layer, 0, 0)),
                  pl.BlockSpec((None, GM_GROUPS, GM_CHUNK, GM_CHUNK), lambda i: (layer, 0, 0, 0)),
                  pl.BlockSpec((None, GM_CHUNK, GM_GROUPS), lambda i: (layer, 0, 0))],
        out_specs=pl.BlockSpec((tm, GM_WIDTH), lambda i: (i, 0)),
        out_shape=jax.ShapeDtypeStruct((N_TOK, GM_WIDTH), BF16),
        scratch_shapes=[pltpu.VMEM((GM_GROUPS, GM_CHUNK, GM_CHUNK), BF16)],
        compiler_params=_params(1),
        name="gm_spatial",
    )(z, vg3, ws, bt)


def _router_kernel(h_ref, g_ref, rwt_ref, idx_ref, gate_ref, rank_ref, cnt_ref, run_ref):
    @pl.when(pl.program_id(0) == 0)
    def _():
        run_ref[...] = jnp.zeros_like(run_ref)

    f = _rms(h_ref[...], g_ref[...])
    logits = lax.dot_general(rwt_ref[...], f, (((1,), (1,)), ((), ())),
                             precision=lax.Precision.HIGHEST, preferred_element_type=F32)
    tm = logits.shape[1]
    eid = lax.broadcasted_iota(I32, logits.shape, 0)
    m1 = jnp.max(logits, axis=0, keepdims=True)
    i1 = jnp.min(jnp.where(logits == m1, eid, N_EXPERTS), axis=0, keepdims=True)
    rest = jnp.where(eid == i1, -jnp.inf, logits)
    m2 = jnp.max(rest, axis=0, keepdims=True)
    i2 = jnp.min(jnp.where(rest == m2, eid, N_EXPERTS), axis=0, keepdims=True)
    e2 = jnp.exp(m2 - m1)
    den = 1.0 + e2
    idx_ref[0:1, :] = i1
    idx_ref[1:2, :] = i2
    gate_ref[0:1, :] = 1.0 / den
    gate_ref[1:2, :] = e2 / den

    sel1 = eid == i1
    sel2 = eid == i2
    onehot = jnp.where(sel1, 1.0, jnp.where(sel2, 1.0, 0.0))
    earlier = (lax.broadcasted_iota(I32, (tm, tm), 0)
               < lax.broadcasted_iota(I32, (tm, tm), 1))
    before = jnp.dot(onehot.astype(BF16), jnp.where(earlier, 1.0, 0.0).astype(BF16),
                     preferred_element_type=F32) + run_ref[:, 0:1]
    rank_ref[0:1, :] = jnp.sum(jnp.where(sel1, before, 0.0), axis=0, keepdims=True).astype(I32)
    rank_ref[1:2, :] = jnp.sum(jnp.where(sel2, before, 0.0), axis=0, keepdims=True).astype(I32)
    run_ref[...] = run_ref[...] + jnp.sum(onehot, axis=1, keepdims=True)
    cnt_ref[...] = run_ref[...]


def _router(h, g3, glayer, rwt, mlayer, tm=512):
    pair = pl.BlockSpec((TOP_K, tm), lambda i: (0, i))
    return pl.pallas_call(
        _router_kernel,
        grid=(N_TOK // tm,),
        in_specs=[pl.BlockSpec((tm, D_MODEL), lambda i: (i, 0)),
                  pl.BlockSpec((None, 1, D_MODEL), lambda i: (glayer, 0, 0)),
                  pl.BlockSpec((None, N_EXPERTS, D_MODEL), lambda i: (mlayer, 0, 0))],
        out_specs=[pair, pair, pair, pl.BlockSpec((N_EXPERTS, LANES), lambda i: (0, 0))],
        out_shape=[jax.ShapeDtypeStruct((TOP_K, N_TOK), I32),
                   jax.ShapeDtypeStruct((TOP_K, N_TOK), F32),
                   jax.ShapeDtypeStruct((TOP_K, N_TOK), I32),
                   jax.ShapeDtypeStruct((N_EXPERTS, LANES), F32)],
        scratch_shapes=[pltpu.VMEM((N_EXPERTS, LANES), F32)],
        compiler_params=_params(1),
        name="router",
    )(h, g3, rwt)


def _dispatch_plan(idx, rank, cnt):
    counts = cnt[:, 0].astype(I32)
    tiles_e = (counts + FFN_TILE - 1) // FFN_TILE
    items_e = (tiles_e + FFN_TILES_PER_CHUNK - 1) // FFN_TILES_PER_CHUNK
    items_end = jnp.cumsum(items_e)
    items_start = items_end - items_e
    n_items = items_end[-1]
    first_row = (jnp.cumsum(tiles_e) - tiles_e) * FFN_TILE
    used_rows = jnp.sum(tiles_e) * FFN_TILE
    pos = rank
    for e in range(N_EXPERTS):
        pos = pos + jnp.where(idx == e, first_row[e], 0)
    it = jnp.arange(MOE_MAX_ITEMS, dtype=I32)
    it_c = jnp.clip(it, 0, jnp.maximum(n_items - 1, 0))
    item_expert = jnp.minimum(jnp.searchsorted(items_end, it_c, side="right"),
                              N_EXPERTS - 1).astype(I32)
    local = it_c - items_start[item_expert]
    item_tiles = jnp.clip(tiles_e[item_expert] - local * FFN_TILES_PER_CHUNK,
                          0, FFN_TILES_PER_CHUNK)
    item_tiles = jnp.where(it < n_items, item_tiles, 0).astype(I32)
    item_row0 = (first_row[item_expert] + local * FFN_CHUNK).astype(I32)
    tok = jnp.broadcast_to(jnp.arange(N_TOK, dtype=I32), (TOP_K, N_TOK))
    src = jnp.zeros((MOE_ROWS,), I32).at[pos.reshape(-1)].set(tok.reshape(-1))
    tile_valid = (jnp.arange(MOE_ROWS // DISPATCH_TILE, dtype=I32) * DISPATCH_TILE
                  < used_rows).astype(I32)
    meta = jnp.stack([n_items, used_rows]).astype(I32)
    return pos, src, tile_valid, item_expert, item_tiles, item_row0, meta


def _row_copy(src_hbm, row, dst_ref, slot, r, sem_ref):
    return pltpu.make_async_copy(src_hbm.at[pl.ds(row, 1), :],
                                 dst_ref.at[slot, pl.ds(r, 1), :], sem_ref.at[slot])


def _dispatch_kernel(src_ref, valid_ref, h_hbm, g_ref, o_ref, buf_ref, sem_ref):
    p = pl.program_id(0)
    last = pl.num_programs(0) - 1

    def issue(tile, slot):
        def body(r, carry):
            _row_copy(h_hbm, src_ref[tile * DISPATCH_TILE + r], buf_ref, slot, r, sem_ref).start()
            return carry
        lax.fori_loop(0, DISPATCH_TILE, body, 0, unroll=DMA_ISSUE_UNROLL)

    @pl.when((p == 0) & (valid_ref[0] > 0))
    def _():
        issue(0, 0)

    nxt = jnp.minimum(p + 1, last)

    @pl.when((p < last) & (valid_ref[nxt] > 0))
    def _():
        issue(nxt, nxt % 2)

    slot = p % 2

    @pl.when(valid_ref[p] > 0)
    def _():
        for r in range(DISPATCH_TILE):
            _row_copy(h_hbm, 0, buf_ref, slot, r, sem_ref).wait()
        o_ref[...] = _rms(buf_ref[slot], g_ref[...]).astype(o_ref.dtype)

    @pl.when(valid_ref[p] == 0)
    def _():
        o_ref[...] = jnp.zeros_like(o_ref)


def _dispatch(h, src, tile_valid, g3, layer):
    return pl.pallas_call(
        _dispatch_kernel,
        grid_spec=pltpu.PrefetchScalarGridSpec(
            num_scalar_prefetch=2,
            grid=(MOE_ROWS // DISPATCH_TILE,),
            in_specs=[pl.BlockSpec(memory_space=pl.ANY),
                      pl.BlockSpec((None, 1, D_MODEL), lambda p, s, v: (layer, 0, 0))],
            out_specs=pl.BlockSpec((DISPATCH_TILE, D_MODEL), lambda p, s, v: (p, 0)),
            scratch_shapes=[pltpu.VMEM((2, DISPATCH_TILE, D_MODEL), F32),
                            pltpu.SemaphoreType.DMA((2,))]),
        out_shape=jax.ShapeDtypeStruct((MOE_ROWS, D_MODEL), BF16),
        compiler_params=_params(1),
        name="moe_dispatch",
    )(src, tile_valid, h, g3)


def _ffn_kernel(exp_ref, tiles_ref, row0_ref, meta_ref, x_hbm, wg_ref, wu_ref, wd_ref, *rest,
                fused):
    if fused:
        (res_hbm, g_ref, y_hbm, a_hbm, xbuf_ref, wgu_ref, wdb_ref, acc_ref, zero_ref, xsem_ref,
         sem_ref, zsem_ref, astage_ref, asem_ref, rsem_ref) = rest
    else:
        y_hbm, xbuf_ref, wgu_ref, wdb_ref, acc_ref, zero_ref, xsem_ref, sem_ref, zsem_ref = rest
    it = pl.program_id(0)
    j = pl.program_id(1)
    last_j = pl.num_programs(1) - 1
    n_tiles = tiles_ref[it]
    n_items = meta_ref[0]
    used_rows = meta_ref[1]
    tf = wg_ref.shape[1]
    chunk = xbuf_ref.shape[1]
    big = FFN_TRIP_TILES[0]
    row0 = row0_ref[it]
    prev_tiles = jnp.where(it > 0, tiles_ref[jnp.maximum(it - 1, 0)], 0)
    x_ref = xbuf_ref.at[it % 2]

    def x_copy(item):
        return pltpu.make_async_copy(
            x_hbm.at[pl.ds(pl.multiple_of(row0_ref[item], FFN_TILE), chunk), :],
            xbuf_ref.at[item % 2], xsem_ref.at[item % 2])

    def out_copy(i):
        r = pl.multiple_of(i * FFN_TILE, FFN_TILE)
        return pltpu.make_async_copy(
            acc_ref.at[pl.ds(r, FFN_TILE), :],
            y_hbm.at[pl.ds(pl.multiple_of(row0 + r, FFN_TILE), FFN_TILE), :], sem_ref.at[i])

    def res_copy(i):
        r = pl.multiple_of(i * FFN_TILE, FFN_TILE)
        return pltpu.make_async_copy(
            res_hbm.at[pl.ds(pl.multiple_of(row0 + r, FFN_TILE), FFN_TILE), :],
            acc_ref.at[pl.ds(r, FFN_TILE), :], rsem_ref.at[0])

    def a_copy(p):
        r = pl.multiple_of(row0 + p * (big * FFN_TILE), FFN_TILE)
        return pltpu.make_async_copy(astage_ref.at[p % 2],
                                     a_hbm.at[pl.ds(r, big * FFN_TILE), :], asem_ref.at[p % 2])

    def ffn_rows(start, n_rows, first):
        rows = pl.ds(pl.multiple_of(start, FFN_TILE), n_rows)
        gu = jnp.dot(x_ref[rows, :], wgu_ref[...], preferred_element_type=F32)
        gate = gu[:, :tf]
        act = ((gate * jax.nn.sigmoid(gate)) * gu[:, tf:]).astype(BF16)
        d = jnp.dot(act, wdb_ref[...], preferred_element_type=F32)
        if first:
            acc_ref[rows, :] = d
        else:
            acc_ref[rows, :] += d

    def wait_previous(i):
        @pl.when(i < prev_tiles)
        def _():
            out_copy(i).wait()

    def run(phase):
        first = phase == "first" and not fused
        last = phase == "last"

        def trip(tile0, tiles):
            if first:
                for i in range(tiles):
                    wait_previous(tile0 + i)
            ffn_rows(tile0 * FFN_TILE, tiles * FFN_TILE, first)
            if last:
                for i in range(tiles):
                    out_copy(tile0 + i).start()

        def big_trip(p, carry):
            trip(pl.multiple_of(p * big, big), big)
            if last and fused:
                @pl.when(p >= 2)
                def _():
                    a_copy(p - 2).wait()
                rows = pl.ds(pl.multiple_of(p * (big * FFN_TILE), big * FFN_TILE), big * FFN_TILE)
                astage_ref[p % 2] = _rms(acc_ref[rows, :], g_ref[...]).astype(BF16)
                a_copy(p).start()
            return carry
        n_big = n_tiles // big
        if fused:
            lax.fori_loop(0, n_big, big_trip, 0)
        else:
            merge = (n_tiles - n_big * big == 1) & (n_big > 0)
            n_loop = n_big - jnp.where(merge, 1, 0)
            lax.fori_loop(0, n_loop, big_trip, 0)

            @pl.when(merge)
            def _():
                trip(pl.multiple_of(n_loop * big, big), big + 1)
            done = jnp.where(merge, n_tiles, n_big * big)
            for tiles in FFN_TRIP_TILES[1:]:
                take = ((n_tiles - done) // tiles) > 0

                @pl.when(take)
                def _(done=done, tiles=tiles):
                    trip(pl.multiple_of(done, tiles), tiles)
                done = done + jnp.where(take, tiles, 0)

        def drain(i, carry):
            out_copy(i).wait()
            return carry
        if phase == "first":
            lax.fori_loop(n_tiles, jnp.maximum(prev_tiles, n_tiles), drain, 0)
        if last:
            @pl.when(it == n_items - 1)
            def _():
                lax.fori_loop(0, n_tiles, drain, 0)
            if fused:
                def drain_a(p, carry):
                    a_copy(p).wait()
                    return carry
                lax.fori_loop(jnp.maximum(n_big - 2, 0), n_big, drain_a, 0)

    zero_rows = zero_ref.shape[0]

    def zero_copy(k):
        row = pl.multiple_of(used_rows + k * zero_rows, zero_rows)
        return pltpu.make_async_copy(zero_ref, y_hbm.at[pl.ds(row, zero_rows), :], zsem_ref.at[0])

    @pl.when((it == n_items - 1) & (j == last_j))
    def _():
        zero_ref[...] = jnp.zeros_like(zero_ref)
        pieces = (y_hbm.shape[0] - used_rows) // zero_rows

        def start(k, carry):
            zero_copy(k).start()
            return carry

        def wait(k, carry):
            zero_copy(k).wait()
            return carry
        lax.fori_loop(0, pieces, start, 0)
        lax.fori_loop(0, pieces, wait, 0)

    @pl.when(n_tiles > 0)
    def _():
        wgu_ref[:, :tf] = wg_ref[...].astype(BF16)
        wgu_ref[:, tf:] = wu_ref[...].astype(BF16)
        wdb_ref[...] = wd_ref[...].astype(BF16)

        @pl.when(j == 0)
        def _():
            @pl.when(it == 0)
            def _():
                x_copy(0).start()
            x_copy(it).wait()

            @pl.when(it + 1 < n_items)
            def _():
                x_copy(it + 1).start()
            if fused:
                def start(i, carry):
                    wait_previous(i)
                    res_copy(i).start()
                    return carry

                def wait(i, carry):
                    res_copy(i).wait()
                    return carry
                lax.fori_loop(0, n_tiles, start, 0)
                lax.fori_loop(0, n_tiles, wait, 0)
            run("first")

        @pl.when((j > 0) & (j < last_j))
        def _():
            run("middle")

        @pl.when(j == last_j)
        def _():
            run("last")


def _grouped_ffn(xs, wg, wu, wd, layer, item_expert, item_tiles, item_row0, meta, chunk,
                 residual=None):
    n_items = item_expert.shape[0]
    out_rows = xs.shape[0]
    nj = D_FF // FFN_TF
    fused = residual is not None

    def col(it, j, e, t, r, m):
        return jnp.where(t[it] > 0, j, nj - 1)

    in_specs = [
        pl.BlockSpec(memory_space=pl.ANY),
        pl.BlockSpec((None, None, D_MODEL, FFN_TF),
                     lambda it, j, e, t, r, m: (layer, e[it], 0, col(it, j, e, t, r, m))),
        pl.BlockSpec((None, None, D_MODEL, FFN_TF),
                     lambda it, j, e, t, r, m: (layer, e[it], 0, col(it, j, e, t, r, m))),
        pl.BlockSpec((None, None, FFN_TF, D_MODEL),
                     lambda it, j, e, t, r, m: (layer, e[it], col(it, j, e, t, r, m), 0)),
    ]
    args = [item_expert, item_tiles, item_row0, meta, xs, wg, wu, wd]
    out_specs = pl.BlockSpec(memory_space=pl.ANY)
    out_shape = jax.ShapeDtypeStruct((out_rows, D_MODEL), F32)
    scratch = [pltpu.VMEM((2, chunk, D_MODEL), BF16),
               pltpu.VMEM((D_MODEL, 2 * FFN_TF), BF16),
               pltpu.VMEM((FFN_TF, D_MODEL), BF16),
               pltpu.VMEM((chunk, D_MODEL), F32),
               pltpu.VMEM((FFN_ZERO_ROWS, D_MODEL), F32),
               pltpu.SemaphoreType.DMA((2,)),
               pltpu.SemaphoreType.DMA((chunk // FFN_TILE,)),
               pltpu.SemaphoreType.DMA((1,))]
    if fused:
        res, g3, g_layer = residual
        in_specs += [pl.BlockSpec(memory_space=pl.ANY),
                     pl.BlockSpec((None, 1, D_MODEL),
                                  lambda it, j, e, t, r, m: (g_layer, 0, 0))]
        args += [res, g3]
        out_specs = [out_specs, pl.BlockSpec(memory_space=pl.ANY)]
        out_shape = [out_shape, jax.ShapeDtypeStruct((out_rows, D_MODEL), BF16)]
        scratch += [pltpu.VMEM((2, FFN_TRIP_TILES[0] * FFN_TILE, D_MODEL), BF16),
                    pltpu.SemaphoreType.DMA((2,)),
                    pltpu.SemaphoreType.DMA((1,))]
    return pl.pallas_call(
        functools.partial(_ffn_kernel, fused=fused),
        grid_spec=pltpu.PrefetchScalarGridSpec(
            num_scalar_prefetch=4,
            grid=(n_items, nj),
            in_specs=in_specs,
            out_specs=out_specs,
            scratch_shapes=scratch),
        out_shape=out_shape,
        compiler_params=_params(2),
        name="grouped_ffn",
    )(*args)


def _dense_ffn(f, wg, wu, wd, layer, h, g3, g_layer):
    n_items = N_TOK // DENSE_CHUNK
    assert DENSE_CHUNK % (FFN_TRIP_TILES[0] * FFN_TILE) == 0
    return _grouped_ffn(
        f, wg[:, None], wu[:, None], wd[:, None], layer,
        jnp.zeros((n_items,), I32), jnp.full((n_items,), DENSE_CHUNK // FFN_TILE, I32),
        jnp.arange(n_items, dtype=I32) * DENSE_CHUNK, jnp.array([n_items, N_TOK], I32),
        DENSE_CHUNK, residual=(h, g3, g_layer))


def _combine_kernel(pos_ref, h_ref, gate_ref, g_ref, y_hbm, *rest, final):
    if final:
        o_ref, buf_ref, sem_ref = rest
    else:
        hn_ref, a_ref, buf_ref, sem_ref = rest
    t = pl.program_id(0)
    last = pl.num_programs(0) - 1
    tm = h_ref.shape[0]

    def issue(tile, slot):
        def body(r, carry):
            for k in range(TOP_K):
                _row_copy(y_hbm, pos_ref[k * N_TOK + tile * tm + r],
                          buf_ref, slot, k * tm + r, sem_ref).start()
            return carry
        lax.fori_loop(0, tm, body, 0, unroll=DMA_ISSUE_UNROLL)

    @pl.when(t == 0)
    def _():
        issue(0, 0)

    @pl.when(t < last)
    def _():
        issue(t + 1, (t + 1) % 2)

    slot = t % 2
    for r in range(TOP_K * tm):
        _row_copy(y_hbm, 0, buf_ref, slot, r, sem_ref).wait()
    hn = (h_ref[...] + gate_ref[:, 0:1] * buf_ref[slot, pl.ds(0, tm), :]
          + gate_ref[:, 1:2] * buf_ref[slot, pl.ds(tm, tm), :])
    if final:
        o_ref[...] = _rms(hn, g_ref[...])
    else:
        hn_ref[...] = hn
        a_ref[...] = _rms(hn, g_ref[...]).astype(a_ref.dtype)


def _combine(h, y, pos, gates_t, g3, layer, final, tm=256):
    row = pl.BlockSpec((tm, D_MODEL), lambda t, p: (t, 0))
    if final:
        out_specs = row
        out_shape = jax.ShapeDtypeStruct((N_TOK, D_MODEL), F32)
    else:
        out_specs = [row, row]
        out_shape = [jax.ShapeDtypeStruct((N_TOK, D_MODEL), F32),
                     jax.ShapeDtypeStruct((N_TOK, D_MODEL), BF16)]
    return pl.pallas_call(
        functools.partial(_combine_kernel, final=final),
        grid_spec=pltpu.PrefetchScalarGridSpec(
            num_scalar_prefetch=1,
            grid=(N_TOK // tm,),
            in_specs=[row,
                      pl.BlockSpec((tm, TOP_K), lambda t, p: (t, 0)),
                      pl.BlockSpec((None, 1, D_MODEL), lambda t, p: (layer, 0, 0)),
                      pl.BlockSpec(memory_space=pl.ANY)],
            out_specs=out_specs,
            scratch_shapes=[pltpu.VMEM((2, TOP_K * tm, D_MODEL), F32),
                            pltpu.SemaphoreType.DMA((2,))]),
        out_shape=out_shape,
        compiler_params=_params(1),
        name="moe_combine",
    )(pos.reshape(-1), h, gates_t, g3, y)


def _alibi_slopes():
    n = N_DIL * N_HEADS
    s = jnp.exp2(-8.0 * jnp.arange(1, n + 1, dtype=F32) / n)
    return s.reshape(N_HEADS, N_DIL).T


def kernel(x, mix_norm_g, ffn_norm_g, attn_w_in, attn_w_out, gm_w_in, gm_v_norm_g, gm_w_s,
           gm_b_s, gm_w_out, dense_w_gate, dense_w_up, dense_w_down, router_w, moe_w_gate,
           moe_w_up, moe_w_down, final_norm_g):
    h = x.reshape(N_TOK, D_MODEL)
    mix_g = mix_norm_g.reshape(DEPTH, 1, D_MODEL)
    ffn_g = ffn_norm_g.reshape(DEPTH, 1, D_MODEL)
    final_g = final_norm_g.reshape(1, 1, D_MODEL)
    gm_vg = gm_v_norm_g.reshape(-1, 1, GM_WIDTH)
    gm_bt = jnp.swapaxes(gm_b_s, 1, 2)
    router_wt = jnp.swapaxes(router_w, 1, 2)
    slopes = _alibi_slopes()

    a = _rmsnorm(h, mix_g, 0, BF16)
    for i in range(DEPTH):
        j = i // 2
        if i % 2 == 0:
            groups = [_mm_qkv(a, attn_w_in, j, dil, g * (QKV_GROUP_COLS // 1024))
                      for g, (win, dil) in enumerate(DIL_CONFIGS)]
            h, f = _proj_norm(_attention(groups, slopes), attn_w_out, j, h, ffn_g, i)
            h, a = _dense_ffn(f, dense_w_gate, dense_w_up, dense_w_down, j, h, mix_g, i + 1)
        else:
            z = _mm_act(a, gm_w_in, j, 2 * GM_WIDTH, "gelu")
            y = _gm_spatial(z, gm_vg, gm_w_s, gm_bt, j)
            h = _proj(y, gm_w_out, j, h)
            idx, gates, rank, cnt = _router(h, ffn_g, i, router_wt, j)
            pos, src, tile_valid, item_expert, item_tiles, item_row0, meta = _dispatch_plan(
                idx, rank, cnt)
            xs = _dispatch(h, src, tile_valid, ffn_g, i)
            y = _grouped_ffn(xs, moe_w_gate, moe_w_up, moe_w_down, j,
                             item_expert, item_tiles, item_row0, meta, FFN_CHUNK)
            if i == DEPTH - 1:
                return _combine(h, y, pos, gates.T, final_g, 0, True).reshape(BATCH, SEQ, D_MODEL)
            h, a = _combine(h, y, pos, gates.T, mix_g, i + 1, False)
```

```python
import functools

import jax
import jax.numpy as jnp
from jax import lax
from jax.experimental import pallas as pl
from jax.experimental.pallas import tpu as pltpu

D_MODEL = 2048
BATCH = 2
SEQ = 4096
DEPTH = 4
N_TOK = BATCH * SEQ
HEAD_DIM = 128
N_HEADS = D_MODEL // HEAD_DIM
DIL_CONFIGS = ((128, 1), (512, 4), (2048, 16))
N_DIL = len(DIL_CONFIGS)
ATT_BLK = 128
ATT_SKEW = 3
QKV_GROUP_COLS = 3 * N_HEADS * HEAD_DIM
GM_CHUNK = 128
GM_WIDTH = D_MODEL
GM_GROUP_DIM = 128
GM_GROUPS = GM_WIDTH // GM_GROUP_DIM
D_FF = 7 * D_MODEL // 2
N_EXPERTS = 8
TOP_K = 2
EPS = 1e-6
NEG_INF = -1e30

LANES = 128
VMEM_LIMIT = 56 * 1024 * 1024
CAST_ROWS = 256

FFN_TILE = 128
FFN_TRIP_TILES = (4, 2, 1)
DISPATCH_TILE = 256
FFN_CHUNK = 2304
FFN_TILES_PER_CHUNK = FFN_CHUNK // FFN_TILE
DENSE_CHUNK = 2048
FFN_TF = 256
DMA_ISSUE_UNROLL = 8
FFN_ZERO_ROWS = 64
MOE_MAX_ITEMS = -(-TOP_K * N_TOK // FFN_CHUNK) + N_EXPERTS
MOE_ROWS = TOP_K * N_TOK + N_EXPERTS * FFN_TILE + FFN_CHUNK
assert MOE_ROWS % DISPATCH_TILE == 0

F32 = jnp.float32
BF16 = jnp.bfloat16
I32 = jnp.int32


def _params(n_axes):
    return pltpu.CompilerParams(
        dimension_semantics=("arbitrary",) * n_axes, vmem_limit_bytes=VMEM_LIMIT)


def _cast_weight(w_ref, wb_ref):
    def body(c, carry):
        r = pl.multiple_of(c * CAST_ROWS, CAST_ROWS)
        wb_ref[pl.ds(r, CAST_ROWS), :] = w_ref[pl.ds(r, CAST_ROWS), :].astype(BF16)
        return carry
    lax.fori_loop(0, w_ref.shape[0] // CAST_ROWS, body, 0)


def _gelu_tanh(x):
    return 0.5 * x * (1.0 + jnp.tanh(0.7978845608028654 * (x + 0.044715 * (x * x * x))))


def _rms(x, g):
    ms = jnp.mean(x * x, axis=-1, keepdims=True)
    return (x * lax.rsqrt(ms + EPS)) * g


def _rms_kernel(x_ref, g_ref, o_ref):
    o_ref[...] = _rms(x_ref[...], g_ref[...]).astype(o_ref.dtype)


def _rmsnorm(h, g3, layer, out_dtype, tm=512):
    return pl.pallas_call(
        _rms_kernel,
        grid=(N_TOK // tm,),
        in_specs=[pl.BlockSpec((tm, D_MODEL), lambda i: (i, 0)),
                  pl.BlockSpec((None, 1, D_MODEL), lambda i: (layer, 0, 0))],
        out_specs=pl.BlockSpec((tm, D_MODEL), lambda i: (i, 0)),
        out_shape=jax.ShapeDtypeStruct((N_TOK, D_MODEL), out_dtype),
        compiler_params=_params(1),
        name="rmsnorm",
    )(h, g3)


def _w_spec(layer, k, tn, col_off):
    return pl.BlockSpec((None, k, tn), lambda j, i: (layer, 0, col_off + j))


def _mm_act_kernel(a_ref, w_ref, o_ref, wb_ref, *, act):
    @pl.when(pl.program_id(1) == 0)
    def _():
        _cast_weight(w_ref, wb_ref)
    acc = jnp.dot(a_ref[...], wb_ref[...], preferred_element_type=F32)
    if act == "gelu":
        acc = _gelu_tanh(acc)
    o_ref[...] = acc.astype(o_ref.dtype)


def _mm_act(a, w, layer, n_cols, act, tm=1024, tn=1024, col_off=0):
    k = a.shape[1]
    return pl.pallas_call(
        functools.partial(_mm_act_kernel, act=act),
        grid=(n_cols // tn, N_TOK // tm),
        in_specs=[pl.BlockSpec((tm, k), lambda j, i: (i, 0)),
                  _w_spec(layer, k, tn, col_off)],
        out_specs=pl.BlockSpec((tm, tn), lambda j, i: (i, j)),
        out_shape=jax.ShapeDtypeStruct((N_TOK, n_cols), BF16),
        scratch_shapes=[pltpu.VMEM((k, tn), BF16)],
        compiler_params=_params(2),
        name="mm_act",
    )(a, w)


def _mm_qkv_kernel(a_ref, w_ref, o_ref, wb_ref, *rest, dil):
    @pl.when(pl.program_id(1) == 0)
    def _():
        _cast_weight(w_ref, wb_ref)
    acc = jnp.dot(a_ref[...], wb_ref[...], preferred_element_type=F32)
    heads = o_ref.shape[0]
    if dil == 1:
        for hh in range(heads):
            o_ref[hh, 0] = acc[:, hh * HEAD_DIM:(hh + 1) * HEAD_DIM].astype(o_ref.dtype)
    else:
        acc_ref, = rest
        rows = acc_ref.shape[1] // dil
        for hh in range(heads):
            acc_ref[hh] = acc[:, hh * HEAD_DIM:(hh + 1) * HEAD_DIM]

        def regroup(hh, carry):
            for r in range(dil):
                o_ref[hh, r] = acc_ref[hh, pl.ds(r, rows, stride=dil), :].astype(o_ref.dtype)
            return carry
        lax.fori_loop(0, heads, regroup, 0, unroll=dil <= 4)


def _mm_qkv(a, w, layer, dil, col_off, tm=1024, tn=1024):
    k = a.shape[1]
    tiles_per_b = SEQ // tm
    heads = tn // HEAD_DIM
    scratch = [pltpu.VMEM((k, tn), BF16)]
    if dil > 1:
        scratch.append(pltpu.VMEM((heads, tm, HEAD_DIM), F32))
    out = pl.pallas_call(
        functools.partial(_mm_qkv_kernel, dil=dil),
        grid=(QKV_GROUP_COLS // tn, N_TOK // tm),
        in_specs=[pl.BlockSpec((tm, k), lambda j, i: (i, 0)),
                  _w_spec(layer, k, tn, col_off)],
        out_specs=pl.BlockSpec((None, heads, dil, tm // dil, HEAD_DIM),
                               lambda j, i: (i // tiles_per_b, j, 0, i % tiles_per_b, 0)),
        out_shape=jax.ShapeDtypeStruct(
            (BATCH, 3 * N_HEADS, dil, SEQ // dil, HEAD_DIM), BF16),
        scratch_shapes=scratch,
        compiler_params=_params(2),
        name="mm_qkv",
    )(a, w)
    return out.reshape(BATCH, 3 * N_HEADS, SEQ, HEAD_DIM)


def _proj_kernel(a_ref, w_ref, res_ref, o_ref, wb_ref):
    @pl.when(pl.program_id(1) == 0)
    def _():
        _cast_weight(w_ref, wb_ref)
    o_ref[...] = res_ref[...] + jnp.dot(a_ref[...], wb_ref[...], preferred_element_type=F32)


def _proj(a, w, layer, res, tm=512, tn=1024):
    k = a.shape[1]
    return pl.pallas_call(
        _proj_kernel,
        grid=(D_MODEL // tn, N_TOK // tm),
        in_specs=[pl.BlockSpec((tm, k), lambda j, i: (i, 0)),
                  _w_spec(layer, k, tn, 0),
                  pl.BlockSpec((tm, tn), lambda j, i: (i, j))],
        out_specs=pl.BlockSpec((tm, tn), lambda j, i: (i, j)),
        out_shape=jax.ShapeDtypeStruct((N_TOK, D_MODEL), F32),
        scratch_shapes=[pltpu.VMEM((k, tn), BF16)],
        compiler_params=_params(2),
        name="proj",
    )(a, w, res)


def _proj_norm_kernel(a_ref, w_ref, res_ref, g_ref, o_ref, f_ref, wb_ref):
    @pl.when(pl.program_id(1) == 0)
    def _():
        _cast_weight(w_ref, wb_ref)
    a = jnp.concatenate([a_ref[hh] for hh in range(a_ref.shape[0])], axis=1)
    out = res_ref[...] + jnp.dot(a, wb_ref[...], preferred_element_type=F32)
    o_ref[...] = out
    f_ref[...] = _rms(out, g_ref[...]).astype(f_ref.dtype)


def _proj_norm(a, w, layer, res, g3, g_layer, tm=256):
    tiles_per_b = SEQ // tm
    k = a.shape[1] * a.shape[3]
    row = pl.BlockSpec((tm, D_MODEL), lambda j, i: (i, 0))
    return pl.pallas_call(
        _proj_norm_kernel,
        grid=(1, N_TOK // tm),
        in_specs=[pl.BlockSpec((None, a.shape[1], tm, a.shape[3]),
                               lambda j, i: (i // tiles_per_b, 0, i % tiles_per_b, 0)),
                  _w_spec(layer, k, D_MODEL, 0),
                  row,
                  pl.BlockSpec((None, 1, D_MODEL), lambda j, i: (g_layer, 0, 0))],
        out_specs=[row, row],
        out_shape=[jax.ShapeDtypeStruct((N_TOK, D_MODEL), F32),
                   jax.ShapeDtypeStruct((N_TOK, D_MODEL), BF16)],
        scratch_shapes=[pltpu.VMEM((k, D_MODEL), BF16)],
        compiler_params=_params(2),
        name="proj_norm",
    )(a, w, res, g3)


def _attn_kernel(slopes_ref, *refs):
    qkv_refs = refs[:9]
    o_ref, tab_ref, oscr_ref, lscr_ref = refs[9:]
    h = pl.program_id(1)
    ii = lax.broadcasted_iota(I32, (ATT_BLK, 2 * ATT_BLK), 0)
    jj = lax.broadcasted_iota(I32, (ATT_BLK, 2 * ATT_BLK), 1)
    delta = ii + ATT_BLK - jj
    scale = HEAD_DIM ** -0.5
    nt = (((1,), (1,)), ((), ()))
    n_blocks = SEQ // ATT_BLK
    ones = jnp.ones((2 * ATT_BLK, HEAD_DIM), BF16)
    for g, (win, dil) in enumerate(DIL_CONFIGS):
        q_ref, k_ref, v_ref = qkv_refs[3 * g:3 * g + 3]
        nb = n_blocks // dil
        tab_ref[...] = jnp.where((delta >= 0) & (delta <= ATT_BLK),
                                 -slopes_ref[g, h] * (dil * delta).astype(F32), NEG_INF)
        s, m, p, done = {}, {}, {}, []

        def keys(c, nb=nb):
            first = c % nb == 0
            return (slice((c - (not first)) * ATT_BLK, (c + 1) * ATT_BLK),
                    slice(ATT_BLK if first else 0, 2 * ATT_BLK))

        def scores(c):
            rows, cols = keys(c)
            q = q_ref[c * ATT_BLK:(c + 1) * ATT_BLK, :]
            s[c] = lax.dot_general(q, k_ref[rows, :], nt,
                                   preferred_element_type=F32) * scale + tab_ref[:, cols]

        def softmax(c):
            m[c] = jnp.max(s[c], axis=-1, keepdims=True)
            p[c] = jnp.exp(s.pop(c) - m[c]).astype(BF16)

        def values(c):
            rows, _ = keys(c)
            vv = jnp.concatenate([v_ref[rows, :], ones[:rows.stop - rows.start]], axis=1)
            od = jnp.dot(p.pop(c), vv, preferred_element_type=F32)
            den = od[:, HEAD_DIM:]
            done.append((c, od[:, :HEAD_DIM] / den, m.pop(c) + jnp.log(den)))

        for step in range(n_blocks + 2 * ATT_SKEW):
            if step < n_blocks:
                scores(step)
            if 0 <= step - ATT_SKEW < n_blocks:
                softmax(step - ATT_SKEW)
            if 0 <= step - 2 * ATT_SKEW < n_blocks:
                values(step - 2 * ATT_SKEW)
        for c, o, lse in done:
            start = (c % nb) * (ATT_BLK * dil) + c // nb
            dst = pl.ds(start, ATT_BLK) if dil == 1 else pl.ds(start, ATT_BLK, stride=dil)
            oscr_ref[g, dst, :] = o
            lscr_ref[g, dst, :] = jnp.broadcast_to(lse, (ATT_BLK, HEAD_DIM))

    rows = 256

    def merge(t, carry):
        sl = pl.ds(pl.multiple_of(t * rows, rows), rows)
        l0 = lscr_ref[0, sl, :]
        l1 = lscr_ref[1, sl, :]
        l2 = lscr_ref[2, sl, :]
        m = jnp.maximum(jnp.maximum(l0, l1), l2)
        e0 = jnp.exp(l0 - m)
        e1 = jnp.exp(l1 - m)
        e2 = jnp.exp(l2 - m)
        den = e0 + e1 + e2
        o = (e0 * oscr_ref[0, sl, :] + e1 * oscr_ref[1, sl, :] + e2 * oscr_ref[2, sl, :]) / den
        o_ref[sl, :] = o.astype(o_ref.dtype)
        return carry

    lax.fori_loop(0, SEQ // rows, merge, 0)


def _attention(qkv_groups, slopes):
    in_specs = [pl.BlockSpec(memory_space=pltpu.SMEM)]
    args = [slopes]
    for qkv in qkv_groups:
        for part in range(3):
            in_specs.append(pl.BlockSpec(
                (None, None, SEQ, HEAD_DIM),
                lambda b, h, part=part: (b, part * N_HEADS + h, 0, 0)))
            args.append(qkv)
    return pl.pallas_call(
        _attn_kernel,
        grid=(BATCH, N_HEADS),
        in_specs=in_specs,
        out_specs=pl.BlockSpec((None, None, SEQ, HEAD_DIM), lambda b, h: (b, h, 0, 0)),
        out_shape=jax.ShapeDtypeStruct((BATCH, N_HEADS, SEQ, HEAD_DIM), BF16),
        scratch_shapes=[pltpu.VMEM((ATT_BLK, 2 * ATT_BLK), F32),
                        pltpu.VMEM((N_DIL, SEQ, HEAD_DIM), F32),
                        pltpu.VMEM((N_DIL, SEQ, HEAD_DIM), F32)],
        compiler_params=_params(2),
        name="dilated_attn",
    )(*args)


def _gm_spatial_kernel(z_ref, vg_ref, ws_ref, bt_ref, y_ref, wsb_ref):
    @pl.when(pl.program_id(0) == 0)
    def _():
        ii = lax.broadcasted_iota(I32, (GM_CHUNK, GM_CHUNK), 0)
        jj = lax.broadcasted_iota(I32, (GM_CHUNK, GM_CHUNK), 1)
        for g in range(GM_GROUPS):
            wsb_ref[g] = jnp.where(ii >= jj, ws_ref[g], 0.0).astype(BF16)

    for c in range(z_ref.shape[0] // GM_CHUNK):
        rows = slice(c * GM_CHUNK, (c + 1) * GM_CHUNK)
        vn = _rms(z_ref[rows, GM_WIDTH:].astype(F32), vg_ref[...]).astype(BF16)
        for g in range(GM_GROUPS):
            cols = slice(g * GM_GROUP_DIM, (g + 1) * GM_GROUP_DIM)
            s = jnp.dot(wsb_ref[g], vn[:, cols], preferred_element_type=F32) + bt_ref[:, g:g + 1]
            y_ref[rows, cols] = (z_ref[rows, cols].astype(F32) * s).astype(y_ref.dtype)


def _gm_spatial(z, vg3, ws, bt, layer, tm=256):
    return pl.pallas_call(
        _gm_spatial_kernel,
        grid=(N_TOK // tm,),
        in_specs=[pl.BlockSpec((tm, 2 * GM_WIDTH), lambda i: (i, 0)),
                  pl.BlockSpec((None, 1, GM_WIDTH), lambda i: (layer, 0, 0)),
                  pl.BlockSpec((None, GM_GROUPS, GM_CHUNK, GM_CHUNK), lambda i: (layer, 0, 0, 0)),
                  pl.BlockSpec((None, GM_CHUNK, GM_GROUPS), lambda i: (layer, 0, 0))],
        out_specs=pl.BlockSpec((tm, GM_WIDTH), lambda i: (i, 0)),
        out_shape=jax.ShapeDtypeStruct((N_TOK, GM_WIDTH), BF16),
        scratch_shapes=[pltpu.VMEM((GM_GROUPS, GM_CHUNK, GM_CHUNK), BF16)],
        compiler_params=_params(1),
        name="gm_spatial",
    )(z, vg3, ws, bt)


def _router_kernel(h_ref, g_ref, rwt_ref, idx_ref, gate_ref, rank_ref, cnt_ref, run_ref):
    @pl.when(pl.program_id(0) == 0)
    def _():
        run_ref[...] = jnp.zeros_like(run_ref)

    f = _rms(h_ref[...], g_ref[...])
    logits = lax.dot_general(rwt_ref[...], f, (((1,), (1,)), ((), ())),
                             precision=lax.Precision.HIGHEST, preferred_element_type=F32)
    tm = logits.shape[1]
    eid = lax.broadcasted_iota(I32, logits.shape, 0)
    m1 = jnp.max(logits, axis=0, keepdims=True)
    i1 = jnp.min(jnp.where(logits == m1, eid, N_EXPERTS), axis=0, keepdims=True)
    rest = jnp.where(eid == i1, -jnp.inf, logits)
    m2 = jnp.max(rest, axis=0, keepdims=True)
    i2 = jnp.min(jnp.where(rest == m2, eid, N_EXPERTS), axis=0, keepdims=True)
    e2 = jnp.exp(m2 - m1)
    den = 1.0 + e2
    idx_ref[0:1, :] = i1
    idx_ref[1:2, :] = i2
    gate_ref[0:1, :] = 1.0 / den
    gate_ref[1:2, :] = e2 / den

    sel1 = eid == i1
    sel2 = eid == i2
    onehot = jnp.where(sel1, 1.0, jnp.where(sel2, 1.0, 0.0))
    earlier = (lax.broadcasted_iota(I32, (tm, tm), 0)
               < lax.broadcasted_iota(I32, (tm, tm), 1))
    before = jnp.dot(onehot.astype(BF16), jnp.where(earlier, 1.0, 0.0).astype(BF16),
                     preferred_element_type=F32) + run_ref[:, 0:1]
    rank_ref[0:1, :] = jnp.sum(jnp.where(sel1, before, 0.0), axis=0, keepdims=True).astype(I32)
    rank_ref[1:2, :] = jnp.sum(jnp.where(sel2, before, 0.0), axis=0, keepdims=True).astype(I32)
    run_ref[...] = run_ref[...] + jnp.sum(onehot, axis=1, keepdims=True)
    cnt_ref[...] = run_ref[...]


def _router(h, g3, glayer, rwt, mlayer, tm=512):
    pair = pl.BlockSpec((TOP_K, tm), lambda i: (0, i))
    return pl.pallas_call(
        _router_kernel,
        grid=(N_TOK // tm,),
        in_specs=[pl.BlockSpec((tm, D_MODEL), lambda i: (i, 0)),
                  pl.BlockSpec((None, 1, D_MODEL), lambda i: (glayer, 0, 0)),
                  pl.BlockSpec((None, N_EXPERTS, D_MODEL), lambda i: (mlayer, 0, 0))],
        out_specs=[pair, pair, pair, pl.BlockSpec((N_EXPERTS, LANES), lambda i: (0, 0))],
        out_shape=[jax.ShapeDtypeStruct((TOP_K, N_TOK), I32),
                   jax.ShapeDtypeStruct((TOP_K, N_TOK), F32),
                   jax.ShapeDtypeStruct((TOP_K, N_TOK), I32),
                   jax.ShapeDtypeStruct((N_EXPERTS, LANES), F32)],
        scratch_shapes=[pltpu.VMEM((N_EXPERTS, LANES), F32)],
        compiler_params=_params(1),
        name="router",
    )(h, g3, rwt)


def _dispatch_plan(idx, rank, cnt):
    counts = cnt[:, 0].astype(I32)
    tiles_e = (counts + FFN_TILE - 1) // FFN_TILE
    items_e = (tiles_e + FFN_TILES_PER_CHUNK - 1) // FFN_TILES_PER_CHUNK
    items_end = jnp.cumsum(items_e)
    items_start = items_end - items_e
    n_items = items_end[-1]
    first_row = (jnp.cumsum(tiles_e) - tiles_e) * FFN_TILE
    used_rows = jnp.sum(tiles_e) * FFN_TILE
    pos = rank
    for e in range(N_EXPERTS):
        pos = pos + jnp.where(idx == e, first_row[e], 0)
    it = jnp.arange(MOE_MAX_ITEMS, dtype=I32)
    it_c = jnp.clip(it, 0, jnp.maximum(n_items - 1, 0))
    item_expert = jnp.minimum(jnp.searchsorted(items_end, it_c, side="right"),
                              N_EXPERTS - 1).astype(I32)
    local = it_c - items_start[item_expert]
    item_tiles = jnp.clip(tiles_e[item_expert] - local * FFN_TILES_PER_CHUNK,
                          0, FFN_TILES_PER_CHUNK)
    item_tiles = jnp.where(it < n_items, item_tiles, 0).astype(I32)
    item_row0 = (first_row[item_expert] + local * FFN_CHUNK).astype(I32)
    tok = jnp.broadcast_to(jnp.arange(N_TOK, dtype=I32), (TOP_K, N_TOK))
    src = jnp.zeros((MOE_ROWS,), I32).at[pos.reshape(-1)].set(tok.reshape(-1))
    tile_valid = (jnp.arange(MOE_ROWS // DISPATCH_TILE, dtype=I32) * DISPATCH_TILE
                  < used_rows).astype(I32)
    meta = jnp.stack([n_items, used_rows]).astype(I32)
    return pos, src, tile_valid, item_expert, item_tiles, item_row0, meta


def _row_copy(src_hbm, row, dst_ref, slot, r, sem_ref):
    return pltpu.make_async_copy(src_hbm.at[pl.ds(row, 1), :],
                                 dst_ref.at[slot, pl.ds(r, 1), :], sem_ref.at[slot])


def _dispatch_kernel(src_ref, valid_ref, h_hbm, g_ref, o_ref, buf_ref, sem_ref):
    p = pl.program_id(0)
    last = pl.num_programs(0) - 1

    def issue(tile, slot):
        def body(r, carry):
            _row_copy(h_hbm, src_ref[tile * DISPATCH_TILE + r], buf_ref, slot, r, sem_ref).start()
            return carry
        lax.fori_loop(0, DISPATCH_TILE, body, 0, unroll=DMA_ISSUE_UNROLL)

    @pl.when((p == 0) & (valid_ref[0] > 0))
    def _():
        issue(0, 0)

    nxt = jnp.minimum(p + 1, last)

    @pl.when((p < last) & (valid_ref[nxt] > 0))
    def _():
        issue(nxt, nxt % 2)

    slot = p % 2

    @pl.when(valid_ref[p] > 0)
    def _():
        for r in range(DISPATCH_TILE):
            _row_copy(h_hbm, 0, buf_ref, slot, r, sem_ref).wait()
        o_ref[...] = _rms(buf_ref[slot], g_ref[...]).astype(o_ref.dtype)

    @pl.when(valid_ref[p] == 0)
    def _():
        o_ref[...] = jnp.zeros_like(o_ref)


def _dispatch(h, src, tile_valid, g3, layer):
    return pl.pallas_call(
        _dispatch_kernel,
        grid_spec=pltpu.PrefetchScalarGridSpec(
            num_scalar_prefetch=2,
            grid=(MOE_ROWS // DISPATCH_TILE,),
            in_specs=[pl.BlockSpec(memory_space=pl.ANY),
                      pl.BlockSpec((None, 1, D_MODEL), lambda p, s, v: (layer, 0, 0))],
            out_specs=pl.BlockSpec((DISPATCH_TILE, D_MODEL), lambda p, s, v: (p, 0)),
            scratch_shapes=[pltpu.VMEM((2, DISPATCH_TILE, D_MODEL), F32),
                            pltpu.SemaphoreType.DMA((2,))]),
        out_shape=jax.ShapeDtypeStruct((MOE_ROWS, D_MODEL), BF16),
        compiler_params=_params(1),
        name="moe_dispatch",
    )(src, tile_valid, h, g3)


def _ffn_kernel(exp_ref, tiles_ref, row0_ref, meta_ref, x_hbm, wg_ref, wu_ref, wd_ref, *rest,
                fused):
    if fused:
        (res_hbm, g_ref, y_hbm, a_hbm, xbuf_ref, wgu_ref, wdb_ref, acc_ref, zero_ref, xsem_ref,
         sem_ref, zsem_ref, astage_ref, asem_ref, rsem_ref) = rest
    else:
        y_hbm, xbuf_ref, wgu_ref, wdb_ref, acc_ref, zero_ref, xsem_ref, sem_ref, zsem_ref = rest
    it = pl.program_id(0)
    j = pl.program_id(1)
    last_j = pl.num_programs(1) - 1
    n_tiles = tiles_ref[it]
    n_items = meta_ref[0]
    used_rows = meta_ref[1]
    tf = wg_ref.shape[1]
    chunk = xbuf_ref.shape[1]
    big = FFN_TRIP_TILES[0]
    row0 = row0_ref[it]
    prev_tiles = jnp.where(it > 0, tiles_ref[jnp.maximum(it - 1, 0)], 0)
    x_ref = xbuf_ref.at[it % 2]

    def x_copy(item):
        return pltpu.make_async_copy(
            x_hbm.at[pl.ds(pl.multiple_of(row0_ref[item], FFN_TILE), chunk), :],
            xbuf_ref.at[item % 2], xsem_ref.at[item % 2])

    def out_copy(i):
        r = pl.multiple_of(i * FFN_TILE, FFN_TILE)
        return pltpu.make_async_copy(
            acc_ref.at[pl.ds(r, FFN_TILE), :],
            y_hbm.at[pl.ds(pl.multiple_of(row0 + r, FFN_TILE), FFN_TILE), :], sem_ref.at[i])

    def res_copy(i):
        r = pl.multiple_of(i * FFN_TILE, FFN_TILE)
        return pltpu.make_async_copy(
            res_hbm.at[pl.ds(pl.multiple_of(row0 + r, FFN_TILE), FFN_TILE), :],
            acc_ref.at[pl.ds(r, FFN_TILE), :], rsem_ref.at[0])

    def a_copy(p):
        r = pl.multiple_of(row0 + p * (big * FFN_TILE), FFN_TILE)
        return pltpu.make_async_copy(astage_ref.at[p % 2],
                                     a_hbm.at[pl.ds(r, big * FFN_TILE), :], asem_ref.at[p % 2])

    def cast_weights():
        wgu_ref[:, :tf] = wg_ref[...].astype(BF16)
        wgu_ref[:, tf:] = wu_ref[...].astype(BF16)
        wdb_ref[...] = wd_ref[...].astype(BF16)

    def ffn_rows(start, n_rows, first, cast=False):
        rows = pl.ds(pl.multiple_of(start, FFN_TILE), n_rows)
        if cast:
            wgu_ref[:, :tf] = wg_ref[...].astype(BF16)
            gate = jnp.dot(x_ref[rows, :], wgu_ref[:, :tf], preferred_element_type=F32)
            wgu_ref[:, tf:] = wu_ref[...].astype(BF16)
            up = jnp.dot(x_ref[rows, :], wgu_ref[:, tf:], preferred_element_type=F32)
            wdb_ref[...] = wd_ref[...].astype(BF16)
        else:
            gu = jnp.dot(x_ref[rows, :], wgu_ref[...], preferred_element_type=F32)
            gate, up = gu[:, :tf], gu[:, tf:]
        act = ((gate * jax.nn.sigmoid(gate)) * up).astype(BF16)
        d = jnp.dot(act, wdb_ref[...], preferred_element_type=F32)
        if first:
            acc_ref[rows, :] = d
        else:
            acc_ref[rows, :] += d

    def wait_previous(i):
        @pl.when(i < prev_tiles)
        def _():
            out_copy(i).wait()

    def run(phase):
        first = phase == "first" and not fused
        last = phase == "last"

        def trip(tile0, tiles, cast=False):
            if first:
                for i in range(tiles):
                    wait_previous(tile0 + i)
            ffn_rows(tile0 * FFN_TILE, tiles * FFN_TILE, first, cast)
            if last:
                for i in range(tiles):
                    out_copy(tile0 + i).start()

        def big_trip(p, carry, cast=False):
            trip(pl.multiple_of(p * big, big), big, cast)
            if last and fused:
                @pl.when(p >= 2)
                def _():
                    a_copy(p - 2).wait()
                rows = pl.ds(pl.multiple_of(p * (big * FFN_TILE), big * FFN_TILE), big * FFN_TILE)
                astage_ref[p % 2] = _rms(acc_ref[rows, :], g_ref[...]).astype(BF16)
                a_copy(p).start()
            return carry
        n_big = n_tiles // big
        merge = (n_tiles - n_big * big == 1) & (n_big > 0) & (not fused)
        n_loop = n_big - jnp.where(merge, 1, 0)

        @pl.when(n_loop > 0)
        def _():
            big_trip(jnp.int32(0), 0, cast=True)

        @pl.when(n_loop == 0)
        def _():
            cast_weights()
        lax.fori_loop(1, n_loop, big_trip, 0)
        if not fused:
            @pl.when(merge)
            def _():
                trip(pl.multiple_of(n_loop * big, big), big + 1)
            done = jnp.where(merge, n_tiles, n_big * big)
            for tiles in FFN_TRIP_TILES[1:]:
                take = ((n_tiles - done) // tiles) > 0

                @pl.when(take)
                def _(done=done, tiles=tiles):
                    trip(pl.multiple_of(done, tiles), tiles)
                done = done + jnp.where(take, tiles, 0)

        def drain(i, carry):
            out_copy(i).wait()
            return carry
        if phase == "first":
            lax.fori_loop(n_tiles, jnp.maximum(prev_tiles, n_tiles), drain, 0)
        if last:
            @pl.when(it == n_items - 1)
            def _():
                lax.fori_loop(0, n_tiles, drain, 0)
            if fused:
                def drain_a(p, carry):
                    a_copy(p).wait()
                    return carry
                lax.fori_loop(jnp.maximum(n_big - 2, 0), n_big, drain_a, 0)

    zero_rows = zero_ref.shape[0]

    def zero_copy(k):
        row = pl.multiple_of(used_rows + k * zero_rows, zero_rows)
        return pltpu.make_async_copy(zero_ref, y_hbm.at[pl.ds(row, zero_rows), :], zsem_ref.at[0])

    @pl.when((it == n_items - 1) & (j == last_j))
    def _():
        zero_ref[...] = jnp.zeros_like(zero_ref)
        pieces = (y_hbm.shape[0] - used_rows) // zero_rows

        def start(k, carry):
            zero_copy(k).start()
            return carry

        def wait(k, carry):
            zero_copy(k).wait()
            return carry
        lax.fori_loop(0, pieces, start, 0)
        lax.fori_loop(0, pieces, wait, 0)

    @pl.when(n_tiles > 0)
    def _():
        @pl.when(j == 0)
        def _():
            @pl.when(it == 0)
            def _():
                x_copy(0).start()
            x_copy(it).wait()

            @pl.when(it + 1 < n_items)
            def _():
                x_copy(it + 1).start()
            if fused:
                def start(i, carry):
                    wait_previous(i)
                    res_copy(i).start()
                    return carry

                def wait(i, carry):
                    res_copy(i).wait()
                    return carry
                lax.fori_loop(0, n_tiles, start, 0)
                lax.fori_loop(0, n_tiles, wait, 0)
            run("first")

        @pl.when((j > 0) & (j < last_j))
        def _():
            run("middle")

        @pl.when(j == last_j)
        def _():
            run("last")


def _grouped_ffn(xs, wg, wu, wd, layer, item_expert, item_tiles, item_row0, meta, chunk,
                 residual=None):
    n_items = item_expert.shape[0]
    out_rows = xs.shape[0]
    nj = D_FF // FFN_TF
    fused = residual is not None

    def col(it, j, e, t, r, m):
        return jnp.where(t[it] > 0, j, nj - 1)

    in_specs = [
        pl.BlockSpec(memory_space=pl.ANY),
        pl.BlockSpec((None, None, D_MODEL, FFN_TF),
                     lambda it, j, e, t, r, m: (layer, e[it], 0, col(it, j, e, t, r, m))),
        pl.BlockSpec((None, None, D_MODEL, FFN_TF),
                     lambda it, j, e, t, r, m: (layer, e[it], 0, col(it, j, e, t, r, m))),
        pl.BlockSpec((None, None, FFN_TF, D_MODEL),
                     lambda it, j, e, t, r, m: (layer, e[it], col(it, j, e, t, r, m), 0)),
    ]
    args = [item_expert, item_tiles, item_row0, meta, xs, wg, wu, wd]
    out_specs = pl.BlockSpec(memory_space=pl.ANY)
    out_shape = jax.ShapeDtypeStruct((out_rows, D_MODEL), F32)
    scratch = [pltpu.VMEM((2, chunk, D_MODEL), BF16),
               pltpu.VMEM((D_MODEL, 2 * FFN_TF), BF16),
               pltpu.VMEM((FFN_TF, D_MODEL), BF16),
               pltpu.VMEM((chunk, D_MODEL), F32),
               pltpu.VMEM((FFN_ZERO_ROWS, D_MODEL), F32),
               pltpu.SemaphoreType.DMA((2,)),
               pltpu.SemaphoreType.DMA((chunk // FFN_TILE,)),
               pltpu.SemaphoreType.DMA((1,))]
    if fused:
        res, g3, g_layer = residual
        in_specs += [pl.BlockSpec(memory_space=pl.ANY),
                     pl.BlockSpec((None, 1, D_MODEL),
                                  lambda it, j, e, t, r, m: (g_layer, 0, 0))]
        args += [res, g3]
        out_specs = [out_specs, pl.BlockSpec(memory_space=pl.ANY)]
        out_shape = [out_shape, jax.ShapeDtypeStruct((out_rows, D_MODEL), BF16)]
        scratch += [pltpu.VMEM((2, FFN_TRIP_TILES[0] * FFN_TILE, D_MODEL), BF16),
                    pltpu.SemaphoreType.DMA((2,)),
                    pltpu.SemaphoreType.DMA((1,))]
    return pl.pallas_call(
        functools.partial(_ffn_kernel, fused=fused),
        grid_spec=pltpu.PrefetchScalarGridSpec(
            num_scalar_prefetch=4,
            grid=(n_items, nj),
            in_specs=in_specs,
            out_specs=out_specs,
            scratch_shapes=scratch),
        out_shape=out_shape,
        compiler_params=_params(2),
        name="grouped_ffn",
    )(*args)


def _dense_ffn(f, wg, wu, wd, layer, h, g3, g_layer):
    n_items = N_TOK // DENSE_CHUNK
    assert DENSE_CHUNK % (FFN_TRIP_TILES[0] * FFN_TILE) == 0
    return _grouped_ffn(
        f, wg[:, None], wu[:, None], wd[:, None], layer,
        jnp.zeros((n_items,), I32), jnp.full((n_items,), DENSE_CHUNK // FFN_TILE, I32),
        jnp.arange(n_items, dtype=I32) * DENSE_CHUNK, jnp.array([n_items, N_TOK], I32),
        DENSE_CHUNK, residual=(h, g3, g_layer))


def _combine_kernel(pos_ref, h_ref, gate_ref, g_ref, y_hbm, *rest, final):
    if final:
        o_ref, buf_ref, sem_ref = rest
    else:
        hn_ref, a_ref, buf_ref, sem_ref = rest
    t = pl.program_id(0)
    last = pl.num_programs(0) - 1
    tm = h_ref.shape[0]

    def issue(tile, slot):
        def body(r, carry):
            for k in range(TOP_K):
                _row_copy(y_hbm, pos_ref[k * N_TOK + tile * tm + r],
                          buf_ref, slot, k * tm + r, sem_ref).start()
            return carry
        lax.fori_loop(0, tm, body, 0, unroll=DMA_ISSUE_UNROLL)

    @pl.when(t == 0)
    def _():
        issue(0, 0)

    @pl.when(t < last)
    def _():
        issue(t + 1, (t + 1) % 2)

    slot = t % 2
    for r in range(TOP_K * tm):
        _row_copy(y_hbm, 0, buf_ref, slot, r, sem_ref).wait()
    hn = (h_ref[...] + gate_ref[:, 0:1] * buf_ref[slot, pl.ds(0, tm), :]
          + gate_ref[:, 1:2] * buf_ref[slot, pl.ds(tm, tm), :])
    if final:
        o_ref[...] = _rms(hn, g_ref[...])
    else:
        hn_ref[...] = hn
        a_ref[...] = _rms(hn, g_ref[...]).astype(a_ref.dtype)


def _combine(h, y, pos, gates_t, g3, layer, final, tm=256):
    row = pl.BlockSpec((tm, D_MODEL), lambda t, p: (t, 0))
    if final:
        out_specs = row
        out_shape = jax.ShapeDtypeStruct((N_TOK, D_MODEL), F32)
    else:
        out_specs = [row, row]
        out_shape = [jax.ShapeDtypeStruct((N_TOK, D_MODEL), F32),
                     jax.ShapeDtypeStruct((N_TOK, D_MODEL), BF16)]
    return pl.pallas_call(
        functools.partial(_combine_kernel, final=final),
        grid_spec=pltpu.PrefetchScalarGridSpec(
            num_scalar_prefetch=1,
            grid=(N_TOK // tm,),
            in_specs=[row,
                      pl.BlockSpec((tm, TOP_K), lambda t, p: (t, 0)),
                      pl.BlockSpec((None, 1, D_MODEL), lambda t, p: (layer, 0, 0)),
                      pl.BlockSpec(memory_space=pl.ANY)],
            out_specs=out_specs,
            scratch_shapes=[pltpu.VMEM((2, TOP_K * tm, D_MODEL), F32),
                            pltpu.SemaphoreType.DMA((2,))]),
        out_shape=out_shape,
        compiler_params=_params(1),
        name="moe_combine",
    )(pos.reshape(-1), h, gates_t, g3, y)


def _alibi_slopes():
    n = N_DIL * N_HEADS
    s = jnp.exp2(-8.0 * jnp.arange(1, n + 1, dtype=F32) / n)
    return s.reshape(N_HEADS, N_DIL).T


def kernel(x, mix_norm_g, ffn_norm_g, attn_w_in, attn_w_out, gm_w_in, gm_v_norm_g, gm_w_s,
           gm_b_s, gm_w_out, dense_w_gate, dense_w_up, dense_w_down, router_w, moe_w_gate,
           moe_w_up, moe_w_down, final_norm_g):
    h = x.reshape(N_TOK, D_MODEL)
    mix_g = mix_norm_g.reshape(DEPTH, 1, D_MODEL)
    ffn_g = ffn_norm_g.reshape(DEPTH, 1, D_MODEL)
    final_g = final_norm_g.reshape(1, 1, D_MODEL)
    gm_vg = gm_v_norm_g.reshape(-1, 1, GM_WIDTH)
    gm_bt = jnp.swapaxes(gm_b_s, 1, 2)
    router_wt = jnp.swapaxes(router_w, 1, 2)
    slopes = _alibi_slopes()

    a = _rmsnorm(h, mix_g, 0, BF16)
    for i in range(DEPTH):
        j = i // 2
        if i % 2 == 0:
            groups = [_mm_qkv(a, attn_w_in, j, dil, g * (QKV_GROUP_COLS // 1024))
                      for g, (win, dil) in enumerate(DIL_CONFIGS)]
            h, f = _proj_norm(_attention(groups, slopes), attn_w_out, j, h, ffn_g, i)
            h, a = _dense_ffn(f, dense_w_gate, dense_w_up, dense_w_down, j, h, mix_g, i + 1)
        else:
            z = _mm_act(a, gm_w_in, j, 2 * GM_WIDTH, "gelu")
            y = _gm_spatial(z, gm_vg, gm_w_s, gm_bt, j)
            h = _proj(y, gm_w_out, j, h)
            idx, gates, rank, cnt = _router(h, ffn_g, i, router_wt, j)
            pos, src, tile_valid, item_expert, item_tiles, item_row0, meta = _dispatch_plan(
                idx, rank, cnt)
            xs = _dispatch(h, src, tile_valid, ffn_g, i)
            y = _grouped_ffn(xs, moe_w_gate, moe_w_up, moe_w_down, j,
                             item_expert, item_tiles, item_row0, meta, FFN_CHUNK)
            if i == DEPTH - 1:
                return _combine(h, y, pos, gates.T, final_g, 0, True).reshape(BATCH, SEQ, D_MODEL)
            h, a = _combine(h, y, pos, gates.T, mix_g, i + 1, False)
```

```python
import functools

import jax
import jax.numpy as jnp
from jax import lax
from jax.experimental import pallas as pl
from jax.experimental.pallas import tpu as pltpu

D_MODEL = 2048
BATCH = 2
SEQ = 4096
DEPTH = 4
N_TOK = BATCH * SEQ
HEAD_DIM = 128
N_HEADS = D_MODEL // HEAD_DIM
DIL_CONFIGS = ((128, 1), (512, 4), (2048, 16))
N_DIL = len(DIL_CONFIGS)
ATT_BLK = 128
ATT_SKEW = 1
QKV_GROUP_COLS = 3 * N_HEADS * HEAD_DIM
GM_CHUNK = 128
GM_WIDTH = D_MODEL
GM_GROUP_DIM = 128
GM_GROUPS = GM_WIDTH // GM_GROUP_DIM
D_FF = 7 * D_MODEL // 2
N_EXPERTS = 8
TOP_K = 2
EPS = 1e-6
NEG_INF = -1e30

LANES = 128
VMEM_LIMIT = 60 * 1024 * 1024
CAST_ROWS = 256

FFN_TILE = 128
FFN_TRIP_TILES = (8, 4, 2, 1)
DISPATCH_TILE = 256
FFN_CHUNK = 2304
FFN_TILES_PER_CHUNK = FFN_CHUNK // FFN_TILE
DENSE_CHUNK = 2048
FFN_TF = 256
DMA_ISSUE_UNROLL = 8
FFN_ZERO_ROWS = 64
MOE_MAX_ITEMS = -(-TOP_K * N_TOK // FFN_CHUNK) + N_EXPERTS
MOE_ROWS = TOP_K * N_TOK + N_EXPERTS * FFN_TILE + FFN_CHUNK
assert MOE_ROWS % DISPATCH_TILE == 0

F32 = jnp.float32
BF16 = jnp.bfloat16
I32 = jnp.int32


def _params(n_axes):
    return pltpu.CompilerParams(
        dimension_semantics=("arbitrary",) * n_axes, vmem_limit_bytes=VMEM_LIMIT)


def _cast_weight(w_ref, wb_ref):
    def body(c, carry):
        r = pl.multiple_of(c * CAST_ROWS, CAST_ROWS)
        wb_ref[pl.ds(r, CAST_ROWS), :] = w_ref[pl.ds(r, CAST_ROWS), :].astype(BF16)
        return carry
    lax.fori_loop(0, w_ref.shape[0] // CAST_ROWS, body, 0)


def _gelu_tanh(x):
    return 0.5 * x * (1.0 + jnp.tanh(0.7978845608028654 * (x + 0.044715 * (x * x * x))))


def _rms(x, g):
    ms = jnp.mean(x * x, axis=-1, keepdims=True)
    return (x * lax.rsqrt(ms + EPS)) * g


def _rms_kernel(x_ref, g_ref, o_ref):
    o_ref[...] = _rms(x_ref[...], g_ref[...]).astype(o_ref.dtype)


def _rmsnorm(h, g3, layer, out_dtype, tm=512):
    return pl.pallas_call(
        _rms_kernel,
        grid=(N_TOK // tm,),
        in_specs=[pl.BlockSpec((tm, D_MODEL), lambda i: (i, 0)),
                  pl.BlockSpec((None, 1, D_MODEL), lambda i: (layer, 0, 0))],
        out_specs=pl.BlockSpec((tm, D_MODEL), lambda i: (i, 0)),
        out_shape=jax.ShapeDtypeStruct((N_TOK, D_MODEL), out_dtype),
        compiler_params=_params(1),
        name="rmsnorm",
    )(h, g3)


def _w_spec(layer, k, tn, col_off):
    return pl.BlockSpec((None, k, tn), lambda j, i: (layer, 0, col_off + j))


def _mm_act_kernel(a_ref, w_ref, o_ref, wb_ref, *, act):
    @pl.when(pl.program_id(1) == 0)
    def _():
        _cast_weight(w_ref, wb_ref)
    acc = jnp.dot(a_ref[...], wb_ref[...], preferred_element_type=F32)
    if act == "gelu":
        acc = _gelu_tanh(acc)
    o_ref[...] = acc.astype(o_ref.dtype)


def _mm_act(a, w, layer, n_cols, act, tm=1024, tn=1024, col_off=0):
    k = a.shape[1]
    return pl.pallas_call(
        functools.partial(_mm_act_kernel, act=act),
        grid=(n_cols // tn, N_TOK // tm),
        in_specs=[pl.BlockSpec((tm, k), lambda j, i: (i, 0)),
                  _w_spec(layer, k, tn, col_off)],
        out_specs=pl.BlockSpec((tm, tn), lambda j, i: (i, j)),
        out_shape=jax.ShapeDtypeStruct((N_TOK, n_cols), BF16),
        scratch_shapes=[pltpu.VMEM((k, tn), BF16)],
        compiler_params=_params(2),
        name="mm_act",
    )(a, w)


def _mm_qkv_kernel(a_ref, w_ref, o_ref, wb_ref, *rest, dil):
    @pl.when(pl.program_id(1) == 0)
    def _():
        _cast_weight(w_ref, wb_ref)
    acc = jnp.dot(a_ref[...], wb_ref[...], preferred_element_type=F32)
    heads = o_ref.shape[0]
    if dil == 1:
        for hh in range(heads):
            o_ref[hh, 0] = acc[:, hh * HEAD_DIM:(hh + 1) * HEAD_DIM].astype(o_ref.dtype)
    else:
        acc_ref, = rest
        rows = acc_ref.shape[1] // dil
        for hh in range(heads):
            acc_ref[hh] = acc[:, hh * HEAD_DIM:(hh + 1) * HEAD_DIM]

        def regroup(hh, carry):
            for r in range(dil):
                o_ref[hh, r] = acc_ref[hh, pl.ds(r, rows, stride=dil), :].astype(o_ref.dtype)
            return carry
        lax.fori_loop(0, heads, regroup, 0, unroll=dil <= 4)


def _mm_qkv(a, w, layer, dil, col_off, tm=1024, tn=1024):
    k = a.shape[1]
    tiles_per_b = SEQ // tm
    heads = tn // HEAD_DIM
    scratch = [pltpu.VMEM((k, tn), BF16)]
    if dil > 1:
        scratch.append(pltpu.VMEM((heads, tm, HEAD_DIM), F32))
    out = pl.pallas_call(
        functools.partial(_mm_qkv_kernel, dil=dil),
        grid=(QKV_GROUP_COLS // tn, N_TOK // tm),
        in_specs=[pl.BlockSpec((tm, k), lambda j, i: (i, 0)),
                  _w_spec(layer, k, tn, col_off)],
        out_specs=pl.BlockSpec((None, heads, dil, tm // dil, HEAD_DIM),
                               lambda j, i: (i // tiles_per_b, j, 0, i % tiles_per_b, 0)),
        out_shape=jax.ShapeDtypeStruct(
            (BATCH, 3 * N_HEADS, dil, SEQ // dil, HEAD_DIM), BF16),
        scratch_shapes=scratch,
        compiler_params=_params(2),
        name="mm_qkv",
    )(a, w)
    return out.reshape(BATCH, 3 * N_HEADS, SEQ, HEAD_DIM)


def _proj_kernel(a_ref, w_ref, res_ref, o_ref, wb_ref):
    @pl.when(pl.program_id(1) == 0)
    def _():
        _cast_weight(w_ref, wb_ref)
    o_ref[...] = res_ref[...] + jnp.dot(a_ref[...], wb_ref[...], preferred_element_type=F32)


def _proj(a, w, layer, res, tm=512, tn=1024):
    k = a.shape[1]
    return pl.pallas_call(
        _proj_kernel,
        grid=(D_MODEL // tn, N_TOK // tm),
        in_specs=[pl.BlockSpec((tm, k), lambda j, i: (i, 0)),
                  _w_spec(layer, k, tn, 0),
                  pl.BlockSpec((tm, tn), lambda j, i: (i, j))],
        out_specs=pl.BlockSpec((tm, tn), lambda j, i: (i, j)),
        out_shape=jax.ShapeDtypeStruct((N_TOK, D_MODEL), F32),
        scratch_shapes=[pltpu.VMEM((k, tn), BF16)],
        compiler_params=_params(2),
        name="proj",
    )(a, w, res)


def _proj_norm_kernel(a_ref, w_ref, res_ref, g_ref, o_ref, f_ref, wb_ref):
    @pl.when(pl.program_id(1) == 0)
    def _():
        _cast_weight(w_ref, wb_ref)
    a = jnp.concatenate([a_ref[hh] for hh in range(a_ref.shape[0])], axis=1)
    out = res_ref[...] + jnp.dot(a, wb_ref[...], preferred_element_type=F32)
    o_ref[...] = out
    f_ref[...] = _rms(out, g_ref[...]).astype(f_ref.dtype)


def _proj_norm(a, w, layer, res, g3, g_layer, tm=256):
    tiles_per_b = SEQ // tm
    k = a.shape[1] * a.shape[3]
    row = pl.BlockSpec((tm, D_MODEL), lambda j, i: (i, 0))
    return pl.pallas_call(
        _proj_norm_kernel,
        grid=(1, N_TOK // tm),
        in_specs=[pl.BlockSpec((None, a.shape[1], tm, a.shape[3]),
                               lambda j, i: (i // tiles_per_b, 0, i % tiles_per_b, 0)),
                  _w_spec(layer, k, D_MODEL, 0),
                  row,
                  pl.BlockSpec((None, 1, D_MODEL), lambda j, i: (g_layer, 0, 0))],
        out_specs=[row, row],
        out_shape=[jax.ShapeDtypeStruct((N_TOK, D_MODEL), F32),
                   jax.ShapeDtypeStruct((N_TOK, D_MODEL), BF16)],
        scratch_shapes=[pltpu.VMEM((k, D_MODEL), BF16)],
        compiler_params=_params(2),
        name="proj_norm",
    )(a, w, res, g3)


def _attn_kernel(slopes_ref, *refs):
    qkv_refs = refs[:9]
    o_ref, tab_ref, oscr_ref, lscr_ref = refs[9:]
    h = pl.program_id(1)
    ii = lax.broadcasted_iota(I32, (ATT_BLK, 2 * ATT_BLK), 0)
    jj = lax.broadcasted_iota(I32, (ATT_BLK, 2 * ATT_BLK), 1)
    delta = ii + ATT_BLK - jj
    scale = HEAD_DIM ** -0.5
    nt = (((1,), (1,)), ((), ()))
    n_blocks = SEQ // ATT_BLK
    ones = jnp.ones((2 * ATT_BLK, HEAD_DIM), BF16)
    for g, (win, dil) in enumerate(DIL_CONFIGS):
        q_ref, k_ref, v_ref = qkv_refs[3 * g:3 * g + 3]
        nb = n_blocks // dil
        tab_ref[...] = jnp.where((delta >= 0) & (delta <= ATT_BLK),
                                 -slopes_ref[g, h] * (dil * delta).astype(F32), NEG_INF)
        s, m, p, done = {}, {}, {}, []

        def keys(c, nb=nb):
            first = c % nb == 0
            return (slice((c - (not first)) * ATT_BLK, (c + 1) * ATT_BLK),
                    slice(ATT_BLK if first else 0, 2 * ATT_BLK))

        def scores(c):
            rows, cols = keys(c)
            q = q_ref[c * ATT_BLK:(c + 1) * ATT_BLK, :]
            s[c] = lax.dot_general(q, k_ref[rows, :], nt,
                                   preferred_element_type=F32) * scale + tab_ref[:, cols]

        def softmax(c):
            m[c] = jnp.max(s[c], axis=-1, keepdims=True)
            p[c] = jnp.exp(s.pop(c) - m[c]).astype(BF16)

        def values(c):
            rows, _ = keys(c)
            vv = jnp.concatenate([v_ref[rows, :], ones[:rows.stop - rows.start]], axis=1)
            od = jnp.dot(p.pop(c), vv, preferred_element_type=F32)
            den = od[:, HEAD_DIM:]
            done.append((c, od[:, :HEAD_DIM] / den, m.pop(c) + jnp.log(den)))

        for step in range(n_blocks + 2 * ATT_SKEW):
            if step < n_blocks:
                scores(step)
            if 0 <= step - ATT_SKEW < n_blocks:
                softmax(step - ATT_SKEW)
            if 0 <= step - 2 * ATT_SKEW < n_blocks:
                values(step - 2 * ATT_SKEW)
        for c, o, lse in done:
            start = (c % nb) * (ATT_BLK * dil) + c // nb
            dst = pl.ds(start, ATT_BLK) if dil == 1 else pl.ds(start, ATT_BLK, stride=dil)
            oscr_ref[g, dst, :] = o
            lscr_ref[g, dst, :] = jnp.broadcast_to(lse, (ATT_BLK, HEAD_DIM))

    rows = 256

    def merge(t, carry):
        sl = pl.ds(pl.multiple_of(t * rows, rows), rows)
        l0 = lscr_ref[0, sl, :]
        l1 = lscr_ref[1, sl, :]
        l2 = lscr_ref[2, sl, :]
        m = jnp.maximum(jnp.maximum(l0, l1), l2)
        e0 = jnp.exp(l0 - m)
        e1 = jnp.exp(l1 - m)
        e2 = jnp.exp(l2 - m)
        den = e0 + e1 + e2
        o = (e0 * oscr_ref[0, sl, :] + e1 * oscr_ref[1, sl, :] + e2 * oscr_ref[2, sl, :]) / den
        o_ref[sl, :] = o.astype(o_ref.dtype)
        return carry

    lax.fori_loop(0, SEQ // rows, merge, 0)


def _attention(qkv_groups, slopes):
    in_specs = [pl.BlockSpec(memory_space=pltpu.SMEM)]
    args = [slopes]
    for qkv in qkv_groups:
        for part in range(3):
            in_specs.append(pl.BlockSpec(
                (None, None, SEQ, HEAD_DIM),
                lambda b, h, part=part: (b, part * N_HEADS + h, 0, 0)))
            args.append(qkv)
    return pl.pallas_call(
        _attn_kernel,
        grid=(BATCH, N_HEADS),
        in_specs=in_specs,
        out_specs=pl.BlockSpec((None, None, SEQ, HEAD_DIM), lambda b, h: (b, h, 0, 0)),
        out_shape=jax.ShapeDtypeStruct((BATCH, N_HEADS, SEQ, HEAD_DIM), BF16),
        scratch_shapes=[pltpu.VMEM((ATT_BLK, 2 * ATT_BLK), F32),
                        pltpu.VMEM((N_DIL, SEQ, HEAD_DIM), F32),
                        pltpu.VMEM((N_DIL, SEQ, HEAD_DIM), F32)],
        compiler_params=_params(2),
        name="dilated_attn",
    )(*args)


def _gm_spatial_kernel(z_ref, vg_ref, ws_ref, bt_ref, y_ref, wsb_ref):
    @pl.when(pl.program_id(0) == 0)
    def _():
        ii = lax.broadcasted_iota(I32, (GM_CHUNK, GM_CHUNK), 0)
        jj = lax.broadcasted_iota(I32, (GM_CHUNK, GM_CHUNK), 1)
        for g in range(GM_GROUPS):
            wsb_ref[g] = jnp.where(ii >= jj, ws_ref[g], 0.0).astype(BF16)

    for c in range(z_ref.shape[0] // GM_CHUNK):
        rows = slice(c * GM_CHUNK, (c + 1) * GM_CHUNK)
        vn = _rms(z_ref[rows, GM_WIDTH:].astype(F32), vg_ref[...]).astype(BF16)
        for g in range(GM_GROUPS):
            cols = slice(g * GM_GROUP_DIM, (g + 1) * GM_GROUP_DIM)
            s = jnp.dot(wsb_ref[g], vn[:, cols], preferred_element_type=F32) + bt_ref[:, g:g + 1]
            y_ref[rows, cols] = (z_ref[rows, cols].astype(F32) * s).astype(y_ref.dtype)


def _gm_spatial(z, vg3, ws, bt, layer, tm=256):
    return pl.pallas_call(
        _gm_spatial_kernel,
        grid=(N_TOK // tm,),
        in_specs=[pl.BlockSpec((tm, 2 * GM_WIDTH), lambda i: (i, 0)),
                  pl.BlockSpec((None, 1, GM_WIDTH), lambda i: (layer, 0, 0)),
                  pl.BlockSpec((None, GM_GROUPS, GM_CHUNK, GM_CHUNK), lambda i: (layer, 0, 0, 0)),
                  pl.BlockSpec((None, GM_CHUNK, GM_GROUPS), lambda i: (layer, 0, 0))],
        out_specs=pl.BlockSpec((tm, GM_WIDTH), lambda i: (i, 0)),
        out_shape=jax.ShapeDtypeStruct((N_TOK, GM_WIDTH), BF16),
        scratch_shapes=[pltpu.VMEM((GM_GROUPS, GM_CHUNK, GM_CHUNK), BF16)],
        compiler_params=_params(1),
        name="gm_spatial",
    )(z, vg3, ws, bt)


def _router_kernel(h_ref, g_ref, rwt_ref, idx_ref, gate_ref, rank_ref, cnt_ref, run_ref):
    @pl.when(pl.program_id(0) == 0)
    def _():
        run_ref[...] = jnp.zeros_like(run_ref)

    f = _rms(h_ref[...], g_ref[...])
    logits = lax.dot_general(rwt_ref[...], f, (((1,), (1,)), ((), ())),
                             precision=lax.Precision.HIGHEST, preferred_element_type=F32)
    tm = logits.shape[1]
    eid = lax.broadcasted_iota(I32, logits.shape, 0)
    m1 = jnp.max(logits, axis=0, keepdims=True)
    i1 = jnp.min(jnp.where(logits == m1, eid, N_EXPERTS), axis=0, keepdims=True)
    rest = jnp.where(eid == i1, -jnp.inf, logits)
    m2 = jnp.max(rest, axis=0, keepdims=True)
    i2 = jnp.min(jnp.where(rest == m2, eid, N_EXPERTS), axis=0, keepdims=True)
    e2 = jnp.exp(m2 - m1)
    den = 1.0 + e2
    idx_ref[0:1, :] = i1
    idx_ref[1:2, :] = i2
    gate_ref[0:1, :] = 1.0 / den
    gate_ref[1:2, :] = e2 / den

    sel1 = eid == i1
    sel2 = eid == i2
    onehot = jnp.where(sel1, 1.0, jnp.where(sel2, 1.0, 0.0))
    earlier = (lax.broadcasted_iota(I32, (tm, tm), 0)
               < lax.broadcasted_iota(I32, (tm, tm), 1))
    before = jnp.dot(onehot.astype(BF16), jnp.where(earlier, 1.0, 0.0).astype(BF16),
                     preferred_element_type=F32) + run_ref[:, 0:1]
    rank_ref[0:1, :] = jnp.sum(jnp.where(sel1, before, 0.0), axis=0, keepdims=True).astype(I32)
    rank_ref[1:2, :] = jnp.sum(jnp.where(sel2, before, 0.0), axis=0, keepdims=True).astype(I32)
    run_ref[...] = run_ref[...] + jnp.sum(onehot, axis=1, keepdims=True)
    cnt_ref[...] = run_ref[...]


def _router(h, g3, glayer, rwt, mlayer, tm=512):
    pair = pl.BlockSpec((TOP_K, tm), lambda i: (0, i))
    return pl.pallas_call(
        _router_kernel,
        grid=(N_TOK // tm,),
        in_specs=[pl.BlockSpec((tm, D_MODEL), lambda i: (i, 0)),
                  pl.BlockSpec((None, 1, D_MODEL), lambda i: (glayer, 0, 0)),
                  pl.BlockSpec((None, N_EXPERTS, D_MODEL), lambda i: (mlayer, 0, 0))],
        out_specs=[pair, pair, pair, pl.BlockSpec((N_EXPERTS, LANES), lambda i: (0, 0))],
        out_shape=[jax.ShapeDtypeStruct((TOP_K, N_TOK), I32),
                   jax.ShapeDtypeStruct((TOP_K, N_TOK), F32),
                   jax.ShapeDtypeStruct((TOP_K, N_TOK), I32),
                   jax.ShapeDtypeStruct((N_EXPERTS, LANES), F32)],
        scratch_shapes=[pltpu.VMEM((N_EXPERTS, LANES), F32)],
        compiler_params=_params(1),
        name="router",
    )(h, g3, rwt)


def _dispatch_plan(idx, rank, cnt):
    counts = cnt[:, 0].astype(I32)
    tiles_e = (counts + FFN_TILE - 1) // FFN_TILE
    items_e = (tiles_e + FFN_TILES_PER_CHUNK - 1) // FFN_TILES_PER_CHUNK
    items_end = jnp.cumsum(items_e)
    items_start = items_end - items_e
    n_items = items_end[-1]
    first_row = (jnp.cumsum(tiles_e) - tiles_e) * FFN_TILE
    used_rows = jnp.sum(tiles_e) * FFN_TILE
    pos = rank
    for e in range(N_EXPERTS):
        pos = pos + jnp.where(idx == e, first_row[e], 0)
    it = jnp.arange(MOE_MAX_ITEMS, dtype=I32)
    it_c = jnp.clip(it, 0, jnp.maximum(n_items - 1, 0))
    item_expert = jnp.minimum(jnp.searchsorted(items_end, it_c, side="right"),
                              N_EXPERTS - 1).astype(I32)
    local = it_c - items_start[item_expert]
    item_tiles = jnp.clip(tiles_e[item_expert] - local * FFN_TILES_PER_CHUNK,
                          0, FFN_TILES_PER_CHUNK)
    item_tiles = jnp.where(it < n_items, item_tiles, 0).astype(I32)
    item_row0 = (first_row[item_expert] + local * FFN_CHUNK).astype(I32)
    tok = jnp.broadcast_to(jnp.arange(N_TOK, dtype=I32), (TOP_K, N_TOK))
    src = jnp.zeros((MOE_ROWS,), I32).at[pos.reshape(-1)].set(tok.reshape(-1))
    tile_valid = (jnp.arange(MOE_ROWS // DISPATCH_TILE, dtype=I32) * DISPATCH_TILE
                  < used_rows).astype(I32)
    meta = jnp.stack([n_items, used_rows]).astype(I32)
    return pos, src, tile_valid, item_expert, item_tiles, item_row0, meta


def _row_copy(src_hbm, row, dst_ref, slot, r, sem_ref):
    return pltpu.make_async_copy(src_hbm.at[pl.ds(row, 1), :],
                                 dst_ref.at[slot, pl.ds(r, 1), :], sem_ref.at[slot])


def _dispatch_kernel(src_ref, valid_ref, h_hbm, g_ref, o_ref, buf_ref, sem_ref):
    p = pl.program_id(0)
    last = pl.num_programs(0) - 1

    def issue(tile, slot):
        def body(r, carry):
            _row_copy(h_hbm, src_ref[tile * DISPATCH_TILE + r], buf_ref, slot, r, sem_ref).start()
            return carry
        lax.fori_loop(0, DISPATCH_TILE, body, 0, unroll=DMA_ISSUE_UNROLL)

    @pl.when((p == 0) & (valid_ref[0] > 0))
    def _():
        issue(0, 0)

    nxt = jnp.minimum(p + 1, last)

    @pl.when((p < last) & (valid_ref[nxt] > 0))
    def _():
        issue(nxt, nxt % 2)

    slot = p % 2

    @pl.when(valid_ref[p] > 0)
    def _():
        for r in range(DISPATCH_TILE):
            _row_copy(h_hbm, 0, buf_ref, slot, r, sem_ref).wait()
        o_ref[...] = _rms(buf_ref[slot], g_ref[...]).astype(o_ref.dtype)

    @pl.when(valid_ref[p] == 0)
    def _():
        o_ref[...] = jnp.zeros_like(o_ref)


def _dispatch(h, src, tile_valid, g3, layer):
    return pl.pallas_call(
        _dispatch_kernel,
        grid_spec=pltpu.PrefetchScalarGridSpec(
            num_scalar_prefetch=2,
            grid=(MOE_ROWS // DISPATCH_TILE,),
            in_specs=[pl.BlockSpec(memory_space=pl.ANY),
                      pl.BlockSpec((None, 1, D_MODEL), lambda p, s, v: (layer, 0, 0))],
            out_specs=pl.BlockSpec((DISPATCH_TILE, D_MODEL), lambda p, s, v: (p, 0)),
            scratch_shapes=[pltpu.VMEM((2, DISPATCH_TILE, D_MODEL), F32),
                            pltpu.SemaphoreType.DMA((2,))]),
        out_shape=jax.ShapeDtypeStruct((MOE_ROWS, D_MODEL), BF16),
        compiler_params=_params(1),
        name="moe_dispatch",
    )(src, tile_valid, h, g3)


def _ffn_kernel(exp_ref, tiles_ref, row0_ref, meta_ref, x_hbm, wg_ref, wu_ref, wd_ref, *rest,
                fused):
    if fused:
        (res_hbm, g_ref, y_hbm, a_hbm, xbuf_ref, wgu_ref, wdb_ref, acc_ref, zero_ref, xsem_ref,
         sem_ref, zsem_ref, astage_ref, asem_ref, rsem_ref) = rest
    else:
        y_hbm, xbuf_ref, wgu_ref, wdb_ref, acc_ref, zero_ref, xsem_ref, sem_ref, zsem_ref = rest
    it = pl.program_id(0)
    j = pl.program_id(1)
    last_j = pl.num_programs(1) - 1
    n_tiles = tiles_ref[it]
    n_items = meta_ref[0]
    used_rows = meta_ref[1]
    tf = wg_ref.shape[1]
    chunk = xbuf_ref.shape[1]
    big = FFN_TRIP_TILES[0]
    row0 = row0_ref[it]
    prev_tiles = jnp.where(it > 0, tiles_ref[jnp.maximum(it - 1, 0)], 0)
    x_ref = xbuf_ref.at[it % 2]

    def x_copy(item):
        return pltpu.make_async_copy(
            x_hbm.at[pl.ds(pl.multiple_of(row0_ref[item], FFN_TILE), chunk), :],
            xbuf_ref.at[item % 2], xsem_ref.at[item % 2])

    def out_copy(i):
        r = pl.multiple_of(i * FFN_TILE, FFN_TILE)
        return pltpu.make_async_copy(
            acc_ref.at[pl.ds(r, FFN_TILE), :],
            y_hbm.at[pl.ds(pl.multiple_of(row0 + r, FFN_TILE), FFN_TILE), :], sem_ref.at[i])

    def res_copy(i):
        r = pl.multiple_of(i * FFN_TILE, FFN_TILE)
        return pltpu.make_async_copy(
            res_hbm.at[pl.ds(pl.multiple_of(row0 + r, FFN_TILE), FFN_TILE), :],
            acc_ref.at[pl.ds(r, FFN_TILE), :], rsem_ref.at[0])

    def a_copy(p):
        r = pl.multiple_of(row0 + p * (big * FFN_TILE), FFN_TILE)
        return pltpu.make_async_copy(astage_ref.at[p % 2],
                                     a_hbm.at[pl.ds(r, big * FFN_TILE), :], asem_ref.at[p % 2])

    def cast_weights():
        wgu_ref[:, :tf] = wg_ref[...].astype(BF16)
        wgu_ref[:, tf:] = wu_ref[...].astype(BF16)
        wdb_ref[...] = wd_ref[...].astype(BF16)

    def ffn_rows(start, n_rows, first, cast=False):
        rows = pl.ds(pl.multiple_of(start, FFN_TILE), n_rows)
        if cast:
            wgu_ref[:, :tf] = wg_ref[...].astype(BF16)
            gate = jnp.dot(x_ref[rows, :], wgu_ref[:, :tf], preferred_element_type=F32)
            wgu_ref[:, tf:] = wu_ref[...].astype(BF16)
            up = jnp.dot(x_ref[rows, :], wgu_ref[:, tf:], preferred_element_type=F32)
            wdb_ref[...] = wd_ref[...].astype(BF16)
        else:
            gu = jnp.dot(x_ref[rows, :], wgu_ref[...], preferred_element_type=F32)
            gate, up = gu[:, :tf], gu[:, tf:]
        act = ((gate * jax.nn.sigmoid(gate)) * up).astype(BF16)
        d = jnp.dot(act, wdb_ref[...], preferred_element_type=F32)
        if first:
            acc_ref[rows, :] = d
        else:
            acc_ref[rows, :] += d

    def wait_previous(i):
        @pl.when(i < prev_tiles)
        def _():
            out_copy(i).wait()

    def run(phase):
        first = phase == "first" and not fused
        last = phase == "last"

        def trip(tile0, tiles, cast=False):
            if first:
                for i in range(tiles):
                    wait_previous(tile0 + i)
            ffn_rows(tile0 * FFN_TILE, tiles * FFN_TILE, first, cast)
            if last:
                for i in range(tiles):
                    out_copy(tile0 + i).start()

        def big_trip(p, carry, cast=False):
            trip(pl.multiple_of(p * big, big), big, cast)
            if last and fused:
                @pl.when(p >= 2)
                def _():
                    a_copy(p - 2).wait()
                rows = pl.ds(pl.multiple_of(p * (big * FFN_TILE), big * FFN_TILE), big * FFN_TILE)
                astage_ref[p % 2] = _rms(acc_ref[rows, :], g_ref[...]).astype(BF16)
                a_copy(p).start()
            return carry
        n_big = n_tiles // big
        merge = (n_tiles - n_big * big == 1) & (n_big > 0) & (not fused)
        n_loop = n_big - jnp.where(merge, 1, 0)

        @pl.when(n_loop > 0)
        def _():
            big_trip(jnp.int32(0), 0, cast=True)

        @pl.when(n_loop == 0)
        def _():
            cast_weights()
        lax.fori_loop(1, n_loop, big_trip, 0)
        if not fused:
            @pl.when(merge)
            def _():
                trip(pl.multiple_of(n_loop * big, big), big + 1)
            done = jnp.where(merge, n_tiles, n_big * big)
            for tiles in FFN_TRIP_TILES[1:]:
                take = ((n_tiles - done) // tiles) > 0

                @pl.when(take)
                def _(done=done, tiles=tiles):
                    trip(pl.multiple_of(done, tiles), tiles)
                done = done + jnp.where(take, tiles, 0)

        def drain(i, carry):
            out_copy(i).wait()
            return carry
        if phase == "first":
            lax.fori_loop(n_tiles, jnp.maximum(prev_tiles, n_tiles), drain, 0)
        if last:
            @pl.when(it == n_items - 1)
            def _():
                lax.fori_loop(0, n_tiles, drain, 0)
            if fused:
                def drain_a(p, carry):
                    a_copy(p).wait()
                    return carry
                lax.fori_loop(jnp.maximum(n_big - 2, 0), n_big, drain_a, 0)

    zero_rows = zero_ref.shape[0]

    def zero_copy(k):
        row = pl.multiple_of(used_rows + k * zero_rows, zero_rows)
        return pltpu.make_async_copy(zero_ref, y_hbm.at[pl.ds(row, zero_rows), :], zsem_ref.at[0])

    @pl.when((it == n_items - 1) & (j == last_j))
    def _():
        zero_ref[...] = jnp.zeros_like(zero_ref)
        pieces = (y_hbm.shape[0] - used_rows) // zero_rows

        def start(k, carry):
            zero_copy(k).start()
            return carry

        def wait(k, carry):
            zero_copy(k).wait()
            return carry
        lax.fori_loop(0, pieces, start, 0)
        lax.fori_loop(0, pieces, wait, 0)

    @pl.when(n_tiles > 0)
    def _():
        @pl.when(j == 0)
        def _():
            @pl.when(it == 0)
            def _():
                x_copy(0).start()
            x_copy(it).wait()

            @pl.when(it + 1 < n_items)
            def _():
                x_copy(it + 1).start()
            if fused:
                def start(i, carry):
                    wait_previous(i)
                    res_copy(i).start()
                    return carry

                def wait(i, carry):
                    res_copy(i).wait()
                    return carry
                lax.fori_loop(0, n_tiles, start, 0)
                lax.fori_loop(0, n_tiles, wait, 0)
            run("first")

        @pl.when((j > 0) & (j < last_j))
        def _():
            run("middle")

        @pl.when(j == last_j)
        def _():
            run("last")


def _grouped_ffn(xs, wg, wu, wd, layer, item_expert, item_tiles, item_row0, meta, chunk,
                 residual=None):
    n_items = item_expert.shape[0]
    out_rows = xs.shape[0]
    nj = D_FF // FFN_TF
    fused = residual is not None

    def col(it, j, e, t, r, m):
        return jnp.where(t[it] > 0, j, nj - 1)

    in_specs = [
        pl.BlockSpec(memory_space=pl.ANY),
        pl.BlockSpec((None, None, D_MODEL, FFN_TF),
                     lambda it, j, e, t, r, m: (layer, e[it], 0, col(it, j, e, t, r, m))),
        pl.BlockSpec((None, None, D_MODEL, FFN_TF),
                     lambda it, j, e, t, r, m: (layer, e[it], 0, col(it, j, e, t, r, m))),
        pl.BlockSpec((None, None, FFN_TF, D_MODEL),
                     lambda it, j, e, t, r, m: (layer, e[it], col(it, j, e, t, r, m), 0)),
    ]
    args = [item_expert, item_tiles, item_row0, meta, xs, wg, wu, wd]
    out_specs = pl.BlockSpec(memory_space=pl.ANY)
    out_shape = jax.ShapeDtypeStruct((out_rows, D_MODEL), F32)
    scratch = [pltpu.VMEM((2, chunk, D_MODEL), BF16),
               pltpu.VMEM((D_MODEL, 2 * FFN_TF), BF16),
               pltpu.VMEM((FFN_TF, D_MODEL), BF16),
               pltpu.VMEM((chunk, D_MODEL), F32),
               pltpu.VMEM((FFN_ZERO_ROWS, D_MODEL), F32),
               pltpu.SemaphoreType.DMA((2,)),
               pltpu.SemaphoreType.DMA((chunk // FFN_TILE,)),
               pltpu.SemaphoreType.DMA((1,))]
    if fused:
        res, g3, g_layer = residual
        in_specs += [pl.BlockSpec(memory_space=pl.ANY),
                     pl.BlockSpec((None, 1, D_MODEL),
                                  lambda it, j, e, t, r, m: (g_layer, 0, 0))]
        args += [res, g3]
        out_specs = [out_specs, pl.BlockSpec(memory_space=pl.ANY)]
        out_shape = [out_shape, jax.ShapeDtypeStruct((out_rows, D_MODEL), BF16)]
        scratch += [pltpu.VMEM((2, FFN_TRIP_TILES[0] * FFN_TILE, D_MODEL), BF16),
                    pltpu.SemaphoreType.DMA((2,)),
                    pltpu.SemaphoreType.DMA((1,))]
    return pl.pallas_call(
        functools.partial(_ffn_kernel, fused=fused),
        grid_spec=pltpu.PrefetchScalarGridSpec(
            num_scalar_prefetch=4,
            grid=(n_items, nj),
            in_specs=in_specs,
            out_specs=out_specs,
            scratch_shapes=scratch),
        out_shape=out_shape,
        compiler_params=_params(2),
        name="grouped_ffn",
    )(*args)


def _dense_ffn(f, wg, wu, wd, layer, h, g3, g_layer):
    n_items = N_TOK // DENSE_CHUNK
    assert DENSE_CHUNK % (FFN_TRIP_TILES[0] * FFN_TILE) == 0
    return _grouped_ffn(
        f, wg[:, None], wu[:, None], wd[:, None], layer,
        jnp.zeros((n_items,), I32), jnp.full((n_items,), DENSE_CHUNK // FFN_TILE, I32),
        jnp.arange(n_items, dtype=I32) * DENSE_CHUNK, jnp.array([n_items, N_TOK], I32),
        DENSE_CHUNK, residual=(h, g3, g_layer))


def _combine_kernel(pos_ref, h_ref, gate_ref, g_ref, y_hbm, *rest, final):
    if final:
        o_ref, buf_ref, sem_ref = rest
    else:
        hn_ref, a_ref, buf_ref, sem_ref = rest
    t = pl.program_id(0)
    last = pl.num_programs(0) - 1
    tm = h_ref.shape[0]

    def issue(tile, slot):
        def body(r, carry):
            for k in range(TOP_K):
                _row_copy(y_hbm, pos_ref[k * N_TOK + tile * tm + r],
                          buf_ref, slot, k * tm + r, sem_ref).start()
            return carry
        lax.fori_loop(0, tm, body, 0, unroll=DMA_ISSUE_UNROLL)

    @pl.when(t == 0)
    def _():
        issue(0, 0)

    @pl.when(t < last)
    def _():
        issue(t + 1, (t + 1) % 2)

    slot = t % 2
    for r in range(TOP_K * tm):
        _row_copy(y_hbm, 0, buf_ref, slot, r, sem_ref).wait()
    hn = (h_ref[...] + gate_ref[:, 0:1] * buf_ref[slot, pl.ds(0, tm), :]
          + gate_ref[:, 1:2] * buf_ref[slot, pl.ds(tm, tm), :])
    if final:
        o_ref[...] = _rms(hn, g_ref[...])
    else:
        hn_ref[...] = hn
        a_ref[...] = _rms(hn, g_ref[...]).astype(a_ref.dtype)


def _combine(h, y, pos, gates_t, g3, layer, final, tm=256):
    row = pl.BlockSpec((tm, D_MODEL), lambda t, p: (t, 0))
    if final:
        out_specs = row
        out_shape = jax.ShapeDtypeStruct((N_TOK, D_MODEL), F32)
    else:
        out_specs = [row, row]
        out_shape = [jax.ShapeDtypeStruct((N_TOK, D_MODEL), F32),
                     jax.ShapeDtypeStruct((N_TOK, D_MODEL), BF16)]
    return pl.pallas_call(
        functools.partial(_combine_kernel, final=final),
        grid_spec=pltpu.PrefetchScalarGridSpec(
            num_scalar_prefetch=1,
            grid=(N_TOK // tm,),
            in_specs=[row,
                      pl.BlockSpec((tm, TOP_K), lambda t, p: (t, 0)),
                      pl.BlockSpec((None, 1, D_MODEL), lambda t, p: (layer, 0, 0)),
                      pl.BlockSpec(memory_space=pl.ANY)],
            out_specs=out_specs,
            scratch_shapes=[pltpu.VMEM((2, TOP_K * tm, D_MODEL), F32),
                            pltpu.SemaphoreType.DMA((2,))]),
        out_shape=out_shape,
        compiler_params=_params(1),
        name="moe_combine",
    )(pos.reshape(-1), h, gates_t, g3, y)


def _alibi_slopes():
    n = N_DIL * N_HEADS
    s = jnp.exp2(-8.0 * jnp.arange(1, n + 1, dtype=F32) / n)
    return s.reshape(N_HEADS, N_DIL).T


def kernel(x, mix_norm_g, ffn_norm_g, attn_w_in, attn_w_out, gm_w_in, gm_v_norm_g, gm_w_s,
           gm_b_s, gm_w_out, dense_w_gate, dense_w_up, dense_w_down, router_w, moe_w_gate,
           moe_w_up, moe_w_down, final_norm_g):
    h = x.reshape(N_TOK, D_MODEL)
    mix_g = mix_norm_g.reshape(DEPTH, 1, D_MODEL)
    ffn_g = ffn_norm_g.reshape(DEPTH, 1, D_MODEL)
    final_g = final_norm_g.reshape(1, 1, D_MODEL)
    gm_vg = gm_v_norm_g.reshape(-1, 1, GM_WIDTH)
    gm_bt = jnp.swapaxes(gm_b_s, 1, 2)
    router_wt = jnp.swapaxes(router_w, 1, 2)
    slopes = _alibi_slopes()

    a = _rmsnorm(h, mix_g, 0, BF16)
    for i in range(DEPTH):
        j = i // 2
        if i % 2 == 0:
            groups = [_mm_qkv(a, attn_w_in, j, dil, g * (QKV_GROUP_COLS // 1024))
                      for g, (win, dil) in enumerate(DIL_CONFIGS)]
            h, f = _proj_norm(_attention(groups, slopes), attn_w_out, j, h, ffn_g, i)
            h, a = _dense_ffn(f, dense_w_gate, dense_w_up, dense_w_down, j, h, mix_g, i + 1)
        else:
            z = _mm_act(a, gm_w_in, j, 2 * GM_WIDTH, "gelu")
            y = _gm_spatial(z, gm_vg, gm_w_s, gm_bt, j)
            h = _proj(y, gm_w_out, j, h)
            idx, gates, rank, cnt = _router(h, ffn_g, i, router_wt, j)
            pos, src, tile_valid, item_expert, item_tiles, item_row0, meta = _dispatch_plan(
                idx, rank, cnt)
            xs = _dispatch(h, src, tile_valid, ffn_g, i)
            y = _grouped_ffn(xs, moe_w_gate, moe_w_up, moe_w_down, j,
                             item_expert, item_tiles, item_row0, meta, FFN_CHUNK)
            if i == DEPTH - 1:
                return _combine(h, y, pos, gates.T, final_g, 0, True).reshape(BATCH, SEQ, D_MODEL)
            h, a = _combine(h, y, pos, gates.T, mix_g, i + 1, False)
```

```python
import functools

import jax
import jax.numpy as jnp
from jax import lax
from jax.experimental import pallas as pl
from jax.experimental.pallas import tpu as pltpu

D_MODEL = 2048
BATCH = 2
SEQ = 4096
DEPTH = 4
N_TOK = BATCH * SEQ
HEAD_DIM = 128
N_HEADS = D_MODEL // HEAD_DIM
DIL_CONFIGS = ((128, 1), (512, 4), (2048, 16))
N_DIL = len(DIL_CONFIGS)
ATT_BLK = 128
ATT_SKEW = 1
QKV_GROUP_COLS = 3 * N_HEADS * HEAD_DIM
GM_CHUNK = 128
GM_WIDTH = D_MODEL
GM_GROUP_DIM = 128
GM_GROUPS = GM_WIDTH // GM_GROUP_DIM
D_FF = 7 * D_MODEL // 2
N_EXPERTS = 8
TOP_K = 2
EPS = 1e-6
NEG_INF = -1e30

LANES = 128
VMEM_LIMIT = 60 * 1024 * 1024
CAST_ROWS = 256

FFN_TILE = 128
FFN_TRIP_TILES = (8, 4, 2, 1)
DISPATCH_TILE = 256
FFN_CHUNK = 2304
FFN_TILES_PER_CHUNK = FFN_CHUNK // FFN_TILE
DENSE_CHUNK = 2048
FFN_TF = 256
DMA_ISSUE_UNROLL = 8
DMA_PRIORITIES = 2
FFN_ZERO_ROWS = 64
MOE_MAX_ITEMS = -(-TOP_K * N_TOK // FFN_CHUNK) + N_EXPERTS
MOE_ROWS = TOP_K * N_TOK + N_EXPERTS * FFN_TILE + FFN_CHUNK
assert MOE_ROWS % DISPATCH_TILE == 0

F32 = jnp.float32
BF16 = jnp.bfloat16
I32 = jnp.int32


def _params(n_axes):
    return pltpu.CompilerParams(
        dimension_semantics=("arbitrary",) * n_axes, vmem_limit_bytes=VMEM_LIMIT)


def _cast_weight(w_ref, wb_ref):
    def body(c, carry):
        r = pl.multiple_of(c * CAST_ROWS, CAST_ROWS)
        wb_ref[pl.ds(r, CAST_ROWS), :] = w_ref[pl.ds(r, CAST_ROWS), :].astype(BF16)
        return carry
    lax.fori_loop(0, w_ref.shape[0] // CAST_ROWS, body, 0)


def _gelu_tanh(x):
    return 0.5 * x * (1.0 + jnp.tanh(0.7978845608028654 * (x + 0.044715 * (x * x * x))))


def _rms(x, g):
    ms = jnp.mean(x * x, axis=-1, keepdims=True)
    return (x * lax.rsqrt(ms + EPS)) * g


def _rms_kernel(x_ref, g_ref, o_ref):
    o_ref[...] = _rms(x_ref[...], g_ref[...]).astype(o_ref.dtype)


def _rmsnorm(h, g3, layer, out_dtype, tm=512):
    return pl.pallas_call(
        _rms_kernel,
        grid=(N_TOK // tm,),
        in_specs=[pl.BlockSpec((tm, D_MODEL), lambda i: (i, 0)),
                  pl.BlockSpec((None, 1, D_MODEL), lambda i: (layer, 0, 0))],
        out_specs=pl.BlockSpec((tm, D_MODEL), lambda i: (i, 0)),
        out_shape=jax.ShapeDtypeStruct((N_TOK, D_MODEL), out_dtype),
        compiler_params=_params(1),
        name="rmsnorm",
    )(h, g3)


def _w_spec(layer, k, tn, col_off):
    return pl.BlockSpec((None, k, tn), lambda j, i: (layer, 0, col_off + j))


def _mm_act_kernel(a_ref, w_ref, o_ref, wb_ref, *, act):
    @pl.when(pl.program_id(1) == 0)
    def _():
        _cast_weight(w_ref, wb_ref)
    acc = jnp.dot(a_ref[...], wb_ref[...], preferred_element_type=F32)
    if act == "gelu":
        acc = _gelu_tanh(acc)
    o_ref[...] = acc.astype(o_ref.dtype)


def _mm_act(a, w, layer, n_cols, act, tm=1024, tn=1024, col_off=0):
    k = a.shape[1]
    return pl.pallas_call(
        functools.partial(_mm_act_kernel, act=act),
        grid=(n_cols // tn, N_TOK // tm),
        in_specs=[pl.BlockSpec((tm, k), lambda j, i: (i, 0)),
                  _w_spec(layer, k, tn, col_off)],
        out_specs=pl.BlockSpec((tm, tn), lambda j, i: (i, j)),
        out_shape=jax.ShapeDtypeStruct((N_TOK, n_cols), BF16),
        scratch_shapes=[pltpu.VMEM((k, tn), BF16)],
        compiler_params=_params(2),
        name="mm_act",
    )(a, w)


def _mm_qkv_kernel(a_ref, w_ref, o_ref, wb_ref, *rest, dil):
    @pl.when(pl.program_id(1) == 0)
    def _():
        _cast_weight(w_ref, wb_ref)
    acc = jnp.dot(a_ref[...], wb_ref[...], preferred_element_type=F32)
    heads = o_ref.shape[0]
    if dil == 1:
        for hh in range(heads):
            o_ref[hh, 0] = acc[:, hh * HEAD_DIM:(hh + 1) * HEAD_DIM].astype(o_ref.dtype)
    else:
        acc_ref, = rest
        rows = acc_ref.shape[1] // dil
        for hh in range(heads):
            acc_ref[hh] = acc[:, hh * HEAD_DIM:(hh + 1) * HEAD_DIM]

        def regroup(hh, carry):
            for r in range(dil):
                o_ref[hh, r] = acc_ref[hh, pl.ds(r, rows, stride=dil), :].astype(o_ref.dtype)
            return carry
        lax.fori_loop(0, heads, regroup, 0, unroll=dil <= 4)


def _mm_qkv(a, w, layer, dil, col_off, tm=1024, tn=1024):
    k = a.shape[1]
    tiles_per_b = SEQ // tm
    heads = tn // HEAD_DIM
    scratch = [pltpu.VMEM((k, tn), BF16)]
    if dil > 1:
        scratch.append(pltpu.VMEM((heads, tm, HEAD_DIM), F32))
    out = pl.pallas_call(
        functools.partial(_mm_qkv_kernel, dil=dil),
        grid=(QKV_GROUP_COLS // tn, N_TOK // tm),
        in_specs=[pl.BlockSpec((tm, k), lambda j, i: (i, 0)),
                  _w_spec(layer, k, tn, col_off)],
        out_specs=pl.BlockSpec((None, heads, dil, tm // dil, HEAD_DIM),
                               lambda j, i: (i // tiles_per_b, j, 0, i % tiles_per_b, 0)),
        out_shape=jax.ShapeDtypeStruct(
            (BATCH, 3 * N_HEADS, dil, SEQ // dil, HEAD_DIM), BF16),
        scratch_shapes=scratch,
        compiler_params=_params(2),
        name="mm_qkv",
    )(a, w)
    return out.reshape(BATCH, 3 * N_HEADS, SEQ, HEAD_DIM)


def _proj_kernel(a_ref, w_ref, res_ref, o_ref, wb_ref):
    @pl.when(pl.program_id(1) == 0)
    def _():
        _cast_weight(w_ref, wb_ref)
    o_ref[...] = res_ref[...] + jnp.dot(a_ref[...], wb_ref[...], preferred_element_type=F32)


def _proj(a, w, layer, res, tm=512, tn=1024):
    k = a.shape[1]
    return pl.pallas_call(
        _proj_kernel,
        grid=(D_MODEL // tn, N_TOK // tm),
        in_specs=[pl.BlockSpec((tm, k), lambda j, i: (i, 0)),
                  _w_spec(layer, k, tn, 0),
                  pl.BlockSpec((tm, tn), lambda j, i: (i, j))],
        out_specs=pl.BlockSpec((tm, tn), lambda j, i: (i, j)),
        out_shape=jax.ShapeDtypeStruct((N_TOK, D_MODEL), F32),
        scratch_shapes=[pltpu.VMEM((k, tn), BF16)],
        compiler_params=_params(2),
        name="proj",
    )(a, w, res)


def _proj_norm_kernel(a_ref, w_ref, res_ref, g_ref, o_ref, f_ref, wb_ref):
    @pl.when(pl.program_id(1) == 0)
    def _():
        _cast_weight(w_ref, wb_ref)
    a = jnp.concatenate([a_ref[hh] for hh in range(a_ref.shape[0])], axis=1)
    out = res_ref[...] + jnp.dot(a, wb_ref[...], preferred_element_type=F32)
    o_ref[...] = out
    f_ref[...] = _rms(out, g_ref[...]).astype(f_ref.dtype)


def _proj_norm(a, w, layer, res, g3, g_layer, tm=256):
    tiles_per_b = SEQ // tm
    k = a.shape[1] * a.shape[3]
    row = pl.BlockSpec((tm, D_MODEL), lambda j, i: (i, 0))
    return pl.pallas_call(
        _proj_norm_kernel,
        grid=(1, N_TOK // tm),
        in_specs=[pl.BlockSpec((None, a.shape[1], tm, a.shape[3]),
                               lambda j, i: (i // tiles_per_b, 0, i % tiles_per_b, 0)),
                  _w_spec(layer, k, D_MODEL, 0),
                  row,
                  pl.BlockSpec((None, 1, D_MODEL), lambda j, i: (g_layer, 0, 0))],
        out_specs=[row, row],
        out_shape=[jax.ShapeDtypeStruct((N_TOK, D_MODEL), F32),
                   jax.ShapeDtypeStruct((N_TOK, D_MODEL), BF16)],
        scratch_shapes=[pltpu.VMEM((k, D_MODEL), BF16)],
        compiler_params=_params(2),
        name="proj_norm",
    )(a, w, res, g3)


def _attn_kernel(slopes_ref, *refs):
    qkv_refs = refs[:9]
    o_ref, tab_ref, oscr_ref, lscr_ref = refs[9:]
    h = pl.program_id(1)
    ii = lax.broadcasted_iota(I32, (ATT_BLK, 2 * ATT_BLK), 0)
    jj = lax.broadcasted_iota(I32, (ATT_BLK, 2 * ATT_BLK), 1)
    delta = ii + ATT_BLK - jj
    scale = HEAD_DIM ** -0.5
    nt = (((1,), (1,)), ((), ()))
    n_blocks = SEQ // ATT_BLK
    ones = jnp.ones((2 * ATT_BLK, HEAD_DIM), BF16)
    for g, (win, dil) in enumerate(DIL_CONFIGS):
        q_ref, k_ref, v_ref = qkv_refs[3 * g:3 * g + 3]
        nb = n_blocks // dil
        tab_ref[...] = jnp.where((delta >= 0) & (delta <= ATT_BLK),
                                 -slopes_ref[g, h] * (dil * delta).astype(F32), NEG_INF)
        s, m, p, done = {}, {}, {}, []

        def keys(c, nb=nb):
            first = c % nb == 0
            return (slice((c - (not first)) * ATT_BLK, (c + 1) * ATT_BLK),
                    slice(ATT_BLK if first else 0, 2 * ATT_BLK))

        def scores(c):
            rows, cols = keys(c)
            q = q_ref[c * ATT_BLK:(c + 1) * ATT_BLK, :]
            s[c] = lax.dot_general(q, k_ref[rows, :], nt,
                                   preferred_element_type=F32) * scale + tab_ref[:, cols]

        def softmax(c):
            m[c] = jnp.max(s[c], axis=-1, keepdims=True)
            p[c] = jnp.exp(s.pop(c) - m[c]).astype(BF16)

        def values(c):
            rows, _ = keys(c)
            vv = jnp.concatenate([v_ref[rows, :], ones[:rows.stop - rows.start]], axis=1)
            od = jnp.dot(p.pop(c), vv, preferred_element_type=F32)
            den = od[:, HEAD_DIM:]
            done.append((c, od[:, :HEAD_DIM] / den, m.pop(c) + jnp.log(den)))

        for step in range(n_blocks + 2 * ATT_SKEW):
            if step < n_blocks:
                scores(step)
            if 0 <= step - ATT_SKEW < n_blocks:
                softmax(step - ATT_SKEW)
            if 0 <= step - 2 * ATT_SKEW < n_blocks:
                values(step - 2 * ATT_SKEW)
        for c, o, lse in done:
            start = (c % nb) * (ATT_BLK * dil) + c // nb
            dst = pl.ds(start, ATT_BLK) if dil == 1 else pl.ds(start, ATT_BLK, stride=dil)
            oscr_ref[g, dst, :] = o
            lscr_ref[g, dst, :] = jnp.broadcast_to(lse, (ATT_BLK, HEAD_DIM))

    rows = 256

    def merge(t, carry):
        sl = pl.ds(pl.multiple_of(t * rows, rows), rows)
        l0 = lscr_ref[0, sl, :]
        l1 = lscr_ref[1, sl, :]
        l2 = lscr_ref[2, sl, :]
        m = jnp.maximum(jnp.maximum(l0, l1), l2)
        e0 = jnp.exp(l0 - m)
        e1 = jnp.exp(l1 - m)
        e2 = jnp.exp(l2 - m)
        den = e0 + e1 + e2
        o = (e0 * oscr_ref[0, sl, :] + e1 * oscr_ref[1, sl, :] + e2 * oscr_ref[2, sl, :]) / den
        o_ref[sl, :] = o.astype(o_ref.dtype)
        return carry

    lax.fori_loop(0, SEQ // rows, merge, 0)


def _attention(qkv_groups, slopes):
    in_specs = [pl.BlockSpec(memory_space=pltpu.SMEM)]
    args = [slopes]
    for qkv in qkv_groups:
        for part in range(3):
            in_specs.append(pl.BlockSpec(
                (None, None, SEQ, HEAD_DIM),
                lambda b, h, part=part: (b, part * N_HEADS + h, 0, 0)))
            args.append(qkv)
    return pl.pallas_call(
        _attn_kernel,
        grid=(BATCH, N_HEADS),
        in_specs=in_specs,
        out_specs=pl.BlockSpec((None, None, SEQ, HEAD_DIM), lambda b, h: (b, h, 0, 0)),
        out_shape=jax.ShapeDtypeStruct((BATCH, N_HEADS, SEQ, HEAD_DIM), BF16),
        scratch_shapes=[pltpu.VMEM((ATT_BLK, 2 * ATT_BLK), F32),
                        pltpu.VMEM((N_DIL, SEQ, HEAD_DIM), F32),
                        pltpu.VMEM((N_DIL, SEQ, HEAD_DIM), F32)],
        compiler_params=_params(2),
        name="dilated_attn",
    )(*args)


def _gm_spatial_kernel(z_ref, vg_ref, ws_ref, bt_ref, y_ref, wsb_ref):
    @pl.when(pl.program_id(0) == 0)
    def _():
        ii = lax.broadcasted_iota(I32, (GM_CHUNK, GM_CHUNK), 0)
        jj = lax.broadcasted_iota(I32, (GM_CHUNK, GM_CHUNK), 1)
        for g in range(GM_GROUPS):
            wsb_ref[g] = jnp.where(ii >= jj, ws_ref[g], 0.0).astype(BF16)

    for c in range(z_ref.shape[0] // GM_CHUNK):
        rows = slice(c * GM_CHUNK, (c + 1) * GM_CHUNK)
        vn = _rms(z_ref[rows, GM_WIDTH:].astype(F32), vg_ref[...]).astype(BF16)
        for g in range(GM_GROUPS):
            cols = slice(g * GM_GROUP_DIM, (g + 1) * GM_GROUP_DIM)
            s = jnp.dot(wsb_ref[g], vn[:, cols], preferred_element_type=F32) + bt_ref[:, g:g + 1]
            y_ref[rows, cols] = (z_ref[rows, cols].astype(F32) * s).astype(y_ref.dtype)


def _gm_spatial(z, vg3, ws, bt, layer, tm=256):
    return pl.pallas_call(
        _gm_spatial_kernel,
        grid=(N_TOK // tm,),
        in_specs=[pl.BlockSpec((tm, 2 * GM_WIDTH), lambda i: (i, 0)),
                  pl.BlockSpec((None, 1, GM_WIDTH), lambda i: (layer, 0, 0)),
                  pl.BlockSpec((None, GM_GROUPS, GM_CHUNK, GM_CHUNK), lambda i: (layer, 0, 0, 0)),
                  pl.BlockSpec((None, GM_CHUNK, GM_GROUPS), lambda i: (layer, 0, 0))],
        out_specs=pl.BlockSpec((tm, GM_WIDTH), lambda i: (i, 0)),
        out_shape=jax.ShapeDtypeStruct((N_TOK, GM_WIDTH), BF16),
        scratch_shapes=[pltpu.VMEM((GM_GROUPS, GM_CHUNK, GM_CHUNK), BF16)],
        compiler_params=_params(1),
        name="gm_spatial",
    )(z, vg3, ws, bt)


def _router_kernel(h_ref, g_ref, rwt_ref, idx_ref, gate_ref, rank_ref, cnt_ref, run_ref):
    @pl.when(pl.program_id(0) == 0)
    def _():
        run_ref[...] = jnp.zeros_like(run_ref)

    f = _rms(h_ref[...], g_ref[...])
    logits = lax.dot_general(rwt_ref[...], f, (((1,), (1,)), ((), ())),
                             precision=lax.Precision.HIGHEST, preferred_element_type=F32)
    tm = logits.shape[1]
    eid = lax.broadcasted_iota(I32, logits.shape, 0)
    m1 = jnp.max(logits, axis=0, keepdims=True)
    i1 = jnp.min(jnp.where(logits == m1, eid, N_EXPERTS), axis=0, keepdims=True)
    rest = jnp.where(eid == i1, -jnp.inf, logits)
    m2 = jnp.max(rest, axis=0, keepdims=True)
    i2 = jnp.min(jnp.where(rest == m2, eid, N_EXPERTS), axis=0, keepdims=True)
    e2 = jnp.exp(m2 - m1)
    den = 1.0 + e2
    idx_ref[0:1, :] = i1
    idx_ref[1:2, :] = i2
    gate_ref[0:1, :] = 1.0 / den
    gate_ref[1:2, :] = e2 / den

    sel1 = eid == i1
    sel2 = eid == i2
    onehot = jnp.where(sel1, 1.0, jnp.where(sel2, 1.0, 0.0))
    earlier = (lax.broadcasted_iota(I32, (tm, tm), 0)
               < lax.broadcasted_iota(I32, (tm, tm), 1))
    before = jnp.dot(onehot.astype(BF16), jnp.where(earlier, 1.0, 0.0).astype(BF16),
                     preferred_element_type=F32) + run_ref[:, 0:1]
    rank_ref[0:1, :] = jnp.sum(jnp.where(sel1, before, 0.0), axis=0, keepdims=True).astype(I32)
    rank_ref[1:2, :] = jnp.sum(jnp.where(sel2, before, 0.0), axis=0, keepdims=True).astype(I32)
    run_ref[...] = run_ref[...] + jnp.sum(onehot, axis=1, keepdims=True)
    cnt_ref[...] = run_ref[...]


def _router(h, g3, glayer, rwt, mlayer, tm=512):
    pair = pl.BlockSpec((TOP_K, tm), lambda i: (0, i))
    return pl.pallas_call(
        _router_kernel,
        grid=(N_TOK // tm,),
        in_specs=[pl.BlockSpec((tm, D_MODEL), lambda i: (i, 0)),
                  pl.BlockSpec((None, 1, D_MODEL), lambda i: (glayer, 0, 0)),
                  pl.BlockSpec((None, N_EXPERTS, D_MODEL), lambda i: (mlayer, 0, 0))],
        out_specs=[pair, pair, pair, pl.BlockSpec((N_EXPERTS, LANES), lambda i: (0, 0))],
        out_shape=[jax.ShapeDtypeStruct((TOP_K, N_TOK), I32),
                   jax.ShapeDtypeStruct((TOP_K, N_TOK), F32),
                   jax.ShapeDtypeStruct((TOP_K, N_TOK), I32),
                   jax.ShapeDtypeStruct((N_EXPERTS, LANES), F32)],
        scratch_shapes=[pltpu.VMEM((N_EXPERTS, LANES), F32)],
        compiler_params=_params(1),
        name="router",
    )(h, g3, rwt)


def _dispatch_plan(idx, rank, cnt):
    counts = cnt[:, 0].astype(I32)
    tiles_e = (counts + FFN_TILE - 1) // FFN_TILE
    items_e = (tiles_e + FFN_TILES_PER_CHUNK - 1) // FFN_TILES_PER_CHUNK
    items_end = jnp.cumsum(items_e)
    items_start = items_end - items_e
    n_items = items_end[-1]
    first_row = (jnp.cumsum(tiles_e) - tiles_e) * FFN_TILE
    used_rows = jnp.sum(tiles_e) * FFN_TILE
    pos = rank
    for e in range(N_EXPERTS):
        pos = pos + jnp.where(idx == e, first_row[e], 0)
    it = jnp.arange(MOE_MAX_ITEMS, dtype=I32)
    it_c = jnp.clip(it, 0, jnp.maximum(n_items - 1, 0))
    item_expert = jnp.minimum(jnp.searchsorted(items_end, it_c, side="right"),
                              N_EXPERTS - 1).astype(I32)
    local = it_c - items_start[item_expert]
    item_tiles = jnp.clip(tiles_e[item_expert] - local * FFN_TILES_PER_CHUNK,
                          0, FFN_TILES_PER_CHUNK)
    item_tiles = jnp.where(it < n_items, item_tiles, 0).astype(I32)
    item_row0 = (first_row[item_expert] + local * FFN_CHUNK).astype(I32)
    tok = jnp.broadcast_to(jnp.arange(N_TOK, dtype=I32), (TOP_K, N_TOK))
    src = jnp.zeros((MOE_ROWS,), I32).at[pos.reshape(-1)].set(tok.reshape(-1))
    tile_valid = (jnp.arange(MOE_ROWS // DISPATCH_TILE, dtype=I32) * DISPATCH_TILE
                  < used_rows).astype(I32)
    meta = jnp.stack([n_items, used_rows]).astype(I32)
    return pos, src, tile_valid, item_expert, item_tiles, item_row0, meta


def _row_copy(src_hbm, row, dst_ref, slot, r, sem_ref):
    return pltpu.make_async_copy(src_hbm.at[pl.ds(row, 1), :],
                                 dst_ref.at[slot, pl.ds(r, 1), :], sem_ref.at[slot])


def _dispatch_kernel(src_ref, valid_ref, h_hbm, g_ref, o_ref, buf_ref, sem_ref):
    p = pl.program_id(0)
    last = pl.num_programs(0) - 1

    def issue(tile, slot):
        def body(pair, carry):
            for q in range(DMA_PRIORITIES):
                r = pair * DMA_PRIORITIES + q
                _row_copy(h_hbm, src_ref[tile * DISPATCH_TILE + r],
                          buf_ref, slot, r, sem_ref).start(priority=q)
            return carry
        lax.fori_loop(0, DISPATCH_TILE // DMA_PRIORITIES, body, 0,
                      unroll=DMA_ISSUE_UNROLL // DMA_PRIORITIES)

    @pl.when((p == 0) & (valid_ref[0] > 0))
    def _():
        issue(0, 0)

    nxt = jnp.minimum(p + 1, last)

    @pl.when((p < last) & (valid_ref[nxt] > 0))
    def _():
        issue(nxt, nxt % 2)

    slot = p % 2

    @pl.when(valid_ref[p] > 0)
    def _():
        for r in range(DISPATCH_TILE):
            _row_copy(h_hbm, 0, buf_ref, slot, r, sem_ref).wait()
        o_ref[...] = _rms(buf_ref[slot], g_ref[...]).astype(o_ref.dtype)

    @pl.when(valid_ref[p] == 0)
    def _():
        o_ref[...] = jnp.zeros_like(o_ref)


def _dispatch(h, src, tile_valid, g3, layer):
    return pl.pallas_call(
        _dispatch_kernel,
        grid_spec=pltpu.PrefetchScalarGridSpec(
            num_scalar_prefetch=2,
            grid=(MOE_ROWS // DISPATCH_TILE,),
            in_specs=[pl.BlockSpec(memory_space=pl.ANY),
                      pl.BlockSpec((None, 1, D_MODEL), lambda p, s, v: (layer, 0, 0))],
            out_specs=pl.BlockSpec((DISPATCH_TILE, D_MODEL), lambda p, s, v: (p, 0)),
            scratch_shapes=[pltpu.VMEM((2, DISPATCH_TILE, D_MODEL), F32),
                            pltpu.SemaphoreType.DMA((2,))]),
        out_shape=jax.ShapeDtypeStruct((MOE_ROWS, D_MODEL), BF16),
        compiler_params=_params(1),
        name="moe_dispatch",
    )(src, tile_valid, h, g3)


def _ffn_kernel(exp_ref, tiles_ref, row0_ref, meta_ref, x_hbm, wg_ref, wu_ref, wd_ref, *rest,
                fused):
    if fused:
        (res_hbm, g_ref, y_hbm, a_hbm, xbuf_ref, wgu_ref, wdb_ref, acc_ref, zero_ref, xsem_ref,
         sem_ref, zsem_ref, astage_ref, asem_ref, rsem_ref) = rest
    else:
        y_hbm, xbuf_ref, wgu_ref, wdb_ref, acc_ref, zero_ref, xsem_ref, sem_ref, zsem_ref = rest
    it = pl.program_id(0)
    j = pl.program_id(1)
    last_j = pl.num_programs(1) - 1
    n_tiles = tiles_ref[it]
    n_items = meta_ref[0]
    used_rows = meta_ref[1]
    tf = wg_ref.shape[1]
    chunk = xbuf_ref.shape[1]
    big = FFN_TRIP_TILES[0]
    row0 = row0_ref[it]
    prev_tiles = jnp.where(it > 0, tiles_ref[jnp.maximum(it - 1, 0)], 0)
    x_ref = xbuf_ref.at[it % 2]

    def x_copy(item):
        return pltpu.make_async_copy(
            x_hbm.at[pl.ds(pl.multiple_of(row0_ref[item], FFN_TILE), chunk), :],
            xbuf_ref.at[item % 2], xsem_ref.at[item % 2])

    def out_copy(i):
        r = pl.multiple_of(i * FFN_TILE, FFN_TILE)
        return pltpu.make_async_copy(
            acc_ref.at[pl.ds(r, FFN_TILE), :],
            y_hbm.at[pl.ds(pl.multiple_of(row0 + r, FFN_TILE), FFN_TILE), :], sem_ref.at[i])

    def res_copy(i):
        r = pl.multiple_of(i * FFN_TILE, FFN_TILE)
        return pltpu.make_async_copy(
            res_hbm.at[pl.ds(pl.multiple_of(row0 + r, FFN_TILE), FFN_TILE), :],
            acc_ref.at[pl.ds(r, FFN_TILE), :], rsem_ref.at[0])

    def a_copy(p):
        r = pl.multiple_of(row0 + p * (big * FFN_TILE), FFN_TILE)
        return pltpu.make_async_copy(astage_ref.at[p % 2],
                                     a_hbm.at[pl.ds(r, big * FFN_TILE), :], asem_ref.at[p % 2])

    def cast_weights():
        wgu_ref[:, :tf] = wg_ref[...].astype(BF16)
        wgu_ref[:, tf:] = wu_ref[...].astype(BF16)
        wdb_ref[...] = wd_ref[...].astype(BF16)

    def ffn_rows(start, n_rows, first, cast=False):
        rows = pl.ds(pl.multiple_of(start, FFN_TILE), n_rows)
        if cast:
            wgu_ref[:, :tf] = wg_ref[...].astype(BF16)
            gate = jnp.dot(x_ref[rows, :], wgu_ref[:, :tf], preferred_element_type=F32)
            wgu_ref[:, tf:] = wu_ref[...].astype(BF16)
            up = jnp.dot(x_ref[rows, :], wgu_ref[:, tf:], preferred_element_type=F32)
            wdb_ref[...] = wd_ref[...].astype(BF16)
        else:
            gu = jnp.dot(x_ref[rows, :], wgu_ref[...], preferred_element_type=F32)
            gate, up = gu[:, :tf], gu[:, tf:]
        act = ((gate * jax.nn.sigmoid(gate)) * up).astype(BF16)
        d = jnp.dot(act, wdb_ref[...], preferred_element_type=F32)
        if first:
            acc_ref[rows, :] = d
        else:
            acc_ref[rows, :] += d

    def wait_previous(i):
        @pl.when(i < prev_tiles)
        def _():
            out_copy(i).wait()

    def run(phase):
        first = phase == "first" and not fused
        last = phase == "last"

        def trip(tile0, tiles, cast=False):
            if first:
                for i in range(tiles):
                    wait_previous(tile0 + i)
            ffn_rows(tile0 * FFN_TILE, tiles * FFN_TILE, first, cast)
            if last:
                for i in range(tiles):
                    out_copy(tile0 + i).start()

        def big_trip(p, carry, cast=False):
            trip(pl.multiple_of(p * big, big), big, cast)
            if last and fused:
                @pl.when(p >= 2)
                def _():
                    a_copy(p - 2).wait()
                rows = pl.ds(pl.multiple_of(p * (big * FFN_TILE), big * FFN_TILE), big * FFN_TILE)
                astage_ref[p % 2] = _rms(acc_ref[rows, :], g_ref[...]).astype(BF16)
                a_copy(p).start()
            return carry
        n_big = n_tiles // big
        merge = (n_tiles - n_big * big == 1) & (n_big > 0) & (not fused)
        n_loop = n_big - jnp.where(merge, 1, 0)

        @pl.when(n_loop > 0)
        def _():
            big_trip(jnp.int32(0), 0, cast=True)

        @pl.when(n_loop == 0)
        def _():
            cast_weights()
        lax.fori_loop(1, n_loop, big_trip, 0)
        if not fused:
            @pl.when(merge)
            def _():
                trip(pl.multiple_of(n_loop * big, big), big + 1)
            done = jnp.where(merge, n_tiles, n_big * big)
            for tiles in FFN_TRIP_TILES[1:]:
                take = ((n_tiles - done) // tiles) > 0

                @pl.when(take)
                def _(done=done, tiles=tiles):
                    trip(pl.multiple_of(done, tiles), tiles)
                done = done + jnp.where(take, tiles, 0)

        def drain(i, carry):
            out_copy(i).wait()
            return carry
        if phase == "first":
            lax.fori_loop(n_tiles, jnp.maximum(prev_tiles, n_tiles), drain, 0)
        if last:
            @pl.when(it == n_items - 1)
            def _():
                lax.fori_loop(0, n_tiles, drain, 0)
            if fused:
                def drain_a(p, carry):
                    a_copy(p).wait()
                    return carry
                lax.fori_loop(jnp.maximum(n_big - 2, 0), n_big, drain_a, 0)

    zero_rows = zero_ref.shape[0]

    def zero_copy(k):
        row = pl.multiple_of(used_rows + k * zero_rows, zero_rows)
        return pltpu.make_async_copy(zero_ref, y_hbm.at[pl.ds(row, zero_rows), :], zsem_ref.at[0])

    @pl.when((it == n_items - 1) & (j == last_j))
    def _():
        zero_ref[...] = jnp.zeros_like(zero_ref)
        pieces = (y_hbm.shape[0] - used_rows) // zero_rows

        def start(k, carry):
            zero_copy(k).start()
            return carry

        def wait(k, carry):
            zero_copy(k).wait()
            return carry
        lax.fori_loop(0, pieces, start, 0)
        lax.fori_loop(0, pieces, wait, 0)

    @pl.when(n_tiles > 0)
    def _():
        @pl.when(j == 0)
        def _():
            @pl.when(it == 0)
            def _():
                x_copy(0).start()
            x_copy(it).wait()

            @pl.when(it + 1 < n_items)
            def _():
                x_copy(it + 1).start()
            if fused:
                def start(i, carry):
                    wait_previous(i)
                    res_copy(i).start()
                    return carry

                def wait(i, carry):
                    res_copy(i).wait()
                    return carry
                lax.fori_loop(0, n_tiles, start, 0)
                lax.fori_loop(0, n_tiles, wait, 0)
            run("first")

        @pl.when((j > 0) & (j < last_j))
        def _():
            run("middle")

        @pl.when(j == last_j)
        def _():
            run("last")


def _grouped_ffn(xs, wg, wu, wd, layer, item_expert, item_tiles, item_row0, meta, chunk,
                 residual=None):
    n_items = item_expert.shape[0]
    out_rows = xs.shape[0]
    nj = D_FF // FFN_TF
    fused = residual is not None

    def col(it, j, e, t, r, m):
        return jnp.where(t[it] > 0, j, nj - 1)

    in_specs = [
        pl.BlockSpec(memory_space=pl.ANY),
        pl.BlockSpec((None, None, D_MODEL, FFN_TF),
                     lambda it, j, e, t, r, m: (layer, e[it], 0, col(it, j, e, t, r, m))),
        pl.BlockSpec((None, None, D_MODEL, FFN_TF),
                     lambda it, j, e, t, r, m: (layer, e[it], 0, col(it, j, e, t, r, m))),
        pl.BlockSpec((None, None, FFN_TF, D_MODEL),
                     lambda it, j, e, t, r, m: (layer, e[it], col(it, j, e, t, r, m), 0)),
    ]
    args = [item_expert, item_tiles, item_row0, meta, xs, wg, wu, wd]
    out_specs = pl.BlockSpec(memory_space=pl.ANY)
    out_shape = jax.ShapeDtypeStruct((out_rows, D_MODEL), F32)
    scratch = [pltpu.VMEM((2, chunk, D_MODEL), BF16),
               pltpu.VMEM((D_MODEL, 2 * FFN_TF), BF16),
               pltpu.VMEM((FFN_TF, D_MODEL), BF16),
               pltpu.VMEM((chunk, D_MODEL), F32),
               pltpu.VMEM((FFN_ZERO_ROWS, D_MODEL), F32),
               pltpu.SemaphoreType.DMA((2,)),
               pltpu.SemaphoreType.DMA((chunk // FFN_TILE,)),
               pltpu.SemaphoreType.DMA((1,))]
    if fused:
        res, g3, g_layer = residual
        in_specs += [pl.BlockSpec(memory_space=pl.ANY),
                     pl.BlockSpec((None, 1, D_MODEL),
                                  lambda it, j, e, t, r, m: (g_layer, 0, 0))]
        args += [res, g3]
        out_specs = [out_specs, pl.BlockSpec(memory_space=pl.ANY)]
        out_shape = [out_shape, jax.ShapeDtypeStruct((out_rows, D_MODEL), BF16)]
        scratch += [pltpu.VMEM((2, FFN_TRIP_TILES[0] * FFN_TILE, D_MODEL), BF16),
                    pltpu.SemaphoreType.DMA((2,)),
                    pltpu.SemaphoreType.DMA((1,))]
    return pl.pallas_call(
        functools.partial(_ffn_kernel, fused=fused),
        grid_spec=pltpu.PrefetchScalarGridSpec(
            num_scalar_prefetch=4,
            grid=(n_items, nj),
            in_specs=in_specs,
            out_specs=out_specs,
            scratch_shapes=scratch),
        out_shape=out_shape,
        compiler_params=_params(2),
        name="grouped_ffn",
    )(*args)


def _dense_ffn(f, wg, wu, wd, layer, h, g3, g_layer):
    n_items = N_TOK // DENSE_CHUNK
    assert DENSE_CHUNK % (FFN_TRIP_TILES[0] * FFN_TILE) == 0
    return _grouped_ffn(
        f, wg[:, None], wu[:, None], wd[:, None], layer,
        jnp.zeros((n_items,), I32), jnp.full((n_items,), DENSE_CHUNK // FFN_TILE, I32),
        jnp.arange(n_items, dtype=I32) * DENSE_CHUNK, jnp.array([n_items, N_TOK], I32),
        DENSE_CHUNK, residual=(h, g3, g_layer))


def _combine_kernel(pos_ref, h_ref, gate_ref, g_ref, y_hbm, *rest, final):
    if final:
        o_ref, buf_ref, sem_ref = rest
    else:
        hn_ref, a_ref, buf_ref, sem_ref = rest
    t = pl.program_id(0)
    last = pl.num_programs(0) - 1
    tm = h_ref.shape[0]

    def issue(tile, slot):
        def body(r, carry):
            for k in range(TOP_K):
                _row_copy(y_hbm, pos_ref[k * N_TOK + tile * tm + r],
                          buf_ref, slot, k * tm + r, sem_ref).start(priority=k % DMA_PRIORITIES)
            return carry
        lax.fori_loop(0, tm, body, 0, unroll=DMA_ISSUE_UNROLL)

    @pl.when(t == 0)
    def _():
        issue(0, 0)

    @pl.when(t < last)
    def _():
        issue(t + 1, (t + 1) % 2)

    slot = t % 2
    for r in range(TOP_K * tm):
        _row_copy(y_hbm, 0, buf_ref, slot, r, sem_ref).wait()
    hn = (h_ref[...] + gate_ref[:, 0:1] * buf_ref[slot, pl.ds(0, tm), :]
          + gate_ref[:, 1:2] * buf_ref[slot, pl.ds(tm, tm), :])
    if final:
        o_ref[...] = _rms(hn, g_ref[...])
    else:
        hn_ref[...] = hn
        a_ref[...] = _rms(hn, g_ref[...]).astype(a_ref.dtype)


def _combine(h, y, pos, gates_t, g3, layer, final, tm=256):
    row = pl.BlockSpec((tm, D_MODEL), lambda t, p: (t, 0))
    if final:
        out_specs = row
        out_shape = jax.ShapeDtypeStruct((N_TOK, D_MODEL), F32)
    else:
        out_specs = [row, row]
        out_shape = [jax.ShapeDtypeStruct((N_TOK, D_MODEL), F32),
                     jax.ShapeDtypeStruct((N_TOK, D_MODEL), BF16)]
    return pl.pallas_call(
        functools.partial(_combine_kernel, final=final),
        grid_spec=pltpu.PrefetchScalarGridSpec(
            num_scalar_prefetch=1,
            grid=(N_TOK // tm,),
            in_specs=[row,
                      pl.BlockSpec((tm, TOP_K), lambda t, p: (t, 0)),
                      pl.BlockSpec((None, 1, D_MODEL), lambda t, p: (layer, 0, 0)),
                      pl.BlockSpec(memory_space=pl.ANY)],
            out_specs=out_specs,
            scratch_shapes=[pltpu.VMEM((2, TOP_K * tm, D_MODEL), F32),
                            pltpu.SemaphoreType.DMA((2,))]),
        out_shape=out_shape,
        compiler_params=_params(1),
        name="moe_combine",
    )(pos.reshape(-1), h, gates_t, g3, y)


def _alibi_slopes():
    n = N_DIL * N_HEADS
    s = jnp.exp2(-8.0 * jnp.arange(1, n + 1, dtype=F32) / n)
    return s.reshape(N_HEADS, N_DIL).T


def kernel(x, mix_norm_g, ffn_norm_g, attn_w_in, attn_w_out, gm_w_in, gm_v_norm_g, gm_w_s,
           gm_b_s, gm_w_out, dense_w_gate, dense_w_up, dense_w_down, router_w, moe_w_gate,
           moe_w_up, moe_w_down, final_norm_g):
    h = x.reshape(N_TOK, D_MODEL)
    mix_g = mix_norm_g.reshape(DEPTH, 1, D_MODEL)
    ffn_g = ffn_norm_g.reshape(DEPTH, 1, D_MODEL)
    final_g = final_norm_g.reshape(1, 1, D_MODEL)
    gm_vg = gm_v_norm_g.reshape(-1, 1, GM_WIDTH)
    gm_bt = jnp.swapaxes(gm_b_s, 1, 2)
    router_wt = jnp.swapaxes(router_w, 1, 2)
    slopes = _alibi_slopes()

    a = _rmsnorm(h, mix_g, 0, BF16)
    for i in range(DEPTH):
        j = i // 2
        if i % 2 == 0:
            groups = [_mm_qkv(a, attn_w_in, j, dil, g * (QKV_GROUP_COLS // 1024))
                      for g, (win, dil) in enumerate(DIL_CONFIGS)]
            h, f = _proj_norm(_attention(groups, slopes), attn_w_out, j, h, ffn_g, i)
            h, a = _dense_ffn(f, dense_w_gate, dense_w_up, dense_w_down, j, h, mix_g, i + 1)
        else:
            z = _mm_act(a, gm_w_in, j, 2 * GM_WIDTH, "gelu")
            y = _gm_spatial(z, gm_vg, gm_w_s, gm_bt, j)
            h = _proj(y, gm_w_out, j, h)
            idx, gates, rank, cnt = _router(h, ffn_g, i, router_wt, j)
            pos, src, tile_valid, item_expert, item_tiles, item_row0, meta = _dispatch_plan(
                idx, rank, cnt)
            xs = _dispatch(h, src, tile_valid, ffn_g, i)
            y = _grouped_ffn(xs, moe_w_gate, moe_w_up, moe_w_down, j,
                             item_expert, item_tiles, item_row0, meta, FFN_CHUNK)
            if i == DEPTH - 1:
                return _combine(h, y, pos, gates.T, final_g, 0, True).reshape(BATCH, SEQ, D_MODEL)
            h, a = _combine(h, y, pos, gates.T, mix_g, i + 1, False)
```

```python
import functools

import jax
import jax.numpy as jnp
from jax import lax
from jax.experimental import pallas as pl
from jax.experimental.pallas import tpu as pltpu

D_MODEL = 2048
BATCH = 2
SEQ = 4096
DEPTH = 4
N_TOK = BATCH * SEQ
HEAD_DIM = 128
N_HEADS = D_MODEL // HEAD_DIM
DIL_CONFIGS = ((128, 1), (512, 4), (2048, 16))
N_DIL = len(DIL_CONFIGS)
ATT_BLK = 128
ATT_SKEW = 1
QKV_GROUP_COLS = 3 * N_HEADS * HEAD_DIM
GM_CHUNK = 128
GM_WIDTH = D_MODEL
GM_GROUP_DIM = 128
GM_GROUPS = GM_WIDTH // GM_GROUP_DIM
D_FF = 7 * D_MODEL // 2
N_EXPERTS = 8
TOP_K = 2
EPS = 1e-6
NEG_INF = -1e30

LANES = 128
VMEM_LIMIT = 60 * 1024 * 1024
CAST_ROWS = 256

FFN_TILE = 128
FFN_TRIP_TILES = (8, 4, 2, 1)
DISPATCH_TILE = 256
FFN_CHUNK = 2304
FFN_TILES_PER_CHUNK = FFN_CHUNK // FFN_TILE
DENSE_CHUNK = 2048
FFN_TF = 256
DMA_ISSUE_UNROLL = 8
FFN_ZERO_ROWS = 64
MOE_MAX_ITEMS = -(-TOP_K * N_TOK // FFN_CHUNK) + N_EXPERTS
MOE_ROWS = TOP_K * N_TOK + N_EXPERTS * FFN_TILE + FFN_CHUNK
assert MOE_ROWS % DISPATCH_TILE == 0

F32 = jnp.float32
BF16 = jnp.bfloat16
I32 = jnp.int32


def _params(n_axes):
    return pltpu.CompilerParams(
        dimension_semantics=("arbitrary",) * n_axes, vmem_limit_bytes=VMEM_LIMIT)


def _cast_weight(w_ref, wb_ref):
    def body(c, carry):
        r = pl.multiple_of(c * CAST_ROWS, CAST_ROWS)
        wb_ref[pl.ds(r, CAST_ROWS), :] = w_ref[pl.ds(r, CAST_ROWS), :].astype(BF16)
        return carry
    lax.fori_loop(0, w_ref.shape[0] // CAST_ROWS, body, 0)


def _gelu_tanh(x):
    return 0.5 * x * (1.0 + jnp.tanh(0.7978845608028654 * (x + 0.044715 * (x * x * x))))


def _rms(x, g):
    ms = jnp.mean(x * x, axis=-1, keepdims=True)
    return (x * lax.rsqrt(ms + EPS)) * g


def _rms_kernel(x_ref, g_ref, o_ref):
    o_ref[...] = _rms(x_ref[...], g_ref[...]).astype(o_ref.dtype)


def _rmsnorm(h, g3, layer, out_dtype, tm=512):
    return pl.pallas_call(
        _rms_kernel,
        grid=(N_TOK // tm,),
        in_specs=[pl.BlockSpec((tm, D_MODEL), lambda i: (i, 0)),
                  pl.BlockSpec((None, 1, D_MODEL), lambda i: (layer, 0, 0))],
        out_specs=pl.BlockSpec((tm, D_MODEL), lambda i: (i, 0)),
        out_shape=jax.ShapeDtypeStruct((N_TOK, D_MODEL), out_dtype),
        compiler_params=_params(1),
        name="rmsnorm",
    )(h, g3)


def _w_spec(layer, k, tn, col_off):
    return pl.BlockSpec((None, k, tn), lambda j, i: (layer, 0, col_off + j))


def _mm_act_kernel(a_ref, w_ref, o_ref, wb_ref, *, act):
    @pl.when(pl.program_id(1) == 0)
    def _():
        _cast_weight(w_ref, wb_ref)
    acc = jnp.dot(a_ref[...], wb_ref[...], preferred_element_type=F32)
    if act == "gelu":
        acc = _gelu_tanh(acc)
    o_ref[...] = acc.astype(o_ref.dtype)


def _mm_act(a, w, layer, n_cols, act, tm=1024, tn=1024, col_off=0):
    k = a.shape[1]
    return pl.pallas_call(
        functools.partial(_mm_act_kernel, act=act),
        grid=(n_cols // tn, N_TOK // tm),
        in_specs=[pl.BlockSpec((tm, k), lambda j, i: (i, 0)),
                  _w_spec(layer, k, tn, col_off)],
        out_specs=pl.BlockSpec((tm, tn), lambda j, i: (i, j)),
        out_shape=jax.ShapeDtypeStruct((N_TOK, n_cols), BF16),
        scratch_shapes=[pltpu.VMEM((k, tn), BF16)],
        compiler_params=_params(2),
        name="mm_act",
    )(a, w)


def _mm_qkv_kernel(a_ref, w_ref, o_ref, wb_ref, *rest, dil):
    @pl.when(pl.program_id(1) == 0)
    def _():
        _cast_weight(w_ref, wb_ref)
    acc = jnp.dot(a_ref[...], wb_ref[...], preferred_element_type=F32)
    heads = o_ref.shape[0]
    if dil == 1:
        for hh in range(heads):
            o_ref[hh, 0] = acc[:, hh * HEAD_DIM:(hh + 1) * HEAD_DIM].astype(o_ref.dtype)
    else:
        acc_ref, = rest
        rows = acc_ref.shape[1] // dil
        for hh in range(heads):
            acc_ref[hh] = acc[:, hh * HEAD_DIM:(hh + 1) * HEAD_DIM]

        def regroup(hh, carry):
            for r in range(dil):
                o_ref[hh, r] = acc_ref[hh, pl.ds(r, rows, stride=dil), :].astype(o_ref.dtype)
            return carry
        lax.fori_loop(0, heads, regroup, 0, unroll=dil <= 4)


def _mm_qkv(a, w, layer, dil, col_off, tm=1024, tn=1024):
    k = a.shape[1]
    tiles_per_b = SEQ // tm
    heads = tn // HEAD_DIM
    scratch = [pltpu.VMEM((k, tn), BF16)]
    if dil > 1:
        scratch.append(pltpu.VMEM((heads, tm, HEAD_DIM), F32))
    out = pl.pallas_call(
        functools.partial(_mm_qkv_kernel, dil=dil),
        grid=(QKV_GROUP_COLS // tn, N_TOK // tm),
        in_specs=[pl.BlockSpec((tm, k), lambda j, i: (i, 0)),
                  _w_spec(layer, k, tn, col_off)],
        out_specs=pl.BlockSpec((None, heads, dil, tm // dil, HEAD_DIM),
                               lambda j, i: (i // tiles_per_b, j, 0, i % tiles_per_b, 0)),
        out_shape=jax.ShapeDtypeStruct(
            (BATCH, 3 * N_HEADS, dil, SEQ // dil, HEAD_DIM), BF16),
        scratch_shapes=scratch,
        compiler_params=_params(2),
        name="mm_qkv",
    )(a, w)
    return out.reshape(BATCH, 3 * N_HEADS, SEQ, HEAD_DIM)


def _proj_kernel(a_ref, w_ref, res_ref, o_ref, wb_ref):
    @pl.when(pl.program_id(1) == 0)
    def _():
        _cast_weight(w_ref, wb_ref)
    o_ref[...] = res_ref[...] + jnp.dot(a_ref[...], wb_ref[...], preferred_element_type=F32)


def _proj(a, w, layer, res, tm=512, tn=1024):
    k = a.shape[1]
    return pl.pallas_call(
        _proj_kernel,
        grid=(D_MODEL // tn, N_TOK // tm),
        in_specs=[pl.BlockSpec((tm, k), lambda j, i: (i, 0)),
                  _w_spec(layer, k, tn, 0),
                  pl.BlockSpec((tm, tn), lambda j, i: (i, j))],
        out_specs=pl.BlockSpec((tm, tn), lambda j, i: (i, j)),
        out_shape=jax.ShapeDtypeStruct((N_TOK, D_MODEL), F32),
        scratch_shapes=[pltpu.VMEM((k, tn), BF16)],
        compiler_params=_params(2),
        name="proj",
    )(a, w, res)


def _proj_norm_kernel(a_ref, w_ref, res_ref, g_ref, o_ref, f_ref, wb_ref):
    @pl.when(pl.program_id(1) == 0)
    def _():
        _cast_weight(w_ref, wb_ref)
    a = jnp.concatenate([a_ref[hh] for hh in range(a_ref.shape[0])], axis=1)
    out = res_ref[...] + jnp.dot(a, wb_ref[...], preferred_element_type=F32)
    o_ref[...] = out
    f_ref[...] = _rms(out, g_ref[...]).astype(f_ref.dtype)


def _proj_norm(a, w, layer, res, g3, g_layer, tm=256):
    tiles_per_b = SEQ // tm
    k = a.shape[1] * a.shape[3]
    row = pl.BlockSpec((tm, D_MODEL), lambda j, i: (i, 0))
    return pl.pallas_call(
        _proj_norm_kernel,
        grid=(1, N_TOK // tm),
        in_specs=[pl.BlockSpec((None, a.shape[1], tm, a.shape[3]),
                               lambda j, i: (i // tiles_per_b, 0, i % tiles_per_b, 0)),
                  _w_spec(layer, k, D_MODEL, 0),
                  row,
                  pl.BlockSpec((None, 1, D_MODEL), lambda j, i: (g_layer, 0, 0))],
        out_specs=[row, row],
        out_shape=[jax.ShapeDtypeStruct((N_TOK, D_MODEL), F32),
                   jax.ShapeDtypeStruct((N_TOK, D_MODEL), BF16)],
        scratch_shapes=[pltpu.VMEM((k, D_MODEL), BF16)],
        compiler_params=_params(2),
        name="proj_norm",
    )(a, w, res, g3)


def _attn_kernel(slopes_ref, *refs):
    qkv_refs = refs[:9]
    o_ref, tab_ref, oscr_ref, lscr_ref = refs[9:]
    h = pl.program_id(1)
    ii = lax.broadcasted_iota(I32, (ATT_BLK, 2 * ATT_BLK), 0)
    jj = lax.broadcasted_iota(I32, (ATT_BLK, 2 * ATT_BLK), 1)
    delta = ii + ATT_BLK - jj
    scale = HEAD_DIM ** -0.5
    nt = (((1,), (1,)), ((), ()))
    n_blocks = SEQ // ATT_BLK
    ones = jnp.ones((2 * ATT_BLK, HEAD_DIM), BF16)
    for g, (win, dil) in enumerate(DIL_CONFIGS):
        q_ref, k_ref, v_ref = qkv_refs[3 * g:3 * g + 3]
        nb = n_blocks // dil
        tab_ref[...] = jnp.where((delta >= 0) & (delta <= ATT_BLK),
                                 -slopes_ref[g, h] * (dil * delta).astype(F32), NEG_INF)
        s, m, p, done = {}, {}, {}, []

        def keys(c, nb=nb):
            first = c % nb == 0
            return (slice((c - (not first)) * ATT_BLK, (c + 1) * ATT_BLK),
                    slice(ATT_BLK if first else 0, 2 * ATT_BLK))

        def scores(c):
            rows, cols = keys(c)
            q = q_ref[c * ATT_BLK:(c + 1) * ATT_BLK, :]
            s[c] = lax.dot_general(q, k_ref[rows, :], nt,
                                   preferred_element_type=F32) * scale + tab_ref[:, cols]

        def softmax(c):
            m[c] = jnp.max(s[c], axis=-1, keepdims=True)
            p[c] = jnp.exp(s.pop(c) - m[c]).astype(BF16)

        def values(c):
            rows, _ = keys(c)
            vv = jnp.concatenate([v_ref[rows, :], ones[:rows.stop - rows.start]], axis=1)
            od = jnp.dot(p.pop(c), vv, preferred_element_type=F32)
            den = od[:, HEAD_DIM:]
            done.append((c, od[:, :HEAD_DIM] / den, m.pop(c) + jnp.log(den)))

        for step in range(n_blocks + 2 * ATT_SKEW):
            if step < n_blocks:
                scores(step)
            if 0 <= step - ATT_SKEW < n_blocks:
                softmax(step - ATT_SKEW)
            if 0 <= step - 2 * ATT_SKEW < n_blocks:
                values(step - 2 * ATT_SKEW)
        for c, o, lse in done:
            start = (c % nb) * (ATT_BLK * dil) + c // nb
            dst = pl.ds(start, ATT_BLK) if dil == 1 else pl.ds(start, ATT_BLK, stride=dil)
            oscr_ref[g, dst, :] = o
            lscr_ref[g, dst, :] = jnp.broadcast_to(lse, (ATT_BLK, HEAD_DIM))

    rows = 256

    def merge(t, carry):
        sl = pl.ds(pl.multiple_of(t * rows, rows), rows)
        l0 = lscr_ref[0, sl, :]
        l1 = lscr_ref[1, sl, :]
        l2 = lscr_ref[2, sl, :]
        m = jnp.maximum(jnp.maximum(l0, l1), l2)
        e0 = jnp.exp(l0 - m)
        e1 = jnp.exp(l1 - m)
        e2 = jnp.exp(l2 - m)
        den = e0 + e1 + e2
        o = (e0 * oscr_ref[0, sl, :] + e1 * oscr_ref[1, sl, :] + e2 * oscr_ref[2, sl, :]) / den
        o_ref[sl, :] = o.astype(o_ref.dtype)
        return carry

    lax.fori_loop(0, SEQ // rows, merge, 0)


def _attention(qkv_groups, slopes):
    in_specs = [pl.BlockSpec(memory_space=pltpu.SMEM)]
    args = [slopes]
    for qkv in qkv_groups:
        for part in range(3):
            in_specs.append(pl.BlockSpec(
                (None, None, SEQ, HEAD_DIM),
                lambda b, h, part=part: (b, part * N_HEADS + h, 0, 0)))
            args.append(qkv)
    return pl.pallas_call(
        _attn_kernel,
        grid=(BATCH, N_HEADS),
        in_specs=in_specs,
        out_specs=pl.BlockSpec((None, None, SEQ, HEAD_DIM), lambda b, h: (b, h, 0, 0)),
        out_shape=jax.ShapeDtypeStruct((BATCH, N_HEADS, SEQ, HEAD_DIM), BF16),
        scratch_shapes=[pltpu.VMEM((ATT_BLK, 2 * ATT_BLK), F32),
                        pltpu.VMEM((N_DIL, SEQ, HEAD_DIM), F32),
                        pltpu.VMEM((N_DIL, SEQ, HEAD_DIM), F32)],
        compiler_params=_params(2),
        name="dilated_attn",
    )(*args)


def _gm_spatial_kernel(z_ref, vg_ref, ws_ref, bt_ref, y_ref, wsb_ref):
    @pl.when(pl.program_id(0) == 0)
    def _():
        ii = lax.broadcasted_iota(I32, (GM_CHUNK, GM_CHUNK), 0)
        jj = lax.broadcasted_iota(I32, (GM_CHUNK, GM_CHUNK), 1)
        for g in range(GM_GROUPS):
            wsb_ref[g] = jnp.where(ii >= jj, ws_ref[g], 0.0).astype(BF16)

    for c in range(z_ref.shape[0] // GM_CHUNK):
        rows = slice(c * GM_CHUNK, (c + 1) * GM_CHUNK)
        vn = _rms(z_ref[rows, GM_WIDTH:].astype(F32), vg_ref[...]).astype(BF16)
        for g in range(GM_GROUPS):
            cols = slice(g * GM_GROUP_DIM, (g + 1) * GM_GROUP_DIM)
            s = jnp.dot(wsb_ref[g], vn[:, cols], preferred_element_type=F32) + bt_ref[:, g:g + 1]
            y_ref[rows, cols] = (z_ref[rows, cols].astype(F32) * s).astype(y_ref.dtype)


def _gm_spatial(z, vg3, ws, bt, layer, tm=256):
    return pl.pallas_call(
        _gm_spatial_kernel,
        grid=(N_TOK // tm,),
        in_specs=[pl.BlockSpec((tm, 2 * GM_WIDTH), lambda i: (i, 0)),
                  pl.BlockSpec((None, 1, GM_WIDTH), lambda i: (layer, 0, 0)),
                  pl.BlockSpec((None, GM_GROUPS, GM_CHUNK, GM_CHUNK), lambda i: (layer, 0, 0, 0)),
                  pl.BlockSpec((None, GM_CHUNK, GM_GROUPS), lambda i: (layer, 0, 0))],
        out_specs=pl.BlockSpec((tm, GM_WIDTH), lambda i: (i, 0)),
        out_shape=jax.ShapeDtypeStruct((N_TOK, GM_WIDTH), BF16),
        scratch_shapes=[pltpu.VMEM((GM_GROUPS, GM_CHUNK, GM_CHUNK), BF16)],
        compiler_params=_params(1),
        name="gm_spatial",
    )(z, vg3, ws, bt)


def _router_kernel(h_ref, g_ref, rwt_ref, idx_ref, gate_ref, rank_ref, cnt_ref, run_ref):
    @pl.when(pl.program_id(0) == 0)
    def _():
        run_ref[...] = jnp.zeros_like(run_ref)

    f = _rms(h_ref[...], g_ref[...])
    nt = (((1,), (1,)), ((), ()))
    w = rwt_ref[...]
    f_hi, w_hi = f.astype(BF16), w.astype(BF16)
    f_lo = (f - f_hi.astype(F32)).astype(BF16)
    w_lo = (w - w_hi.astype(F32)).astype(BF16)
    logits = (lax.dot_general(w_hi, f_hi, nt, preferred_element_type=F32)
              + (lax.dot_general(w_hi, f_lo, nt, preferred_element_type=F32)
                 + lax.dot_general(w_lo, f_hi, nt, preferred_element_type=F32)))
    tm = logits.shape[1]
    eid = lax.broadcasted_iota(I32, logits.shape, 0)
    m1 = jnp.max(logits, axis=0, keepdims=True)
    i1 = jnp.min(jnp.where(logits == m1, eid, N_EXPERTS), axis=0, keepdims=True)
    rest = jnp.where(eid == i1, -jnp.inf, logits)
    m2 = jnp.max(rest, axis=0, keepdims=True)
    i2 = jnp.min(jnp.where(rest == m2, eid, N_EXPERTS), axis=0, keepdims=True)
    e2 = jnp.exp(m2 - m1)
    den = 1.0 + e2
    idx_ref[0:1, :] = i1
    idx_ref[1:2, :] = i2
    gate_ref[0:1, :] = 1.0 / den
    gate_ref[1:2, :] = e2 / den

    sel1 = eid == i1
    sel2 = eid == i2
    onehot = jnp.where(sel1, 1.0, jnp.where(sel2, 1.0, 0.0))
    earlier = (lax.broadcasted_iota(I32, (tm, tm), 0)
               < lax.broadcasted_iota(I32, (tm, tm), 1))
    before = jnp.dot(onehot.astype(BF16), jnp.where(earlier, 1.0, 0.0).astype(BF16),
                     preferred_element_type=F32) + run_ref[:, 0:1]
    rank_ref[0:1, :] = jnp.sum(jnp.where(sel1, before, 0.0), axis=0, keepdims=True).astype(I32)
    rank_ref[1:2, :] = jnp.sum(jnp.where(sel2, before, 0.0), axis=0, keepdims=True).astype(I32)
    run_ref[...] = run_ref[...] + jnp.sum(onehot, axis=1, keepdims=True)
    cnt_ref[...] = run_ref[...]


def _router(h, g3, glayer, rwt, mlayer, tm=512):
    pair = pl.BlockSpec((TOP_K, tm), lambda i: (0, i))
    return pl.pallas_call(
        _router_kernel,
        grid=(N_TOK // tm,),
        in_specs=[pl.BlockSpec((tm, D_MODEL), lambda i: (i, 0)),
                  pl.BlockSpec((None, 1, D_MODEL), lambda i: (glayer, 0, 0)),
                  pl.BlockSpec((None, N_EXPERTS, D_MODEL), lambda i: (mlayer, 0, 0))],
        out_specs=[pair, pair, pair, pl.BlockSpec((N_EXPERTS, LANES), lambda i: (0, 0))],
        out_shape=[jax.ShapeDtypeStruct((TOP_K, N_TOK), I32),
                   jax.ShapeDtypeStruct((TOP_K, N_TOK), F32),
                   jax.ShapeDtypeStruct((TOP_K, N_TOK), I32),
                   jax.ShapeDtypeStruct((N_EXPERTS, LANES), F32)],
        scratch_shapes=[pltpu.VMEM((N_EXPERTS, LANES), F32)],
        compiler_params=_params(1),
        name="router",
    )(h, g3, rwt)


def _dispatch_plan(idx, rank, cnt):
    counts = cnt[:, 0].astype(I32)
    tiles_e = (counts + FFN_TILE - 1) // FFN_TILE
    items_e = (tiles_e + FFN_TILES_PER_CHUNK - 1) // FFN_TILES_PER_CHUNK
    items_end = jnp.cumsum(items_e)
    items_start = items_end - items_e
    n_items = items_end[-1]
    first_row = (jnp.cumsum(tiles_e) - tiles_e) * FFN_TILE
    used_rows = jnp.sum(tiles_e) * FFN_TILE
    pos = rank
    for e in range(N_EXPERTS):
        pos = pos + jnp.where(idx == e, first_row[e], 0)
    it = jnp.arange(MOE_MAX_ITEMS, dtype=I32)
    it_c = jnp.clip(it, 0, jnp.maximum(n_items - 1, 0))
    item_expert = jnp.minimum(jnp.searchsorted(items_end, it_c, side="right"),
                              N_EXPERTS - 1).astype(I32)
    local = it_c - items_start[item_expert]
    item_tiles = jnp.clip(tiles_e[item_expert] - local * FFN_TILES_PER_CHUNK,
                          0, FFN_TILES_PER_CHUNK)
    item_tiles = jnp.where(it < n_items, item_tiles, 0).astype(I32)
    item_row0 = (first_row[item_expert] + local * FFN_CHUNK).astype(I32)
    tok = jnp.broadcast_to(jnp.arange(N_TOK, dtype=I32), (TOP_K, N_TOK))
    src = jnp.zeros((MOE_ROWS,), I32).at[pos.reshape(-1)].set(tok.reshape(-1))
    tile_valid = (jnp.arange(MOE_ROWS // DISPATCH_TILE, dtype=I32) * DISPATCH_TILE
                  < used_rows).astype(I32)
    meta = jnp.stack([n_items, used_rows]).astype(I32)
    return pos, src, tile_valid, item_expert, item_tiles, item_row0, meta


def _row_copy(src_hbm, row, dst_ref, slot, r, sem_ref):
    return pltpu.make_async_copy(src_hbm.at[pl.ds(row, 1), :],
                                 dst_ref.at[slot, pl.ds(r, 1), :], sem_ref.at[slot])


def _dispatch_kernel(src_ref, valid_ref, h_hbm, g_ref, o_ref, buf_ref, sem_ref):
    p = pl.program_id(0)
    last = pl.num_programs(0) - 1

    def issue(tile, slot):
        def body(r, carry):
            _row_copy(h_hbm, src_ref[tile * DISPATCH_TILE + r], buf_ref, slot, r, sem_ref).start()
            return carry
        lax.fori_loop(0, DISPATCH_TILE, body, 0, unroll=DMA_ISSUE_UNROLL)

    @pl.when((p == 0) & (valid_ref[0] > 0))
    def _():
        issue(0, 0)

    nxt = jnp.minimum(p + 1, last)

    @pl.when((p < last) & (valid_ref[nxt] > 0))
    def _():
        issue(nxt, nxt % 2)

    slot = p % 2

    @pl.when(valid_ref[p] > 0)
    def _():
        for r in range(DISPATCH_TILE):
            _row_copy(h_hbm, 0, buf_ref, slot, r, sem_ref).wait()
        o_ref[...] = _rms(buf_ref[slot], g_ref[...]).astype(o_ref.dtype)

    @pl.when(valid_ref[p] == 0)
    def _():
        o_ref[...] = jnp.zeros_like(o_ref)


def _dispatch(h, src, tile_valid, g3, layer):
    return pl.pallas_call(
        _dispatch_kernel,
        grid_spec=pltpu.PrefetchScalarGridSpec(
            num_scalar_prefetch=2,
            grid=(MOE_ROWS // DISPATCH_TILE,),
            in_specs=[pl.BlockSpec(memory_space=pl.ANY),
                      pl.BlockSpec((None, 1, D_MODEL), lambda p, s, v: (layer, 0, 0))],
            out_specs=pl.BlockSpec((DISPATCH_TILE, D_MODEL), lambda p, s, v: (p, 0)),
            scratch_shapes=[pltpu.VMEM((2, DISPATCH_TILE, D_MODEL), F32),
                            pltpu.SemaphoreType.DMA((2,))]),
        out_shape=jax.ShapeDtypeStruct((MOE_ROWS, D_MODEL), BF16),
        compiler_params=_params(1),
        name="moe_dispatch",
    )(src, tile_valid, h, g3)


def _ffn_kernel(exp_ref, tiles_ref, row0_ref, meta_ref, x_hbm, wg_ref, wu_ref, wd_ref, *rest,
                fused):
    if fused:
        (res_hbm, g_ref, y_hbm, a_hbm, xbuf_ref, wgu_ref, wdb_ref, acc_ref, zero_ref, xsem_ref,
         sem_ref, zsem_ref, astage_ref, asem_ref, rsem_ref) = rest
    else:
        y_hbm, xbuf_ref, wgu_ref, wdb_ref, acc_ref, zero_ref, xsem_ref, sem_ref, zsem_ref = rest
    it = pl.program_id(0)
    j = pl.program_id(1)
    last_j = pl.num_programs(1) - 1
    n_tiles = tiles_ref[it]
    n_items = meta_ref[0]
    used_rows = meta_ref[1]
    tf = wg_ref.shape[1]
    chunk = xbuf_ref.shape[1]
    big = FFN_TRIP_TILES[0]
    row0 = row0_ref[it]
    prev_tiles = jnp.where(it > 0, tiles_ref[jnp.maximum(it - 1, 0)], 0)
    x_ref = xbuf_ref.at[it % 2]

    def x_copy(item):
        return pltpu.make_async_copy(
            x_hbm.at[pl.ds(pl.multiple_of(row0_ref[item], FFN_TILE), chunk), :],
            xbuf_ref.at[item % 2], xsem_ref.at[item % 2])

    def out_copy(i):
        r = pl.multiple_of(i * FFN_TILE, FFN_TILE)
        return pltpu.make_async_copy(
            acc_ref.at[pl.ds(r, FFN_TILE), :],
            y_hbm.at[pl.ds(pl.multiple_of(row0 + r, FFN_TILE), FFN_TILE), :], sem_ref.at[i])

    def res_copy(i):
        r = pl.multiple_of(i * FFN_TILE, FFN_TILE)
        return pltpu.make_async_copy(
            res_hbm.at[pl.ds(pl.multiple_of(row0 + r, FFN_TILE), FFN_TILE), :],
            acc_ref.at[pl.ds(r, FFN_TILE), :], rsem_ref.at[0])

    def a_copy(p):
        r = pl.multiple_of(row0 + p * (big * FFN_TILE), FFN_TILE)
        return pltpu.make_async_copy(astage_ref.at[p % 2],
                                     a_hbm.at[pl.ds(r, big * FFN_TILE), :], asem_ref.at[p % 2])

    def cast_weights():
        wgu_ref[:, :tf] = wg_ref[...].astype(BF16)
        wgu_ref[:, tf:] = wu_ref[...].astype(BF16)
        wdb_ref[...] = wd_ref[...].astype(BF16)

    def ffn_rows(start, n_rows, first, cast=False):
        rows = pl.ds(pl.multiple_of(start, FFN_TILE), n_rows)
        if cast:
            wgu_ref[:, :tf] = wg_ref[...].astype(BF16)
            gate = jnp.dot(x_ref[rows, :], wgu_ref[:, :tf], preferred_element_type=F32)
            wgu_ref[:, tf:] = wu_ref[...].astype(BF16)
            up = jnp.dot(x_ref[rows, :], wgu_ref[:, tf:], preferred_element_type=F32)
            wdb_ref[...] = wd_ref[...].astype(BF16)
        else:
            gu = jnp.dot(x_ref[rows, :], wgu_ref[...], preferred_element_type=F32)
            gate, up = gu[:, :tf], gu[:, tf:]
        act = ((gate * jax.nn.sigmoid(gate)) * up).astype(BF16)
        d = jnp.dot(act, wdb_ref[...], preferred_element_type=F32)
        if first:
            acc_ref[rows, :] = d
        else:
            acc_ref[rows, :] += d

    def wait_previous(i):
        @pl.when(i < prev_tiles)
        def _():
            out_copy(i).wait()

    def run(phase):
        first = phase == "first" and not fused
        last = phase == "last"

        def trip(tile0, tiles, cast=False):
            if first:
                for i in range(tiles):
                    wait_previous(tile0 + i)
            ffn_rows(tile0 * FFN_TILE, tiles * FFN_TILE, first, cast)
            if last:
                for i in range(tiles):
                    out_copy(tile0 + i).start()

        def big_trip(p, carry, cast=False):
            trip(pl.multiple_of(p * big, big), big, cast)
            if last and fused:
                @pl.when(p >= 2)
                def _():
                    a_copy(p - 2).wait()
                rows = pl.ds(pl.multiple_of(p * (big * FFN_TILE), big * FFN_TILE), big * FFN_TILE)
                astage_ref[p % 2] = _rms(acc_ref[rows, :], g_ref[...]).astype(BF16)
                a_copy(p).start()
            return carry
        n_big = n_tiles // big
        merge = (n_tiles - n_big * big == 1) & (n_big > 0) & (not fused)
        n_loop = n_big - jnp.where(merge, 1, 0)

        @pl.when(n_loop > 0)
        def _():
            big_trip(jnp.int32(0), 0, cast=True)

        @pl.when(n_loop == 0)
        def _():
            cast_weights()
        lax.fori_loop(1, n_loop, big_trip, 0)
        if not fused:
            @pl.when(merge)
            def _():
                trip(pl.multiple_of(n_loop * big, big), big + 1)
            done = jnp.where(merge, n_tiles, n_big * big)
            for tiles in FFN_TRIP_TILES[1:]:
                take = ((n_tiles - done) // tiles) > 0

                @pl.when(take)
                def _(done=done, tiles=tiles):
                    trip(pl.multiple_of(done, tiles), tiles)
                done = done + jnp.where(take, tiles, 0)

        def drain(i, carry):
            out_copy(i).wait()
            return carry
        if phase == "first":
            lax.fori_loop(n_tiles, jnp.maximum(prev_tiles, n_tiles), drain, 0)
        if last:
            @pl.when(it == n_items - 1)
            def _():
                lax.fori_loop(0, n_tiles, drain, 0)
            if fused:
                def drain_a(p, carry):
                    a_copy(p).wait()
                    return carry
                lax.fori_loop(jnp.maximum(n_big - 2, 0), n_big, drain_a, 0)

    zero_rows = zero_ref.shape[0]

    def zero_copy(k):
        row = pl.multiple_of(used_rows + k * zero_rows, zero_rows)
        return pltpu.make_async_copy(zero_ref, y_hbm.at[pl.ds(row, zero_rows), :], zsem_ref.at[0])

    @pl.when((it == n_items - 1) & (j == last_j))
    def _():
        zero_ref[...] = jnp.zeros_like(zero_ref)
        pieces = (y_hbm.shape[0] - used_rows) // zero_rows

        def start(k, carry):
            zero_copy(k).start()
            return carry

        def wait(k, carry):
            zero_copy(k).wait()
            return carry
        lax.fori_loop(0, pieces, start, 0)
        lax.fori_loop(0, pieces, wait, 0)

    @pl.when(n_tiles > 0)
    def _():
        @pl.when(j == 0)
        def _():
            @pl.when(it == 0)
            def _():
                x_copy(0).start()
            x_copy(it).wait()

            @pl.when(it + 1 < n_items)
            def _():
                x_copy(it + 1).start()
            if fused:
                def start(i, carry):
                    wait_previous(i)
                    res_copy(i).start()
                    return carry

                def wait(i, carry):
                    res_copy(i).wait()
                    return carry
                lax.fori_loop(0, n_tiles, start, 0)
                lax.fori_loop(0, n_tiles, wait, 0)
            run("first")

        @pl.when((j > 0) & (j < last_j))
        def _():
            run("middle")

        @pl.when(j == last_j)
        def _():
            run("last")


def _grouped_ffn(xs, wg, wu, wd, layer, item_expert, item_tiles, item_row0, meta, chunk,
                 residual=None):
    n_items = item_expert.shape[0]
    out_rows = xs.shape[0]
    nj = D_FF // FFN_TF
    fused = residual is not None

    def col(it, j, e, t, r, m):
        return jnp.where(t[it] > 0, j, nj - 1)

    in_specs = [
        pl.BlockSpec(memory_space=pl.ANY),
        pl.BlockSpec((None, None, D_MODEL, FFN_TF),
                     lambda it, j, e, t, r, m: (layer, e[it], 0, col(it, j, e, t, r, m))),
        pl.BlockSpec((None, None, D_MODEL, FFN_TF),
                     lambda it, j, e, t, r, m: (layer, e[it], 0, col(it, j, e, t, r, m))),
        pl.BlockSpec((None, None, FFN_TF, D_MODEL),
                     lambda it, j, e, t, r, m: (layer, e[it], col(it, j, e, t, r, m), 0)),
    ]
    args = [item_expert, item_tiles, item_row0, meta, xs, wg, wu, wd]
    out_specs = pl.BlockSpec(memory_space=pl.ANY)
    out_shape = jax.ShapeDtypeStruct((out_rows, D_MODEL), F32)
    scratch = [pltpu.VMEM((2, chunk, D_MODEL), BF16),
               pltpu.VMEM((D_MODEL, 2 * FFN_TF), BF16),
               pltpu.VMEM((FFN_TF, D_MODEL), BF16),
               pltpu.VMEM((chunk, D_MODEL), F32),
               pltpu.VMEM((FFN_ZERO_ROWS, D_MODEL), F32),
               pltpu.SemaphoreType.DMA((2,)),
               pltpu.SemaphoreType.DMA((chunk // FFN_TILE,)),
               pltpu.SemaphoreType.DMA((1,))]
    if fused:
        res, g3, g_layer = residual
        in_specs += [pl.BlockSpec(memory_space=pl.ANY),
                     pl.BlockSpec((None, 1, D_MODEL),
                                  lambda it, j, e, t, r, m: (g_layer, 0, 0))]
        args += [res, g3]
        out_specs = [out_specs, pl.BlockSpec(memory_space=pl.ANY)]
        out_shape = [out_shape, jax.ShapeDtypeStruct((out_rows, D_MODEL), BF16)]
        scratch += [pltpu.VMEM((2, FFN_TRIP_TILES[0] * FFN_TILE, D_MODEL), BF16),
                    pltpu.SemaphoreType.DMA((2,)),
                    pltpu.SemaphoreType.DMA((1,))]
    return pl.pallas_call(
        functools.partial(_ffn_kernel, fused=fused),
        grid_spec=pltpu.PrefetchScalarGridSpec(
            num_scalar_prefetch=4,
            grid=(n_items, nj),
            in_specs=in_specs,
            out_specs=out_specs,
            scratch_shapes=scratch),
        out_shape=out_shape,
        compiler_params=_params(2),
        name="grouped_ffn",
    )(*args)


def _dense_ffn(f, wg, wu, wd, layer, h, g3, g_layer):
    n_items = N_TOK // DENSE_CHUNK
    assert DENSE_CHUNK % (FFN_TRIP_TILES[0] * FFN_TILE) == 0
    return _grouped_ffn(
        f, wg[:, None], wu[:, None], wd[:, None], layer,
        jnp.zeros((n_items,), I32), jnp.full((n_items,), DENSE_CHUNK // FFN_TILE, I32),
        jnp.arange(n_items, dtype=I32) * DENSE_CHUNK, jnp.array([n_items, N_TOK], I32),
        DENSE_CHUNK, residual=(h, g3, g_layer))


def _combine_kernel(pos_ref, h_ref, gate_ref, g_ref, y_hbm, *rest, final):
    if final:
        o_ref, buf_ref, sem_ref = rest
    else:
        hn_ref, a_ref, buf_ref, sem_ref = rest
    t = pl.program_id(0)
    last = pl.num_programs(0) - 1
    tm = h_ref.shape[0]

    def issue(tile, slot):
        def body(r, carry):
            for k in range(TOP_K):
                _row_copy(y_hbm, pos_ref[k * N_TOK + tile * tm + r],
                          buf_ref, slot, k * tm + r, sem_ref).start()
            return carry
        lax.fori_loop(0, tm, body, 0, unroll=DMA_ISSUE_UNROLL)

    @pl.when(t == 0)
    def _():
        issue(0, 0)

    @pl.when(t < last)
    def _():
        issue(t + 1, (t + 1) % 2)

    slot = t % 2
    for r in range(TOP_K * tm):
        _row_copy(y_hbm, 0, buf_ref, slot, r, sem_ref).wait()
    hn = (h_ref[...] + gate_ref[:, 0:1] * buf_ref[slot, pl.ds(0, tm), :]
          + gate_ref[:, 1:2] * buf_ref[slot, pl.ds(tm, tm), :])
    if final:
        o_ref[...] = _rms(hn, g_ref[...])
    else:
        hn_ref[...] = hn
        a_ref[...] = _rms(hn, g_ref[...]).astype(a_ref.dtype)


def _combine(h, y, pos, gates_t, g3, layer, final, tm=256):
    row = pl.BlockSpec((tm, D_MODEL), lambda t, p: (t, 0))
    if final:
        out_specs = row
        out_shape = jax.ShapeDtypeStruct((N_TOK, D_MODEL), F32)
    else:
        out_specs = [row, row]
        out_shape = [jax.ShapeDtypeStruct((N_TOK, D_MODEL), F32),
                     jax.ShapeDtypeStruct((N_TOK, D_MODEL), BF16)]
    return pl.pallas_call(
        functools.partial(_combine_kernel, final=final),
        grid_spec=pltpu.PrefetchScalarGridSpec(
            num_scalar_prefetch=1,
            grid=(N_TOK // tm,),
            in_specs=[row,
                      pl.BlockSpec((tm, TOP_K), lambda t, p: (t, 0)),
                      pl.BlockSpec((None, 1, D_MODEL), lambda t, p: (layer, 0, 0)),
                      pl.BlockSpec(memory_space=pl.ANY)],
            out_specs=out_specs,
            scratch_shapes=[pltpu.VMEM((2, TOP_K * tm, D_MODEL), F32),
                            pltpu.SemaphoreType.DMA((2,))]),
        out_shape=out_shape,
        compiler_params=_params(1),
        name="moe_combine",
    )(pos.reshape(-1), h, gates_t, g3, y)


def _alibi_slopes():
    n = N_DIL * N_HEADS
    s = jnp.exp2(-8.0 * jnp.arange(1, n + 1, dtype=F32) / n)
    return s.reshape(N_HEADS, N_DIL).T


def kernel(x, mix_norm_g, ffn_norm_g, attn_w_in, attn_w_out, gm_w_in, gm_v_norm_g, gm_w_s,
           gm_b_s, gm_w_out, dense_w_gate, dense_w_up, dense_w_down, router_w, moe_w_gate,
           moe_w_up, moe_w_down, final_norm_g):
    h = x.reshape(N_TOK, D_MODEL)
    mix_g = mix_norm_g.reshape(DEPTH, 1, D_MODEL)
    ffn_g = ffn_norm_g.reshape(DEPTH, 1, D_MODEL)
    final_g = final_norm_g.reshape(1, 1, D_MODEL)
    gm_vg = gm_v_norm_g.reshape(-1, 1, GM_WIDTH)
    gm_bt = jnp.swapaxes(gm_b_s, 1, 2)
    router_wt = jnp.swapaxes(router_w, 1, 2)
    slopes = _alibi_slopes()

    a = _rmsnorm(h, mix_g, 0, BF16)
    for i in range(DEPTH):
        j = i // 2
        if i % 2 == 0:
            groups = [_mm_qkv(a, attn_w_in, j, dil, g * (QKV_GROUP_COLS // 1024))
                      for g, (win, dil) in enumerate(DIL_CONFIGS)]
            h, f = _proj_norm(_attention(groups, slopes), attn_w_out, j, h, ffn_g, i)
            h, a = _dense_ffn(f, dense_w_gate, dense_w_up, dense_w_down, j, h, mix_g, i + 1)
        else:
            z = _mm_act(a, gm_w_in, j, 2 * GM_WIDTH, "gelu")
            y = _gm_spatial(z, gm_vg, gm_w_s, gm_bt, j)
            h = _proj(y, gm_w_out, j, h)
            idx, gates, rank, cnt = _router(h, ffn_g, i, router_wt, j)
            pos, src, tile_valid, item_expert, item_tiles, item_row0, meta = _dispatch_plan(
                idx, rank, cnt)
            xs = _dispatch(h, src, tile_valid, ffn_g, i)
            y = _grouped_ffn(xs, moe_w_gate, moe_w_up, moe_w_down, j,
                             item_expert, item_tiles, item_row0, meta, FFN_CHUNK)
            if i == DEPTH - 1:
                return _combine(h, y, pos, gates.T, final_g, 0, True).reshape(BATCH, SEQ, D_MODEL)
            h, a = _combine(h, y, pos, gates.T, mix_g, i + 1, False)
```
